```python
import jax, jax.numpy as jnp
from jax import lax
import numpy as np


D_MODEL = 2048
BATCH = 4
SEQ = 2048
DEPTH = 1

RWKV_WIDTH = D_MODEL // 2
HEAD_SIZE = 64
N_RWKV_HEADS = RWKV_WIDTH // HEAD_SIZE
DECAY_LORA = 64
ICLR_LORA = 64
GATE_LORA = 160
GN_EPS = 64e-5
POOL_WIDTH = D_MODEL // 2
POOL_WINDOWS = (2, 4, 8, 16)
N_POOL_GROUPS = len(POOL_WINDOWS)
POOL_GROUP = POOL_WIDTH // N_POOL_GROUPS
POOL_OUT_GROUP = D_MODEL // N_POOL_GROUPS
N_SHIFT = 3 * RWKV_WIDTH + DECAY_LORA + ICLR_LORA + GATE_LORA
N_IN = N_SHIFT + POOL_WIDTH + 2 * D_MODEL
N_EXPERTS = 32
TOP_K = 4
D_FF = D_MODEL
SWIGLU_LIMIT = 7.0
SWIGLU_ALPHA = 1.702
MOE_BLOCK = 256
LN_EPS = 1e-5
DEEPNORM_ALPHA = (2 * DEPTH) ** 0.25
DEEPNORM_BETA = (8 * DEPTH) ** -0.25

kernel_name = 'hybrid_rwkv7_pool_moe_deepnorm_adaln'

F32 = jnp.float32


def layer_norm(x, g, b):
    xf = x.astype(F32)
    mu = jnp.mean(xf, axis=-1, keepdims=True)
    var = jnp.mean(jnp.square(xf - mu), axis=-1, keepdims=True)
    return ((xf - mu) * lax.rsqrt(var + LN_EPS) * g + b).astype(x.dtype)


def token_shift(u):
    return jnp.pad(u, ((0, 0), (1, 0), (0, 0)))[:, :-1]


def wkv7_scan(r, w, k, v, a, b):
    bsz, _, h, n = r.shape

    def step(S, inp):
        r_t, w_t, k_t, v_t, a_t, b_t = inp
        sa = jnp.einsum('bhvk,bhk->bhv', S, a_t)
        S = S * w_t[:, :, None, :] + sa[..., None] * b_t[:, :, None, :] + v_t[..., None] * k_t[:, :, None, :]
        return S, jnp.einsum('bhvk,bhk->bhv', S, r_t)

    xs = tuple(jnp.moveaxis(t, 1, 0) for t in (r, w, k, v, a, b))
    _, y = lax.scan(step, jnp.zeros((bsz, h, n, n), F32), xs)
    return jnp.moveaxis(y, 0, 1)


def rwkv7_branch(p, w0, w_decay_up, a0, w_iclr_up, w_glora_up, k_k, k_a, r_k, lnx_g, lnx_b):
    bsz, t, _ = p.shape
    C = RWKV_WIDTH
    r, k, v = p[..., :C], p[..., C:2 * C], p[..., 2 * C:3 * C]
    o = 3 * C
    wd = p[..., o:o + DECAY_LORA]; o += DECAY_LORA
    ad = p[..., o:o + ICLR_LORA]; o += ICLR_LORA
    gd = p[..., o:o + GATE_LORA]
    heads = lambda z: z.reshape(bsz, t, N_RWKV_HEADS, HEAD_SIZE)
    w_log = -jax.nn.softplus(-(w0 + jnp.tanh(wd) @ w_decay_up).astype(F32)) - 0.5
    decay = jnp.exp(-jnp.exp(w_log))
    iclr = jax.nn.sigmoid((a0 + ad @ w_iclr_up).astype(F32))
    g = jax.nn.sigmoid(gd) @ w_glora_up
    kk = heads((k * k_k).astype(F32))
    kk = kk / jnp.maximum(jnp.sqrt(jnp.sum(kk * kk, axis=-1, keepdims=True)), 1e-12)
    k_mod = k.astype(F32) * (1.0 + (iclr - 1.0) * k_a.astype(F32))
    rh, kh, vh = heads(r.astype(F32)), heads(k_mod), heads(v.astype(F32))
    y = wkv7_scan(rh, heads(decay), kh, vh, -kk, kk * heads(iclr))
    mu = jnp.mean(y, axis=-1, keepdims=True)
    var = jnp.mean(jnp.square(y - mu), axis=-1, keepdims=True)
    yn = ((y - mu) * lax.rsqrt(var + GN_EPS)).reshape(bsz, t, C) * lnx_g + lnx_b
    bonus = jnp.sum(rh * kh * r_k.astype(F32), axis=-1, keepdims=True) * vh
    out = (yn + bonus.reshape(bsz, t, C)) * g
    return out.astype(p.dtype)


def pool_branch(u, w_pool, pool_scale):
    bsz, t, _ = u.shape
    uf = u.astype(F32).reshape(bsz, t, N_POOL_GROUPS, POOL_GROUP)
    cs = jnp.cumsum(uf, axis=1)
    pos = jnp.arange(t)
    outs = []
    for gi, win in enumerate(POOL_WINDOWS):
        c_g = cs[:, :, gi]
        lagged = jnp.pad(c_g, ((0, 0), (win, 0), (0, 0)))[:, :t]
        cnt = jnp.minimum(pos + 1, win).astype(F32)
        outs.append((c_g - lagged) / cnt[None, :, None] - uf[:, :, gi])
    d = jnp.stack(outs, axis=2).astype(u.dtype)
    y = jnp.einsum('btgc,gcd->btgd', d, w_pool).reshape(bsz, t, D_MODEL)
    return y * pool_scale


def moe_ffn(h, w_router, b_router, w_gu, b_gu, w_down, b_down):
    bsz, t, d = h.shape
    m = bsz * t
    mk = m * TOP_K
    n_blocks = -(-mk // MOE_BLOCK) + N_EXPERTS
    hf = h.reshape(m, d)
    logits = (hf @ w_router).astype(F32) + b_router.astype(F32)
    top_val, top_idx = lax.top_k(logits, TOP_K)
    probs = jax.nn.softmax(top_val, axis=-1)
    flat_e = top_idx.reshape(mk)
    order = jnp.argsort(flat_e)
    sorted_e = flat_e[order]
    counts = jnp.bincount(flat_e, length=N_EXPERTS)
    padded = (counts + MOE_BLOCK - 1) // MOE_BLOCK * MOE_BLOCK
    pad_end = jnp.cumsum(padded)
    pad_start = pad_end - padded
    grp_start = jnp.cumsum(counts) - counts
    rank = jnp.arange(mk) - grp_start[sorted_e]
    dest_sorted = (pad_start[sorted_e] + rank).astype(jnp.int32)
    dest = jnp.zeros((mk,), jnp.int32).at[order].set(dest_sorted)
    tok = jnp.arange(mk) // TOP_K
    x_buf = jnp.zeros((n_blocks * MOE_BLOCK, d), h.dtype).at[dest].set(hf[tok])
    block_expert = jnp.minimum(
        jnp.searchsorted(pad_end, jnp.arange(n_blocks) * MOE_BLOCK, side='right'), N_EXPERTS - 1)

    def expert_block(args):
        xb, e = args
        gu = xb @ w_gu[e] + b_gu[e]
        gate = jnp.minimum(gu[:, ::2], SWIGLU_LIMIT)
        up = jnp.clip(gu[:, 1::2], -SWIGLU_LIMIT, SWIGLU_LIMIT)
        glu = gate * jax.nn.sigmoid(gate * SWIGLU_ALPHA)
        return ((up + 1.0) * glu) @ w_down[e] + b_down[e]

    y_buf = lax.map(expert_block, (x_buf.reshape(n_blocks, MOE_BLOCK, d), block_expert))
    y = y_buf.reshape(n_blocks * MOE_BLOCK, d)[dest].reshape(m, TOP_K, d)
    out = jnp.einsum('mkd,mk->md', y, probs.astype(y.dtype))
    return out.reshape(bsz, t, d)


def setup_inputs(seed: int = 0) -> dict:
    key = jax.random.key(seed)
    ks = jax.random.split(key, 32)
    L, D, C = DEPTH, D_MODEL, RWKV_WIDTH
    nrm = lambda k, shape, s: jax.random.normal(k, shape, F32) * s
    return {
        'x': nrm(ks[0], (BATCH, SEQ, D), 1.0),
        'c': nrm(ks[1], (BATCH, D), 1.0),
        'w_ada': nrm(ks[2], (L, D, 6 * D), 0.5 * D ** -0.5),
        'b_ada': nrm(ks[3], (L, 6 * D), 0.02),
        'w_in': nrm(ks[4], (L, D, N_IN), D ** -0.5),
        'mu_shift': jax.random.uniform(ks[5], (L, N_SHIFT), F32),
        'w0': jax.random.uniform(ks[6], (L, C), F32, -3.0, 1.0),
        'w_decay_up': nrm(ks[7], (L, DECAY_LORA, C), 0.5 * DECAY_LORA ** -0.5),
        'a0': nrm(ks[8], (L, C), 0.5),
        'w_iclr_up': nrm(ks[9], (L, ICLR_LORA, C), ICLR_LORA ** -0.5),
        'w_glora_up': nrm(ks[10], (L, GATE_LORA, C), GATE_LORA ** -0.5),
        'k_k': 0.85 + nrm(ks[11], (L, C), 0.02),
        'k_a': 1.0 + nrm(ks[12], (L, C), 0.02),
        'r_k': nrm(ks[13], (L, N_RWKV_HEADS, HEAD_SIZE), 0.1),
        'lnx_g': 1.0 + nrm(ks[14], (L, C), 0.02),
        'lnx_b': nrm(ks[15], (L, C), 0.02),
        'w_rwkv_out': nrm(ks[16], (L, C, D), C ** -0.5),
        'w_pool': nrm(ks[17], (L, N_POOL_GROUPS, POOL_GROUP, POOL_OUT_GROUP), POOL_GROUP ** -0.5),
        'pool_scale': 1.0 + nrm(ks[18], (L, D), 0.02),
        'w_out': nrm(ks[19], (L, D, D), DEEPNORM_BETA * D ** -0.5),
        'ln1_g': 1.0 + nrm(ks[20], (L, D), 0.02),
        'ln1_b': nrm(ks[21], (L, D), 0.02),
        'w_router': nrm(ks[22], (L, D, N_EXPERTS), D ** -0.5),
        'b_router': nrm(ks[23], (L, N_EXPERTS), 0.01),
        'w_gu': nrm(ks[24], (L, N_EXPERTS, D, 2 * D_FF), D ** -0.5),
        'b_gu': nrm(ks[25], (L, N_EXPERTS, 2 * D_FF), 0.02),
        'w_down': nrm(ks[26], (L, N_EXPERTS, D_FF, D), DEEPNORM_BETA * D_FF ** -0.5),
        'b_down': nrm(ks[27], (L, N_EXPERTS, D), 0.02),
        'ln2_g': 1.0 + nrm(ks[28], (L, D), 0.02),
        'ln2_b': nrm(ks[29], (L, D), 0.02),
    }


def reference(x, c, w_ada, b_ada, w_in, mu_shift, w0, w_decay_up, a0, w_iclr_up, w_glora_up,
              k_k, k_a, r_k, lnx_g, lnx_b, w_rwkv_out, w_pool, pool_scale, w_out, ln1_g, ln1_b,
              w_router, b_router, w_gu, b_gu, w_down, b_down, ln2_g, ln2_b):
    cond = jax.nn.silu(c)
    for l in range(DEPTH):
        mod = cond @ w_ada[l] + b_ada[l]
        sh1, sc1, g1, sh2, sc2, g2 = jnp.split(mod, 6, axis=-1)
        h = x * (1.0 + sc1[:, None]) + sh1[:, None]
        p = h @ w_in[l]
        ps = p[..., :N_SHIFT]
        ps = ps + (token_shift(ps) - ps) * mu_shift[l]
        y_rwkv = rwkv7_branch(ps, w0[l], w_decay_up[l], a0[l], w_iclr_up[l], w_glora_up[l],
                              k_k[l], k_a[l], r_k[l], lnx_g[l], lnx_b[l])
        o = N_SHIFT
        u_pool = p[..., o:o + POOL_WIDTH]
        o += POOL_WIDTH
        gate_a = jax.nn.sigmoid(p[..., o:o + D_MODEL])
        gate_b = jax.nn.sigmoid(p[..., o + D_MODEL:])
        branch_a = y_rwkv @ w_rwkv_out[l]
        branch_b = pool_branch(u_pool, w_pool[l], pool_scale[l])
        mix = (gate_a * branch_a + gate_b * branch_b) @ w_out[l]
        x = layer_norm(DEEPNORM_ALPHA * x + g1[:, None] * mix, ln1_g[l], ln1_b[l])
        h2 = x * (1.0 + sc2[:, None]) + sh2[:, None]
        f = moe_ffn(h2, w_router[l], b_router[l], w_gu[l], b_gu[l], w_down[l], b_down[l])
        x = layer_norm(DEEPNORM_ALPHA * x + g2[:, None] * f, ln2_g[l], ln2_b[l])
    return x
```

```python
import functools

import jax
import jax.numpy as jnp
from jax import lax
from jax.experimental import pallas as pl
from jax.experimental.pallas import tpu as pltpu

F32 = jnp.float32
BF16 = jnp.bfloat16

HEAD_SIZE = 64
LANES = 128
TOP_K = 4
LN_EPS = 1e-5
GN_EPS = 64e-5
POOL_WINDOWS = (2, 4, 8, 16)
SWIGLU_LIMIT = 7.0
SWIGLU_ALPHA = 1.702
WKV_CHUNK = 64
MOE_SUB = 256
MOE_RMAX = 1024
MOE_TF = 256
VMEM_LIMIT = 56 * 1024 * 1024

_NT = (((1,), (1,)), ((), ()))
_TN = (((0,), (0,)), ((), ()))


def _cparams(*sem):
    return pltpu.CompilerParams(dimension_semantics=sem, vmem_limit_bytes=VMEM_LIMIT)


def _sigmoid(z):
    return 1.0 / (1.0 + jnp.exp(-z))


def _dot(a, b):
    return jnp.dot(a, b, preferred_element_type=F32)


def _split_dot(xv, w):
    hi = xv.astype(BF16)
    lo = (xv - hi.astype(F32)).astype(BF16)
    return _dot(hi, w) + _dot(lo, w)


def _head_sum(xv, ones_bd):
    outs = []
    for p in range(xv.shape[1] // LANES):
        outs.append(_split_dot(xv[:, p * LANES:(p + 1) * LANES], ones_bd))
    return jnp.concatenate(outs, axis=1)


def _adaln_kernel(c_ref, w_ref, b_ref, o_ref):
    c = c_ref[...]
    cond = c * _sigmoid(c)
    o_ref[...] = _dot(cond.astype(BF16), w_ref[...].astype(BF16)) + b_ref[...]


def _adaln(c8, w, b):
    d, n = w.shape
    tn = 1024
    return pl.pallas_call(
        _adaln_kernel,
        grid=(n // tn,),
        in_specs=[pl.BlockSpec((8, d), lambda j: (0, 0)),
                  pl.BlockSpec((d, tn), lambda j: (0, j)),
                  pl.BlockSpec((1, tn), lambda j: (0, j))],
        out_specs=pl.BlockSpec((8, tn), lambda j: (0, j)),
        out_shape=jax.ShapeDtypeStruct((8, n), F32),
        compiler_params=_cparams("arbitrary"),
        name="adaln",
    )(c8, w, b.reshape(1, n))


def _inproj_kernel(x_ref, sc_ref, sh_ref, w_ref, o_ref, h_scr):
    @pl.when(pl.program_id(1) == 0)
    def _():
        h_scr[...] = (x_ref[...] * (1.0 + sc_ref[0]) + sh_ref[0]).astype(BF16)

    o_ref[...] = _dot(h_scr[...], w_ref[...])


def _inproj(x2, sc, sh, w, seq):
    m, d = x2.shape
    n = w.shape[1]
    tm, tn = 1024, 1024
    tpb = seq // tm
    return pl.pallas_call(
        _inproj_kernel,
        grid=(m // tm, n // tn),
        in_specs=[pl.BlockSpec((tm, d), lambda i, j: (i, 0)),
                  pl.BlockSpec((1, 1, d), lambda i, j: (i // tpb, 0, 0)),
                  pl.BlockSpec((1, 1, d), lambda i, j: (i // tpb, 0, 0)),
                  pl.BlockSpec((d, tn), lambda i, j: (0, j))],
        out_specs=pl.BlockSpec((tm, tn), lambda i, j: (i, j)),
        out_shape=jax.ShapeDtypeStruct((m, n), F32),
        scratch_shapes=[pltpu.VMEM((tm, d), BF16)],
        compiler_params=_cparams("arbitrary", "arbitrary"),
        name="inproj",
    )(x2, sc, sh, w)


def _prep_kernel(p_ref, prev_ref, mu_ref, w0_ref, a0_ref, kk_ref, ka_ref, wd_ref, wa_ref, wg_ref,
                 ones_ref, r_out, lw_out, k_out, v_out, kkn_out, b_out, g_out, *, tpb, c):
    cur = p_ref[...]
    first = (pl.program_id(0) % tpb) == 0
    prev_row = jnp.where(first, 0.0, prev_ref[7:8, :])
    shifted = pltpu.roll(cur, 1, axis=0)
    row0 = lax.broadcasted_iota(jnp.int32, cur.shape, 0) == 0
    shifted = jnp.where(row0, prev_row, shifted)
    ps = cur + (shifted - cur) * mu_ref[...]
    r = ps[:, 0:c]
    k = ps[:, c:2 * c]
    v = ps[:, 2 * c:3 * c]
    o = 3 * c
    wdp = ps[:, o:o + LANES]
    adp = ps[:, o + LANES:o + 2 * LANES]
    gdp = ps[:, o + 2 * LANES:o + 4 * LANES]
    z = w0_ref[...] + _dot(jnp.tanh(wdp).astype(BF16), wd_ref[...])
    lw = -jnp.exp(-0.5) * _sigmoid(z)
    iclr = _sigmoid(a0_ref[...] + _dot(adp.astype(BF16), wa_ref[...]))
    g = _dot(_sigmoid(gdp).astype(BF16), wg_ref[...])
    kk0 = k * kk_ref[...]
    ss = _head_sum(kk0 * kk0, ones_ref[...])
    kkn = kk0 / jnp.maximum(jnp.sqrt(ss), 1e-12)
    kmod = k * (1.0 + (iclr - 1.0) * ka_ref[...])
    r_out[...] = r
    lw_out[...] = lw
    k_out[...] = kmod
    v_out[...] = v
    kkn_out[...] = kkn
    b_out[...] = kkn * iclr
    g_out[...] = g


def _prep(p, mu, w0, a0, k_k, k_a, wd, wa, wg, ones_bd, seq, c, shift_blk):
    m = p.shape[0]
    ns = mu.shape[1]
    tm = 256
    tpb = seq // tm
    row = lambda a: a.reshape(1, -1)
    full = lambda a: pl.BlockSpec(a.shape, lambda i: (0,) * a.ndim)
    out = jax.ShapeDtypeStruct((m, c), F32)
    ospec = pl.BlockSpec((tm, c), lambda i: (i, 0))
    args = (p, p, mu, row(w0), row(a0), row(k_k), row(k_a), wd, wa, wg, ones_bd)
    in_specs = [pl.BlockSpec((tm, ns), lambda i: (i, shift_blk)),
                pl.BlockSpec((8, ns), lambda i: (jnp.maximum(i * (tm // 8) - 1, 0), shift_blk))]
    in_specs += [full(a) for a in args[2:]]
    return pl.pallas_call(
        functools.partial(_prep_kernel, tpb=tpb, c=c),
        grid=(m // tm,),
        in_specs=in_specs,
        out_specs=[ospec] * 7,
        out_shape=[out] * 7,
        compiler_params=_cparams("arbitrary"),
        name="rwkv_prep",
    )(*args)


def _wkv_kernel(r_ref, lw_ref, k_ref, v_ref, kk_ref, b_ref, y_ref, s_scr, *, chunk, c):
    L = chunk

    @pl.when(pl.program_id(1) == 0)
    def _():
        s_scr[...] = jnp.zeros_like(s_scr)

    ri = lax.broadcasted_iota(jnp.int32, (L, L), 0)
    ci = lax.broadcasted_iota(jnp.int32, (L, L), 1)
    tri = (ri >= ci).astype(BF16)
    lw = lw_ref[...]
    cs = _split_dot_lhs(tri, lw)
    cs_last = cs[L - 1:L, :]
    g_in = jnp.exp(cs)
    g_ex = jnp.exp(cs - lw)
    g_inv = jnp.exp(-cs)
    g_tail = jnp.exp(cs_last - cs)
    g_last = jnp.exp(cs_last)
    kk = kk_ref[...]
    bb = b_ref[...]
    kx = k_ref[...]
    vv = v_ref[...]
    a_t = -kk * g_ex
    r_t = r_ref[...] * g_in
    b_t = bb * g_inv
    k_t = kx * g_inv
    b_p = bb * g_tail
    k_p = kx * g_tail

    r2 = lax.broadcasted_iota(jnp.int32, (2 * L, 2 * L), 0)
    c2 = lax.broadcasted_iota(jnp.int32, (2 * L, 2 * L), 1)
    same = (r2 >= L) == (c2 >= L)
    m_strict = same & (r2 > c2)
    m_incl = same & (r2 >= c2)
    lane = lax.broadcasted_iota(jnp.int32, (L, LANES), 1)
    low = lane < HEAD_SIZE

    def stack(zv):
        return jnp.concatenate([jnp.where(low, zv, 0.0), jnp.where(low, 0.0, zv)], axis=0)

    for p in range(c // LANES):
        sl = slice(p * LANES, (p + 1) * LANES)
        a_s = stack(a_t[:, sl])
        r_s = stack(r_t[:, sl])
        b_s = stack(b_t[:, sl])
        k_s = stack(k_t[:, sl])
        v_s = stack(vv[:, sl])
        bp_s = stack(b_p[:, sl])
        kp_s = stack(k_p[:, sl])
        ar = jnp.concatenate([a_s, r_s], axis=0).astype(BF16)
        qb = lax.dot_general(ar, b_s.astype(BF16), _NT, preferred_element_type=F32)
        qk = lax.dot_general(ar, k_s.astype(BF16), _NT, preferred_element_type=F32)
        a_ab = jnp.where(m_strict, qb[:2 * L], 0.0)
        a_rb = jnp.where(m_incl, qb[2 * L:], 0.0)
        a_ak = jnp.where(m_strict, qk[:2 * L], 0.0)
        a_rk = jnp.where(m_incl, qk[2 * L:], 0.0)
        v_b = v_s.astype(BF16)
        z = jnp.concatenate([_dot(a_ak.astype(BF16), v_b), a_s], axis=1)
        amat = a_ab
        n_sq = max(L.bit_length() - 1, 1)
        for it in range(n_sq):
            ab = amat.astype(BF16)
            z = z + _dot(ab, z.astype(BF16))
            if it + 1 < n_sq:
                amat = _dot(ab, ab)
        u_loc = z[:, :LANES]
        a_pr = z[:, LANES:]
        s_old = s_scr[p]
        s_b = s_old.astype(BF16)
        u_s = _dot(a_pr.astype(BF16), s_b) + u_loc
        uv = jnp.concatenate([u_s, v_s], axis=0).astype(BF16)
        y_s = _dot(r_s.astype(BF16), s_b) + _dot(jnp.concatenate([a_rb, a_rk], axis=1).astype(BF16), uv)
        y_ref[:, sl] = y_s[:L] + y_s[L:]
        bk = jnp.concatenate([bp_s, kp_s], axis=0).astype(BF16)
        upd = lax.dot_general(bk, uv, _TN, preferred_element_type=F32)
        g_col = jnp.transpose(jnp.broadcast_to(g_last[:, sl], (LANES, LANES)))
        s_scr[p] = s_old * g_col + upd


def _split_dot_lhs(w, xv):
    hi = xv.astype(BF16)
    lo = (xv - hi.astype(F32)).astype(BF16)
    return _dot(w, hi) + _dot(w, lo)


def _wkv(r, lw, k, v, kk, b, bsz, seq, c):
    L = WKV_CHUNK
    nc = seq // L
    spec = pl.BlockSpec((L, c), lambda bi, ci: (bi * nc + ci, 0))
    return pl.pallas_call(
        functools.partial(_wkv_kernel, chunk=L, c=c),
        grid=(bsz, nc),
        in_specs=[spec] * 6,
        out_specs=spec,
        out_shape=jax.ShapeDtypeStruct((bsz * seq, c), F32),
        scratch_shapes=[pltpu.VMEM((c // LANES, LANES, LANES), F32)],
        compiler_params=_cparams("arbitrary", "arbitrary"),
        name="wkv7",
    )(r, lw, k, v, kk, b)


def _post_kernel(y_ref, r_ref, k_ref, v_ref, g_ref, gate_ref, pool_ref, halo_ref, x_ref,
                 g1_ref, sc2_ref, sh2_ref, rk_ref, lng_ref, lnb_ref, ones_ref, wro_ref, wp_ref,
                 psc_ref, wo_ref, l1g_ref, l1b_ref, wr_ref, br_ref,
                 x1_out, h2_out, lg_out, pool_scr, *, tpb, alpha, d, tm):
    ones_bd = ones_ref[...]
    inv_n = 1.0 / HEAD_SIZE
    y = y_ref[...]
    mu = _head_sum(y, ones_bd) * inv_n
    dy = y - mu
    var = _head_sum(dy * dy, ones_bd) * inv_n
    yn = dy * lax.rsqrt(var + GN_EPS) * lng_ref[...] + lnb_ref[...]
    vv = v_ref[...]
    bonus = _head_sum(r_ref[...] * k_ref[...] * rk_ref[...], ones_bd) * vv
    y_rwkv = ((yn + bonus) * g_ref[...]).astype(BF16)
    branch_a = _dot(y_rwkv, wro_ref[...])

    first = (pl.program_id(0) % tpb) == 0
    halo = 16
    pool_scr[0:halo, :] = jnp.where(first, 0.0, halo_ref[...])
    pool_scr[halo:, :] = pool_ref[...]
    pos = (pl.program_id(0) % tpb) * tm + lax.broadcasted_iota(jnp.int32, (tm, 1), 0)
    gw = pool_ref.shape[1] // len(POOL_WINDOWS)
    dg = d // len(POOL_WINDOWS)
    outs = []
    for gi, win in enumerate(POOL_WINDOWS):
        cols = slice(gi * gw, (gi + 1) * gw)
        u = pool_scr[halo:, cols]
        wsum = u
        for s in range(1, win):
            wsum = wsum + pool_scr[halo - s:halo - s + tm, cols]
        cnt = jnp.minimum(pos + 1, win).astype(F32)
        dd = (wsum / cnt - u).astype(BF16)
        outs.append(_dot(dd, wp_ref[gi]))
    branch_b = jnp.concatenate(outs, axis=1) * psc_ref[...]

    gates = gate_ref[...]
    mixin = _sigmoid(gates[:, :d]) * branch_a + _sigmoid(gates[:, d:]) * branch_b
    mix = _dot(mixin.astype(BF16), wo_ref[...])
    xr = alpha * x_ref[...] + g1_ref[0] * mix
    m1 = jnp.mean(xr, axis=-1, keepdims=True)
    xc = xr - m1
    v1 = jnp.mean(xc * xc, axis=-1, keepdims=True)
    x1 = xc * lax.rsqrt(v1 + LN_EPS) * l1g_ref[...] + l1b_ref[...]
    x1_out[...] = x1
    h2 = x1 * (1.0 + sc2_ref[0]) + sh2_ref[0]
    h2_out[...] = h2
    lg_out[...] = _dot(h2.astype(BF16), wr_ref[...]) + br_ref[...]


def _post(y, r, k, v, g, p, x2, g1, sc2, sh2, r_k, lnx_g, lnx_b, ones_bd, w_ro, w_pool, pool_scale,
          w_out, ln1_g, ln1_b, w_r, b_r, seq, alpha, gate_blk, pool_blk):
    m, d = x2.shape
    c = y.shape[1]
    pw = w_pool.shape[0] * w_pool.shape[1]
    tm = 256
    tpb = seq // tm
    row = lambda a: a.reshape(1, -1)
    const = lambda a: pl.BlockSpec(a.shape, lambda i: (0,) * a.ndim, pipeline_mode=pl.Buffered(1))
    tile = lambda w: pl.BlockSpec((tm, w), lambda i: (i, 0))
    bvec = pl.BlockSpec((1, 1, d), lambda i: (i // tpb, 0, 0))
    consts = (row(r_k), row(lnx_g), row(lnx_b), ones_bd, w_ro, w_pool, row(pool_scale), w_out,
              row(ln1_g), row(ln1_b), w_r, row(b_r))
    in_specs = [tile(c)] * 5 + [
        pl.BlockSpec((tm, 2 * d), lambda i: (i, gate_blk)),
        pl.BlockSpec((tm, pw), lambda i: (i, pool_blk)),
        pl.BlockSpec((16, pw), lambda i: (jnp.maximum(i * (tm // 16) - 1, 0), pool_blk)),
        tile(d), bvec, bvec, bvec] + [const(a) for a in consts]
    return pl.pallas_call(
        functools.partial(_post_kernel, tpb=tpb, alpha=alpha, d=d, tm=tm),
        grid=(m // tm,),
        in_specs=in_specs,
        out_specs=[tile(d), tile(d), tile(LANES)],
        out_shape=[jax.ShapeDtypeStruct((m, d), F32), jax.ShapeDtypeStruct((m, d), F32),
                   jax.ShapeDtypeStruct((m, LANES), F32)],
        scratch_shapes=[pltpu.VMEM((tm + 16, pw), F32)],
        compiler_params=_cparams("arbitrary"),
        name="merge_ln1",
    )(y, r, k, v, g, p, p, p, x2, g1, sc2, sh2, *consts)


def _route_kernel(lg_ref, idx_out, prob_out, rank_out, cnt_out, carry, *, n_exp, tm):
    @pl.when(pl.program_id(0) == 0)
    def _():
        carry[...] = jnp.zeros_like(carry)

    lane = lax.broadcasted_iota(jnp.int32, (tm, LANES), 1)
    lane_f = lane.astype(F32)
    neg = jnp.float32(-jnp.inf)
    work = jnp.where(lane < n_exp, lg_ref[...], neg)
    vals, sels = [], []
    idx_acc = jnp.zeros((tm, LANES), jnp.int32)
    hot = jnp.zeros((tm, LANES), F32)
    for kk in range(TOP_K):
        mx = jnp.max(work, axis=-1, keepdims=True)
        idx = jnp.min(jnp.where(work == mx, lane_f, float(LANES)), axis=-1, keepdims=True).astype(jnp.int32)
        sel = lane == idx
        work = jnp.where(sel, neg, work)
        vals.append(mx)
        sels.append(sel)
        idx_acc = jnp.where(lane == kk, idx, idx_acc)
        hot = hot + sel.astype(F32)
    es = [jnp.exp(vv - vals[0]) for vv in vals]
    den = es[0] + es[1] + es[2] + es[3]
    prob = jnp.zeros((tm, LANES), F32)
    for kk in range(TOP_K):
        prob = jnp.where(lane == kk, es[kk] / den, prob)
    ri = lax.broadcasted_iota(jnp.int32, (tm, tm), 0)
    ci = lax.broadcasted_iota(jnp.int32, (tm, tm), 1)
    before = _dot((ri > ci).astype(BF16), hot.astype(BF16)) + carry[0:1, :]
    rank = jnp.zeros((tm, LANES), jnp.int32)
    for kk in range(TOP_K):
        rk = jnp.sum(jnp.where(sels[kk], before, 0.0), axis=-1, keepdims=True).astype(jnp.int32)
        rank = jnp.where(lane == kk, rk, rank)
    carry[...] = carry[...] + jnp.sum(hot, axis=0, keepdims=True)
    idx_out[...] = idx_acc
    prob_out[...] = prob
    rank_out[...] = rank
    cnt_out[...] = carry[...]


def _route(logits, n_exp):
    m = logits.shape[0]
    tm = 512
    tile = pl.BlockSpec((tm, LANES), lambda i: (i, 0))
    return pl.pallas_call(
        functools.partial(_route_kernel, n_exp=n_exp, tm=tm),
        grid=(m // tm,),
        in_specs=[tile],
        out_specs=[tile, tile, tile, pl.BlockSpec((8, LANES), lambda i: (0, 0))],
        out_shape=[jax.ShapeDtypeStruct((m, LANES), jnp.int32), jax.ShapeDtypeStruct((m, LANES), F32),
                   jax.ShapeDtypeStruct((m, LANES), jnp.int32), jax.ShapeDtypeStruct((8, LANES), F32)],
        scratch_shapes=[pltpu.VMEM((8, LANES), F32)],
        compiler_params=_cparams("arbitrary"),
        name="route",
    )(logits)


def _row_copy(src_hbm, dst, src_row, dst_row, sem):
    return pltpu.make_async_copy(src_hbm.at[pl.ds(src_row, 1), :], dst.at[pl.ds(dst_row, 1), :], sem)


def _dispatch_kernel(src_ref, h_hbm, o_ref, buf, sem, *, rows):
    base = pl.program_id(0) * rows

    def issue(i, carry):
        _row_copy(h_hbm, buf, src_ref[base + i], i, sem).start()
        return carry

    lax.fori_loop(0, rows, issue, 0)

    def drain(i, carry):
        _row_copy(h_hbm, buf, 0, i, sem).wait()
        return carry

    lax.fori_loop(0, rows, drain, 0)
    o_ref[...] = buf[...].astype(BF16)


def _dispatch(src_tok, h2, n_rows):
    d = h2.shape[1]
    rows = MOE_SUB
    return pl.pallas_call(
        functools.partial(_dispatch_kernel, rows=rows),
        grid_spec=pltpu.PrefetchScalarGridSpec(
            num_scalar_prefetch=1,
            grid=(n_rows // rows,),
            in_specs=[pl.BlockSpec(memory_space=pl.ANY)],
            out_specs=pl.BlockSpec((rows, d), lambda i, src: (i, 0)),
            scratch_shapes=[pltpu.VMEM((rows, d), F32), pltpu.SemaphoreType.DMA(())]),
        out_shape=jax.ShapeDtypeStruct((n_rows, d), BF16),
        compiler_params=_cparams("arbitrary"),
        name="dispatch",
    )(src_tok, h2)


def _expert_kernel(e_ref, st_ref, ns_ref, tot_ref, x_ref, wg_ref, wu_ref, bg_ref, bu_ref, wd_ref, bd_ref,
                   y_hbm, acc, sem, *, n_ff, rmax, sub, n_rows):
    i = pl.program_id(0)
    j = pl.program_id(1)
    nsub = ns_ref[i]
    n_sub_max = rmax // sub

    def out_copy(row0, s):
        return pltpu.make_async_copy(acc.at[pl.ds(s * sub, sub), :],
                                     y_hbm.at[pl.ds(pl.multiple_of(row0, sub), sub), :], sem)

    def for_valid_subs(item, fn):
        for s in range(n_sub_max):
            @pl.when(s < ns_ref[item])
            def _():
                fn(out_copy(st_ref[item] + s * sub, s))

    @pl.when(j == 0)
    def _():
        @pl.when(i > 0)
        def _():
            for_valid_subs(i - 1, lambda cp: cp.wait())

        acc[...] = jnp.broadcast_to(bd_ref[0], acc.shape)

    for s in range(n_sub_max):
        @pl.when(s < nsub)
        def _():
            rows = slice(s * sub, (s + 1) * sub)
            xs = x_ref[rows, :]
            gate = jnp.minimum(_dot(xs, wg_ref[0]) + bg_ref[0], SWIGLU_LIMIT)
            up = jnp.clip(_dot(xs, wu_ref[0]) + bu_ref[0], -SWIGLU_LIMIT, SWIGLU_LIMIT)
            glu = gate * _sigmoid(gate * SWIGLU_ALPHA)
            act = ((up + 1.0) * glu).astype(BF16)
            acc[rows, :] += _dot(act, wd_ref[0].astype(BF16))

    @pl.when(j == n_ff - 1)
    def _():
        for_valid_subs(i, lambda cp: cp.start())

        @pl.when(i == pl.num_programs(0) - 1)
        def _():
            for_valid_subs(i, lambda cp: cp.wait())
            acc[0:sub, :] = jnp.zeros((sub, acc.shape[1]), F32)
            first = tot_ref[0] // sub

            def fill_start(blk, carry):
                out_copy(blk * sub, 0).start()
                return carry

            def fill_wait(blk, carry):
                out_copy(blk * sub, 0).wait()
                return carry

            lax.fori_loop(first, n_rows // sub, fill_start, 0)
            lax.fori_loop(first, n_rows // sub, fill_wait, 0)


def _experts(item_e, item_start, item_nsub, total_rows, x_buf, w_gate, w_up, b_gate, b_up, w_down, b_down,
             n_rows):
    n_items = item_e.shape[0]
    n_exp, d, dff = w_gate.shape
    tf = MOE_TF
    n_ff = dff // tf
    rmax = MOE_RMAX

    def jj(i, j, ns):
        return jnp.where(ns[i] > 0, j, n_ff - 1)

    in_specs = [
        pl.BlockSpec((pl.Element(rmax), pl.Element(d)),
                     lambda i, j, e, st, ns, tot: (pl.multiple_of(st[i], MOE_SUB), 0)),
        pl.BlockSpec((1, d, tf), lambda i, j, e, st, ns, tot: (e[i], 0, jj(i, j, ns))),
        pl.BlockSpec((1, d, tf), lambda i, j, e, st, ns, tot: (e[i], 0, jj(i, j, ns))),
        pl.BlockSpec((1, 1, tf), lambda i, j, e, st, ns, tot: (e[i], 0, jj(i, j, ns))),
        pl.BlockSpec((1, 1, tf), lambda i, j, e, st, ns, tot: (e[i], 0, jj(i, j, ns))),
        pl.BlockSpec((1, tf, d), lambda i, j, e, st, ns, tot: (e[i], jj(i, j, ns), 0)),
        pl.BlockSpec((1, 1, d), lambda i, j, e, st, ns, tot: (e[i], 0, 0)),
    ]
    return pl.pallas_call(
        functools.partial(_expert_kernel, n_ff=n_ff, rmax=rmax, sub=MOE_SUB, n_rows=n_rows),
        grid_spec=pltpu.PrefetchScalarGridSpec(
            num_scalar_prefetch=4,
            grid=(n_items, n_ff),
            in_specs=in_specs,
            out_specs=pl.BlockSpec(memory_space=pl.ANY),
            scratch_shapes=[pltpu.VMEM((rmax, d), F32), pltpu.SemaphoreType.DMA(())]),
        out_shape=jax.ShapeDtypeStruct((n_rows, d), F32),
        compiler_params=_cparams("arbitrary", "arbitrary"),
        name="experts",
    )(item_e, item_start, item_nsub, total_rows, x_buf, w_gate, w_up, b_gate.reshape(n_exp, 1, dff),
      b_up.reshape(n_exp, 1, dff), w_down, b_down.reshape(n_exp, 1, d))


def _combine_kernel(dest_ref, y_hbm, prob_ref, x1_ref, g2_ref, lg_ref, lb_ref, o_ref, buf, sem,
                    *, tm, alpha):
    base = pl.program_id(0) * tm * TOP_K

    def issue(i, carry):
        for kk in range(TOP_K):
            _row_copy(y_hbm, buf.at[kk], dest_ref[base + i * TOP_K + kk], i, sem).start()
        return carry

    lax.fori_loop(0, tm, issue, 0)

    def drain(i, carry):
        for kk in range(TOP_K):
            _row_copy(y_hbm, buf.at[kk], 0, i, sem).wait()
        return carry

    lax.fori_loop(0, tm, drain, 0)
    prob = prob_ref[...]
    f = prob[:, 0:1] * buf[0]
    for kk in range(1, TOP_K):
        f = f + prob[:, kk:kk + 1] * buf[kk]
    xr = alpha * x1_ref[...] + g2_ref[0] * f
    m2 = jnp.mean(xr, axis=-1, keepdims=True)
    xc = xr - m2
    v2 = jnp.mean(xc * xc, axis=-1, keepdims=True)
    o_ref[...] = xc * lax.rsqrt(v2 + LN_EPS) * lg_ref[...] + lb_ref[...]


def _combine(dest, y_buf, prob, x1, g2, ln_g, ln_b, seq, alpha):
    m, d = x1.shape
    tm = 128
    tpb = seq // tm
    return pl.pallas_call(
        functools.partial(_combine_kernel, tm=tm, alpha=alpha),
        grid_spec=pltpu.PrefetchScalarGridSpec(
            num_scalar_prefetch=1,
            grid=(m // tm,),
            in_specs=[pl.BlockSpec(memory_space=pl.ANY),
                      pl.BlockSpec((tm, LANES), lambda i, dst: (i, 0)),
                      pl.BlockSpec((tm, d), lambda i, dst: (i, 0)),
                      pl.BlockSpec((1, 1, d), lambda i, dst: (i // tpb, 0, 0)),
                      pl.BlockSpec((1, d), lambda i, dst: (0, 0)),
                      pl.BlockSpec((1, d), lambda i, dst: (0, 0))],
            out_specs=pl.BlockSpec((tm, d), lambda i, dst: (i, 0)),
            scratch_shapes=[pltpu.VMEM((TOP_K, tm, d), F32), pltpu.SemaphoreType.DMA(())]),
        out_shape=jax.ShapeDtypeStruct((m, d), F32),
        compiler_params=_cparams("arbitrary"),
        name="combine_ln2",
    )(dest, y_buf, prob, x1, g2, ln_g.reshape(1, d), ln_b.reshape(1, d))


def _pad_rows(w, rows):
    return jnp.pad(w, ((0, rows - w.shape[0]), (0, 0)))


def _pad_cols(w, cols):
    return jnp.pad(w, ((0, 0), (0, cols - w.shape[1])))


def _layer(x2, c8, bsz, seq, alpha, w_ada, b_ada, w_in, mu_shift, w0, w_decay_up, a0, w_iclr_up,
           w_glora_up, k_k, k_a, r_k, lnx_g, lnx_b, w_rwkv_out, w_pool, pool_scale, w_out, ln1_g,
           ln1_b, w_router, b_router, w_gu, b_gu, w_down, b_down, ln2_g, ln2_b):
    m, d = x2.shape
    c = w0.shape[0]
    n_dec, n_icl, n_gate = w_decay_up.shape[0], w_iclr_up.shape[0], w_glora_up.shape[0]
    pw = w_pool.shape[0] * w_pool.shape[1]
    n_exp = w_router.shape[1]
    assert n_dec <= LANES and n_icl <= LANES and n_gate <= 2 * LANES and d == 2 * c and pw == c

    mod = _adaln(c8, w_ada, b_ada)[:bsz]
    sh1, sc1, g1, sh2, sc2, g2 = [t.reshape(bsz, 1, d) for t in jnp.split(mod, 6, axis=-1)]

    o = 3 * c
    segs = [(o, n_dec, LANES), (o + n_dec, n_icl, LANES), (o + n_dec + n_icl, n_gate, 2 * LANES)]
    n_shift = o + n_dec + n_icl + n_gate
    zpad = 2 * d - 3 * c - 4 * LANES
    w_shift = jnp.concatenate([w_in[:, :o]] + [_pad_cols(w_in[:, s:s + n], wd) for s, n, wd in segs]
                              + [jnp.zeros((d, zpad), F32)], axis=1)
    mu = jnp.concatenate([mu_shift[:o]] + [jnp.pad(mu_shift[s:s + n], (0, wd - n)) for s, n, wd in segs]
                         + [jnp.zeros((zpad,), F32)]).reshape(1, -1)
    w_cat = jnp.concatenate([w_in[:, n_shift + pw:], w_shift, w_in[:, n_shift:n_shift + pw]],
                            axis=1).astype(BF16)
    p = _inproj(x2, sc1, sh1, w_cat, seq)

    ones_bd = jnp.kron(jnp.eye(LANES // HEAD_SIZE, dtype=F32), jnp.ones((HEAD_SIZE, HEAD_SIZE), F32)).astype(BF16)
    r, lw, kmod, v, kkn, bvec, g = _prep(
        p, mu, w0, a0, k_k, k_a, _pad_rows(w_decay_up, LANES).astype(BF16),
        _pad_rows(w_iclr_up, LANES).astype(BF16), _pad_rows(w_glora_up, 2 * LANES).astype(BF16),
        ones_bd, seq, c, shift_blk=1)
    y = _wkv(r, lw, kmod, v, kkn, bvec, bsz, seq, c)

    w_r = _pad_cols(w_router, LANES).astype(BF16)
    b_r = jnp.pad(b_router, (0, LANES - n_exp))
    x1, h2, logits = _post(y, r, kmod, v, g, p, x2, g1, sc2, sh2, r_k.reshape(-1), lnx_g, lnx_b, ones_bd,
                           w_rwkv_out.astype(BF16), w_pool.astype(BF16), pool_scale, w_out.astype(BF16),
                           ln1_g, ln1_b, w_r, b_r, seq, alpha, gate_blk=0, pool_blk=(4 * d) // pw)

    idx, prob, rank, cnt = _route(logits, n_exp)
    counts = cnt[0, :n_exp].astype(jnp.int32)
    padded = (counts + MOE_SUB - 1) // MOE_SUB * MOE_SUB
    pad_end = jnp.cumsum(padded)
    pad_start = pad_end - padded
    mk = m * TOP_K
    n_rows = (mk // MOE_SUB + n_exp) * MOE_SUB
    dest = (pad_start[idx[:, :TOP_K]] + rank[:, :TOP_K]).reshape(mk).astype(jnp.int32)
    src_tok = jnp.zeros((n_rows + MOE_RMAX,), jnp.int32).at[dest].set(jnp.arange(mk, dtype=jnp.int32) // TOP_K)
    n_items = mk // MOE_RMAX + n_exp
    per_e = (padded + MOE_RMAX - 1) // MOE_RMAX
    it_end = jnp.cumsum(per_e)
    it_start = it_end - per_e
    ids = jnp.arange(n_items, dtype=jnp.int32)
    item_e = jnp.minimum(jnp.searchsorted(it_end, ids, side='right'), n_exp - 1).astype(jnp.int32)
    local = ids - it_start[item_e]
    used = ids < it_end[-1]
    item_start = jnp.where(used, pad_start[item_e] + local * MOE_RMAX, n_rows).astype(jnp.int32)
    item_nsub = jnp.where(used, jnp.clip((padded[item_e] - local * MOE_RMAX) // MOE_SUB, 0,
                                         MOE_RMAX // MOE_SUB), 0).astype(jnp.int32)

    x_buf = _dispatch(src_tok, h2, n_rows + MOE_RMAX)
    y_buf = _experts(item_e, item_start, item_nsub, pad_end[-1:].astype(jnp.int32), x_buf,
                     w_gu[:, :, 0::2].astype(BF16), w_gu[:, :, 1::2].astype(BF16), b_gu[:, 0::2],
                     b_gu[:, 1::2], w_down, b_down, n_rows)
    return _combine(dest, y_buf, prob, x1, g2, ln2_g, ln2_b, seq, alpha)


def kernel(x, c, w_ada, b_ada, w_in, mu_shift, w0, w_decay_up, a0, w_iclr_up, w_glora_up, k_k, k_a, r_k, lnx_g, lnx_b, w_rwkv_out, w_pool, pool_scale, w_out, ln1_g, ln1_b, w_router, b_router, w_gu, b_gu, w_down, b_down, ln2_g, ln2_b):
    bsz, seq, d = x.shape
    depth = w_ada.shape[0]
    alpha = float((2 * depth) ** 0.25)
    c8 = jnp.pad(c, ((0, 8 - bsz), (0, 0)))
    x2 = x.reshape(bsz * seq, d)
    weights = (w_ada, b_ada, w_in, mu_shift, w0, w_decay_up, a0, w_iclr_up, w_glora_up, k_k, k_a, r_k,
               lnx_g, lnx_b, w_rwkv_out, w_pool, pool_scale, w_out, ln1_g, ln1_b, w_router, b_router,
               w_gu, b_gu, w_down, b_down, ln2_g, ln2_b)
    for l in range(depth):
        x2 = _layer(x2, c8, bsz, seq, alpha, *[w[l] for w in weights])
    return x2.reshape(bsz, seq, d)
```

```python
import functools

import jax
import jax.numpy as jnp
from jax import lax
from jax.experimental import pallas as pl
from jax.experimental.pallas import tpu as pltpu

F32 = jnp.float32
BF16 = jnp.bfloat16

HEAD_SIZE = 64
LANES = 128
TOP_K = 4
LN_EPS = 1e-5
GN_EPS = 64e-5
POOL_WINDOWS = (2, 4, 8, 16)
SWIGLU_LIMIT = 7.0
SWIGLU_ALPHA = 1.702
WKV_CHUNK = 64
MOE_SUB = 256
MOE_RMAX = 1024
MOE_TF = 512
VMEM_LIMIT = 56 * 1024 * 1024

_NT = (((1,), (1,)), ((), ()))
_TN = (((0,), (0,)), ((), ()))


def _cparams(*sem):
    return pltpu.CompilerParams(dimension_semantics=sem, vmem_limit_bytes=VMEM_LIMIT)


def _sigmoid(z):
    return 1.0 / (1.0 + jnp.exp(-z))


def _dot(a, b):
    return jnp.dot(a, b, preferred_element_type=F32)


def _split_dot(xv, w):
    hi = xv.astype(BF16)
    lo = (xv - hi.astype(F32)).astype(BF16)
    return _dot(hi, w) + _dot(lo, w)


def _head_sum(xv, ones_bd):
    outs = []
    for p in range(xv.shape[1] // LANES):
        outs.append(_split_dot(xv[:, p * LANES:(p + 1) * LANES], ones_bd))
    return jnp.concatenate(outs, axis=1)


def _adaln_kernel(c_ref, w_ref, b_ref, o_ref):
    c = c_ref[...]
    cond = c * _sigmoid(c)
    o_ref[...] = _dot(cond.astype(BF16), w_ref[...].astype(BF16)) + b_ref[...]


def _adaln(c8, w, b):
    d, n = w.shape
    tn = 1024
    return pl.pallas_call(
        _adaln_kernel,
        grid=(n // tn,),
        in_specs=[pl.BlockSpec((8, d), lambda j: (0, 0)),
                  pl.BlockSpec((d, tn), lambda j: (0, j)),
                  pl.BlockSpec((1, tn), lambda j: (0, j))],
        out_specs=pl.BlockSpec((8, tn), lambda j: (0, j)),
        out_shape=jax.ShapeDtypeStruct((8, n), F32),
        compiler_params=_cparams("arbitrary"),
        name="adaln",
    )(c8, w, b.reshape(1, n))


def _inproj_kernel(x_ref, sc_ref, sh_ref, w_ref, o_ref, h_scr):
    @pl.when(pl.program_id(1) == 0)
    def _():
        h_scr[...] = (x_ref[...] * (1.0 + sc_ref[0]) + sh_ref[0]).astype(BF16)

    o_ref[...] = _dot(h_scr[...], w_ref[...])


def _inproj(x2, sc, sh, w, seq):
    m, d = x2.shape
    n = w.shape[1]
    tm, tn = 1024, 1024
    tpb = seq // tm
    return pl.pallas_call(
        _inproj_kernel,
        grid=(m // tm, n // tn),
        in_specs=[pl.BlockSpec((tm, d), lambda i, j: (i, 0)),
                  pl.BlockSpec((1, 1, d), lambda i, j: (i // tpb, 0, 0)),
                  pl.BlockSpec((1, 1, d), lambda i, j: (i // tpb, 0, 0)),
                  pl.BlockSpec((d, tn), lambda i, j: (0, j))],
        out_specs=pl.BlockSpec((tm, tn), lambda i, j: (i, j)),
        out_shape=jax.ShapeDtypeStruct((m, n), F32),
        scratch_shapes=[pltpu.VMEM((tm, d), BF16)],
        compiler_params=_cparams("arbitrary", "arbitrary"),
        name="inproj",
    )(x2, sc, sh, w)


def _prep_kernel(p_ref, prev_ref, mu_ref, w0_ref, a0_ref, kk_ref, ka_ref, wd_ref, wa_ref, wg_ref,
                 ones_ref, r_out, lw_out, k_out, v_out, kkn_out, b_out, g_out, *, tpb, c):
    cur = p_ref[...]
    first = (pl.program_id(0) % tpb) == 0
    prev_row = jnp.where(first, 0.0, prev_ref[7:8, :])
    shifted = pltpu.roll(cur, 1, axis=0)
    row0 = lax.broadcasted_iota(jnp.int32, cur.shape, 0) == 0
    shifted = jnp.where(row0, prev_row, shifted)
    ps = cur + (shifted - cur) * mu_ref[...]
    r = ps[:, 0:c]
    k = ps[:, c:2 * c]
    v = ps[:, 2 * c:3 * c]
    o = 3 * c
    wdp = ps[:, o:o + LANES]
    adp = ps[:, o + LANES:o + 2 * LANES]
    gdp = ps[:, o + 2 * LANES:o + 4 * LANES]
    z = w0_ref[...] + _dot(jnp.tanh(wdp).astype(BF16), wd_ref[...])
    lw = -jnp.exp(-0.5) * _sigmoid(z)
    iclr = _sigmoid(a0_ref[...] + _dot(adp.astype(BF16), wa_ref[...]))
    g = _dot(_sigmoid(gdp).astype(BF16), wg_ref[...])
    kk0 = k * kk_ref[...]
    ss = _head_sum(kk0 * kk0, ones_ref[...])
    kkn = kk0 / jnp.maximum(jnp.sqrt(ss), 1e-12)
    kmod = k * (1.0 + (iclr - 1.0) * ka_ref[...])
    r_out[...] = r
    lw_out[...] = lw
    k_out[...] = kmod
    v_out[...] = v
    kkn_out[...] = kkn
    b_out[...] = kkn * iclr
    g_out[...] = g


def _prep(p, mu, w0, a0, k_k, k_a, wd, wa, wg, ones_bd, seq, c, shift_blk):
    m = p.shape[0]
    ns = mu.shape[1]
    tm = 256
    tpb = seq // tm
    row = lambda a: a.reshape(1, -1)
    full = lambda a: pl.BlockSpec(a.shape, lambda i: (0,) * a.ndim)
    out = jax.ShapeDtypeStruct((m, c), F32)
    ospec = pl.BlockSpec((tm, c), lambda i: (i, 0))
    args = (p, p, mu, row(w0), row(a0), row(k_k), row(k_a), wd, wa, wg, ones_bd)
    in_specs = [pl.BlockSpec((tm, ns), lambda i: (i, shift_blk)),
                pl.BlockSpec((8, ns), lambda i: (jnp.maximum(i * (tm // 8) - 1, 0), shift_blk))]
    in_specs += [full(a) for a in args[2:]]
    return pl.pallas_call(
        functools.partial(_prep_kernel, tpb=tpb, c=c),
        grid=(m // tm,),
        in_specs=in_specs,
        out_specs=[ospec] * 7,
        out_shape=[out] * 7,
        compiler_params=_cparams("arbitrary"),
        name="rwkv_prep",
    )(*args)


def _wkv_kernel(r_ref, lw_ref, k_ref, v_ref, kk_ref, b_ref, y_ref, s_scr, *, chunk, c):
    L = chunk

    @pl.when(pl.program_id(1) == 0)
    def _():
        s_scr[...] = jnp.zeros_like(s_scr)

    ri = lax.broadcasted_iota(jnp.int32, (L, L), 0)
    ci = lax.broadcasted_iota(jnp.int32, (L, L), 1)
    tri = (ri >= ci).astype(BF16)
    lw = lw_ref[...]
    cs = _split_dot_lhs(tri, lw)
    cs_last = cs[L - 1:L, :]
    g_in = jnp.exp(cs)
    g_ex = jnp.exp(cs - lw)
    g_inv = jnp.exp(-cs)
    g_tail = jnp.exp(cs_last - cs)
    g_last = jnp.exp(cs_last)
    kk = kk_ref[...]
    bb = b_ref[...]
    kx = k_ref[...]
    vv = v_ref[...]
    a_t = -kk * g_ex
    r_t = r_ref[...] * g_in
    b_t = bb * g_inv
    k_t = kx * g_inv
    b_p = bb * g_tail
    k_p = kx * g_tail

    r2 = lax.broadcasted_iota(jnp.int32, (2 * L, 2 * L), 0)
    c2 = lax.broadcasted_iota(jnp.int32, (2 * L, 2 * L), 1)
    same = (r2 >= L) == (c2 >= L)
    m_strict = same & (r2 > c2)
    m_incl = same & (r2 >= c2)
    lane = lax.broadcasted_iota(jnp.int32, (L, LANES), 1)
    low = lane < HEAD_SIZE

    def stack(zv):
        return jnp.concatenate([jnp.where(low, zv, 0.0), jnp.where(low, 0.0, zv)], axis=0)

    for p in range(c // LANES):
        sl = slice(p * LANES, (p + 1) * LANES)
        a_s = stack(a_t[:, sl])
        r_s = stack(r_t[:, sl])
        b_s = stack(b_t[:, sl])
        k_s = stack(k_t[:, sl])
        v_s = stack(vv[:, sl])
        bp_s = stack(b_p[:, sl])
        kp_s = stack(k_p[:, sl])
        ar = jnp.concatenate([a_s, r_s], axis=0).astype(BF16)
        qb = lax.dot_general(ar, b_s.astype(BF16), _NT, preferred_element_type=F32)
        qk = lax.dot_general(ar, k_s.astype(BF16), _NT, preferred_element_type=F32)
        a_ab = jnp.where(m_strict, qb[:2 * L], 0.0)
        a_rb = jnp.where(m_incl, qb[2 * L:], 0.0)
        a_ak = jnp.where(m_strict, qk[:2 * L], 0.0)
        a_rk = jnp.where(m_incl, qk[2 * L:], 0.0)
        v_b = v_s.astype(BF16)
        z = jnp.concatenate([_dot(a_ak.astype(BF16), v_b), a_s], axis=1)
        amat = a_ab
        n_sq = max(L.bit_length() - 1, 1)
        for it in range(n_sq):
            ab = amat.astype(BF16)
            z = z + _dot(ab, z.astype(BF16))
            if it + 1 < n_sq:
                amat = _dot(ab, ab)
        u_loc = z[:, :LANES]
        a_pr = z[:, LANES:]
        s_old = s_scr[p]
        s_b = s_old.astype(BF16)
        u_s = _dot(a_pr.astype(BF16), s_b) + u_loc
        uv = jnp.concatenate([u_s, v_s], axis=0).astype(BF16)
        y_s = _dot(r_s.astype(BF16), s_b) + _dot(jnp.concatenate([a_rb, a_rk], axis=1).astype(BF16), uv)
        y_ref[:, sl] = y_s[:L] + y_s[L:]
        bk = jnp.concatenate([bp_s, kp_s], axis=0).astype(BF16)
        upd = lax.dot_general(bk, uv, _TN, preferred_element_type=F32)
        g_col = jnp.transpose(jnp.broadcast_to(g_last[:, sl], (LANES, LANES)))
        s_scr[p] = s_old * g_col + upd


def _split_dot_lhs(w, xv):
    hi = xv.astype(BF16)
    lo = (xv - hi.astype(F32)).astype(BF16)
    return _dot(w, hi) + _dot(w, lo)


def _wkv(r, lw, k, v, kk, b, bsz, seq, c):
    L = WKV_CHUNK
    nc = seq // L
    spec = pl.BlockSpec((L, c), lambda bi, ci: (bi * nc + ci, 0))
    return pl.pallas_call(
        functools.partial(_wkv_kernel, chunk=L, c=c),
        grid=(bsz, nc),
        in_specs=[spec] * 6,
        out_specs=spec,
        out_shape=jax.ShapeDtypeStruct((bsz * seq, c), F32),
        scratch_shapes=[pltpu.VMEM((c // LANES, LANES, LANES), F32)],
        compiler_params=_cparams("arbitrary", "arbitrary"),
        name="wkv7",
    )(r, lw, k, v, kk, b)


def _post_kernel(y_ref, r_ref, k_ref, v_ref, g_ref, gate_ref, pool_ref, halo_ref, x_ref,
                 g1_ref, sc2_ref, sh2_ref, rk_ref, lng_ref, lnb_ref, ones_ref, wro_ref, wp_ref,
                 psc_ref, wo_ref, l1g_ref, l1b_ref, wr_ref, br_ref,
                 x1_out, h2_out, lg_out, pool_scr, *, tpb, alpha, d, tm):
    ones_bd = ones_ref[...]
    inv_n = 1.0 / HEAD_SIZE
    y = y_ref[...]
    mu = _head_sum(y, ones_bd) * inv_n
    dy = y - mu
    var = _head_sum(dy * dy, ones_bd) * inv_n
    yn = dy * lax.rsqrt(var + GN_EPS) * lng_ref[...] + lnb_ref[...]
    vv = v_ref[...]
    bonus = _head_sum(r_ref[...] * k_ref[...] * rk_ref[...], ones_bd) * vv
    y_rwkv = ((yn + bonus) * g_ref[...]).astype(BF16)
    branch_a = _dot(y_rwkv, wro_ref[...])

    first = (pl.program_id(0) % tpb) == 0
    halo = 16
    pool_scr[0:halo, :] = jnp.where(first, 0.0, halo_ref[...])
    pool_scr[halo:, :] = pool_ref[...]
    pos = (pl.program_id(0) % tpb) * tm + lax.broadcasted_iota(jnp.int32, (tm, 1), 0)
    gw = pool_ref.shape[1] // len(POOL_WINDOWS)
    dg = d // len(POOL_WINDOWS)
    outs = []
    for gi, win in enumerate(POOL_WINDOWS):
        cols = slice(gi * gw, (gi + 1) * gw)
        u = pool_scr[halo:, cols]
        wsum = u
        for s in range(1, win):
            wsum = wsum + pool_scr[halo - s:halo - s + tm, cols]
        cnt = jnp.minimum(pos + 1, win).astype(F32)
        dd = (wsum / cnt - u).astype(BF16)
        outs.append(_dot(dd, wp_ref[gi]))
    branch_b = jnp.concatenate(outs, axis=1) * psc_ref[...]

    gates = gate_ref[...]
    mixin = _sigmoid(gates[:, :d]) * branch_a + _sigmoid(gates[:, d:]) * branch_b
    mix = _dot(mixin.astype(BF16), wo_ref[...])
    xr = alpha * x_ref[...] + g1_ref[0] * mix
    m1 = jnp.mean(xr, axis=-1, keepdims=True)
    xc = xr - m1
    v1 = jnp.mean(xc * xc, axis=-1, keepdims=True)
    x1 = xc * lax.rsqrt(v1 + LN_EPS) * l1g_ref[...] + l1b_ref[...]
    x1_out[...] = x1
    h2 = x1 * (1.0 + sc2_ref[0]) + sh2_ref[0]
    for g in range(d // LANES):
        h2_out[:, g, :] = h2[:, g * LANES:(g + 1) * LANES]
    lg_out[...] = _dot(h2.astype(BF16), wr_ref[...]) + br_ref[...]


def _post(y, r, k, v, g, p, x2, g1, sc2, sh2, r_k, lnx_g, lnx_b, ones_bd, w_ro, w_pool, pool_scale,
          w_out, ln1_g, ln1_b, w_r, b_r, seq, alpha, gate_blk, pool_blk):
    m, d = x2.shape
    c = y.shape[1]
    pw = w_pool.shape[0] * w_pool.shape[1]
    tm = 256
    tpb = seq // tm
    row = lambda a: a.reshape(1, -1)
    const = lambda a: pl.BlockSpec(a.shape, lambda i: (0,) * a.ndim, pipeline_mode=pl.Buffered(1))
    tile = lambda w: pl.BlockSpec((tm, w), lambda i: (i, 0))
    bvec = pl.BlockSpec((1, 1, d), lambda i: (i // tpb, 0, 0))
    consts = (row(r_k), row(lnx_g), row(lnx_b), ones_bd, w_ro, w_pool, row(pool_scale), w_out,
              row(ln1_g), row(ln1_b), w_r, row(b_r))
    in_specs = [tile(c)] * 5 + [
        pl.BlockSpec((tm, 2 * d), lambda i: (i, gate_blk)),
        pl.BlockSpec((tm, pw), lambda i: (i, pool_blk)),
        pl.BlockSpec((16, pw), lambda i: (jnp.maximum(i * (tm // 16) - 1, 0), pool_blk)),
        tile(d), bvec, bvec, bvec] + [const(a) for a in consts]
    return pl.pallas_call(
        functools.partial(_post_kernel, tpb=tpb, alpha=alpha, d=d, tm=tm),
        grid=(m // tm,),
        in_specs=in_specs,
        out_specs=[tile(d), pl.BlockSpec((tm, d // LANES, LANES), lambda i: (i, 0, 0)), tile(LANES)],
        out_shape=[jax.ShapeDtypeStruct((m, d), F32), jax.ShapeDtypeStruct((m, d // LANES, LANES), F32),
                   jax.ShapeDtypeStruct((m, LANES), F32)],
        scratch_shapes=[pltpu.VMEM((tm + 16, pw), F32)],
        compiler_params=_cparams("arbitrary"),
        name="merge_ln1",
    )(y, r, k, v, g, p, p, p, x2, g1, sc2, sh2, *consts)


def _route_kernel(lg_ref, idx_out, prob_out, rank_out, cnt_out, carry, *, n_exp, tm):
    @pl.when(pl.program_id(0) == 0)
    def _():
        carry[...] = jnp.zeros_like(carry)

    lane = lax.broadcasted_iota(jnp.int32, (tm, LANES), 1)
    lane_f = lane.astype(F32)
    neg = jnp.float32(-jnp.inf)
    work = jnp.where(lane < n_exp, lg_ref[...], neg)
    vals, sels = [], []
    idx_acc = jnp.zeros((tm, LANES), jnp.int32)
    hot = jnp.zeros((tm, LANES), F32)
    for kk in range(TOP_K):
        mx = jnp.max(work, axis=-1, keepdims=True)
        idx = jnp.min(jnp.where(work == mx, lane_f, float(LANES)), axis=-1, keepdims=True).astype(jnp.int32)
        sel = lane == idx
        work = jnp.where(sel, neg, work)
        vals.append(mx)
        sels.append(sel)
        idx_acc = jnp.where(lane == kk, idx, idx_acc)
        hot = hot + sel.astype(F32)
    es = [jnp.exp(vv - vals[0]) for vv in vals]
    den = es[0] + es[1] + es[2] + es[3]
    prob = jnp.zeros((tm, LANES), F32)
    for kk in range(TOP_K):
        prob = jnp.where(lane == kk, es[kk] / den, prob)
    ri = lax.broadcasted_iota(jnp.int32, (tm, tm), 0)
    ci = lax.broadcasted_iota(jnp.int32, (tm, tm), 1)
    before = _dot((ri > ci).astype(BF16), hot.astype(BF16)) + carry[0:1, :]
    rank = jnp.zeros((tm, LANES), jnp.int32)
    for kk in range(TOP_K):
        rk = jnp.sum(jnp.where(sels[kk], before, 0.0), axis=-1, keepdims=True).astype(jnp.int32)
        rank = jnp.where(lane == kk, rk, rank)
    carry[...] = carry[...] + jnp.sum(hot, axis=0, keepdims=True)
    idx_out[...] = idx_acc
    prob_out[...] = prob
    rank_out[...] = rank
    cnt_out[...] = carry[...]


def _route(logits, n_exp):
    m = logits.shape[0]
    tm = 512
    tile = pl.BlockSpec((tm, LANES), lambda i: (i, 0))
    return pl.pallas_call(
        functools.partial(_route_kernel, n_exp=n_exp, tm=tm),
        grid=(m // tm,),
        in_specs=[tile],
        out_specs=[tile, tile, tile, pl.BlockSpec((8, LANES), lambda i: (0, 0))],
        out_shape=[jax.ShapeDtypeStruct((m, LANES), jnp.int32), jax.ShapeDtypeStruct((m, LANES), F32),
                   jax.ShapeDtypeStruct((m, LANES), jnp.int32), jax.ShapeDtypeStruct((8, LANES), F32)],
        scratch_shapes=[pltpu.VMEM((8, LANES), F32)],
        compiler_params=_cparams("arbitrary"),
        name="route",
    )(logits)


def _row_copy(src_hbm, dst, src_row, dst_row, sem):
    return pltpu.make_async_copy(src_hbm.at[pl.ds(src_row, 1), :], dst.at[pl.ds(dst_row, 1), :], sem)


def _dispatch_kernel(src_ref, h_hbm, o_ref, buf, sem, *, rows):
    base = pl.program_id(0) * rows

    def slab_copy(tok, i):
        return pltpu.make_async_copy(h_hbm.at[tok], buf.at[i], sem)

    def issue(i, carry):
        slab_copy(src_ref[base + i], i).start()
        return carry

    lax.fori_loop(0, rows, issue, 0, unroll=8)

    def drain(i, carry):
        slab_copy(0, i).wait()
        return carry

    lax.fori_loop(0, rows, drain, 0, unroll=8)
    for g in range(buf.shape[1]):
        o_ref[:, g * LANES:(g + 1) * LANES] = buf[:, g, :].astype(BF16)


def _dispatch(src_tok, h2, n_rows):
    _, ng, _ = h2.shape
    rows = MOE_SUB
    return pl.pallas_call(
        functools.partial(_dispatch_kernel, rows=rows),
        grid_spec=pltpu.PrefetchScalarGridSpec(
            num_scalar_prefetch=1,
            grid=(n_rows // rows,),
            in_specs=[pl.BlockSpec(memory_space=pl.ANY)],
            out_specs=pl.BlockSpec((rows, ng * LANES), lambda i, src: (i, 0)),
            scratch_shapes=[pltpu.VMEM((rows, ng, LANES), F32), pltpu.SemaphoreType.DMA(())]),
        out_shape=jax.ShapeDtypeStruct((n_rows, ng * LANES), BF16),
        compiler_params=_cparams("arbitrary"),
        name="dispatch",
    )(src_tok, h2)


def _roll_lanes(v, shift):
    return jnp.concatenate([pltpu.roll(v[:, g * LANES:(g + 1) * LANES], shift, axis=1)
                            for g in range(v.shape[1] // LANES)], axis=1)


def _expert_kernel(e_ref, st_ref, ns_ref, tot_ref, x_ref, wgu_ref, bgu_ref, wd_ref, bd_ref,
                   y_hbm, acc, wgu_b, wd_i, wd_b, sem, *, n_ff, rmax, sub, n_rows):
    i = pl.program_id(0)
    j = pl.program_id(1)
    nsub = ns_ref[i]
    n_sub_max = rmax // sub
    tfc = wd_ref.shape[1]
    half = tfc // 2

    def out_copy(row0, s):
        return pltpu.make_async_copy(acc.at[pl.ds(s * sub, sub), :],
                                     y_hbm.at[pl.ds(pl.multiple_of(row0, sub), sub), :], sem)

    def for_valid_subs(item, fn):
        for s in range(n_sub_max):
            @pl.when(s < ns_ref[item])
            def _():
                fn(out_copy(st_ref[item] + s * sub, s))

    @pl.when(j == 0)
    def _():
        @pl.when(i > 0)
        def _():
            for_valid_subs(i - 1, lambda cp: cp.wait())

        acc[...] = jnp.broadcast_to(bd_ref[0], acc.shape)

    @pl.when(nsub > 0)
    def _():
        wgu_b[...] = wgu_ref[0].astype(BF16)
        for g in range(wd_i.shape[0]):
            cols = slice(g * LANES, (g + 1) * LANES)
            wd_i[g, pl.ds(0, half, stride=2), :] = wd_ref[0, 0:half, cols]
            wd_i[g, pl.ds(1, half, stride=2), :] = wd_ref[0, half:tfc, cols]
            wd_b[:, cols] = wd_i[g].astype(BF16)

    even = (lax.broadcasted_iota(jnp.int32, (sub, tfc), 1) % 2) == 0
    for s in range(n_sub_max):
        @pl.when(s < nsub)
        def _():
            rows = slice(s * sub, (s + 1) * sub)
            gu = _dot(x_ref[rows, :], wgu_b[...]) + bgu_ref[0]
            gate = jnp.minimum(gu, SWIGLU_LIMIT)
            glu = gate * _sigmoid(gate * SWIGLU_ALPHA)
            up1 = jnp.clip(gu, -SWIGLU_LIMIT, SWIGLU_LIMIT) + 1.0
            prod = glu * _roll_lanes(up1, LANES - 1)
            act = jnp.where(even, prod[:, :tfc], _roll_lanes(prod[:, tfc:], 1))
            acc[rows, :] += _dot(act.astype(BF16), wd_b[...])

    @pl.when(j == n_ff - 1)
    def _():
        for_valid_subs(i, lambda cp: cp.start())

        @pl.when(i == pl.num_programs(0) - 1)
        def _():
            for_valid_subs(i, lambda cp: cp.wait())
            acc[0:sub, :] = jnp.zeros((sub, acc.shape[1]), F32)
            first = tot_ref[0] // sub

            def fill_start(blk, carry):
                out_copy(blk * sub, 0).start()
                return carry

            def fill_wait(blk, carry):
                out_copy(blk * sub, 0).wait()
                return carry

            lax.fori_loop(first, n_rows // sub, fill_start, 0)
            lax.fori_loop(first, n_rows // sub, fill_wait, 0)


def _experts(item_e, item_start, item_nsub, total_rows, x_buf, w_gu, b_gu, w_down, b_down, n_rows):
    n_items = item_e.shape[0]
    n_exp, dff, d = w_down.shape
    tf = MOE_TF
    n_ff = dff // tf
    rmax = MOE_RMAX

    def jj(i, j, ns):
        return jnp.where(ns[i] > 0, j, n_ff - 1)

    in_specs = [
        pl.BlockSpec((pl.Element(rmax), pl.Element(d)),
                     lambda i, j, e, st, ns, tot: (pl.multiple_of(st[i], MOE_SUB), 0)),
        pl.BlockSpec((1, d, 2 * tf), lambda i, j, e, st, ns, tot: (e[i], 0, jj(i, j, ns))),
        pl.BlockSpec((1, 1, 2 * tf), lambda i, j, e, st, ns, tot: (e[i], 0, jj(i, j, ns))),
        pl.BlockSpec((1, tf, d), lambda i, j, e, st, ns, tot: (e[i], jj(i, j, ns), 0)),
        pl.BlockSpec((1, 1, d), lambda i, j, e, st, ns, tot: (e[i], 0, 0)),
    ]
    return pl.pallas_call(
        functools.partial(_expert_kernel, n_ff=n_ff, rmax=rmax, sub=MOE_SUB, n_rows=n_rows),
        grid_spec=pltpu.PrefetchScalarGridSpec(
            num_scalar_prefetch=4,
            grid=(n_items, n_ff),
            in_specs=in_specs,
            out_specs=pl.BlockSpec(memory_space=pl.ANY),
            scratch_shapes=[pltpu.VMEM((rmax, d), F32), pltpu.VMEM((d, 2 * tf), BF16),
                            pltpu.VMEM((d // LANES, tf, LANES), F32), pltpu.VMEM((tf, d), BF16),
                            pltpu.SemaphoreType.DMA(())]),
        out_shape=jax.ShapeDtypeStruct((n_rows, d), F32),
        compiler_params=_cparams("arbitrary", "arbitrary"),
        name="experts",
    )(item_e, item_start, item_nsub, total_rows, x_buf, w_gu, b_gu.reshape(n_exp, 1, 2 * dff),
      w_down, b_down.reshape(n_exp, 1, d))


def _combine_kernel(dest_ref, y_hbm, prob_ref, x1_ref, g2_ref, lg_ref, lb_ref, o_ref, buf, sem,
                    *, tm, alpha):
    base = pl.program_id(0) * tm * TOP_K

    def issue(i, carry):
        for kk in range(TOP_K):
            _row_copy(y_hbm, buf.at[kk], dest_ref[base + i * TOP_K + kk], i, sem).start()
        return carry

    lax.fori_loop(0, tm, issue, 0)

    def drain(i, carry):
        for kk in range(TOP_K):
            _row_copy(y_hbm, buf.at[kk], 0, i, sem).wait()
        return carry

    lax.fori_loop(0, tm, drain, 0)
    prob = prob_ref[...]
    f = prob[:, 0:1] * buf[0]
    for kk in range(1, TOP_K):
        f = f + prob[:, kk:kk + 1] * buf[kk]
    xr = alpha * x1_ref[...] + g2_ref[0] * f
    m2 = jnp.mean(xr, axis=-1, keepdims=True)
    xc = xr - m2
    v2 = jnp.mean(xc * xc, axis=-1, keepdims=True)
    o_ref[...] = xc * lax.rsqrt(v2 + LN_EPS) * lg_ref[...] + lb_ref[...]


def _combine(dest, y_buf, prob, x1, g2, ln_g, ln_b, seq, alpha):
    m, d = x1.shape
    tm = 128
    tpb = seq // tm
    return pl.pallas_call(
        functools.partial(_combine_kernel, tm=tm, alpha=alpha),
        grid_spec=pltpu.PrefetchScalarGridSpec(
            num_scalar_prefetch=1,
            grid=(m // tm,),
            in_specs=[pl.BlockSpec(memory_space=pl.ANY),
                      pl.BlockSpec((tm, LANES), lambda i, dst: (i, 0)),
                      pl.BlockSpec((tm, d), lambda i, dst: (i, 0)),
                      pl.BlockSpec((1, 1, d), lambda i, dst: (i // tpb, 0, 0)),
                      pl.BlockSpec((1, d), lambda i, dst: (0, 0)),
                      pl.BlockSpec((1, d), lambda i, dst: (0, 0))],
            out_specs=pl.BlockSpec((tm, d), lambda i, dst: (i, 0)),
            scratch_shapes=[pltpu.VMEM((TOP_K, tm, d), F32), pltpu.SemaphoreType.DMA(())]),
        out_shape=jax.ShapeDtypeStruct((m, d), F32),
        compiler_params=_cparams("arbitrary"),
        name="combine_ln2",
    )(dest, y_buf, prob, x1, g2, ln_g.reshape(1, d), ln_b.reshape(1, d))


def _pad_rows(w, rows):
    return jnp.pad(w, ((0, rows - w.shape[0]), (0, 0)))


def _pad_cols(w, cols):
    return jnp.pad(w, ((0, 0), (0, cols - w.shape[1])))


def _layer(x2, c8, bsz, seq, alpha, w_ada, b_ada, w_in, mu_shift, w0, w_decay_up, a0, w_iclr_up,
           w_glora_up, k_k, k_a, r_k, lnx_g, lnx_b, w_rwkv_out, w_pool, pool_scale, w_out, ln1_g,
           ln1_b, w_router, b_router, w_gu, b_gu, w_down, b_down, ln2_g, ln2_b):
    m, d = x2.shape
    c = w0.shape[0]
    n_dec, n_icl, n_gate = w_decay_up.shape[0], w_iclr_up.shape[0], w_glora_up.shape[0]
    pw = w_pool.shape[0] * w_pool.shape[1]
    n_exp = w_router.shape[1]
    assert n_dec <= LANES and n_icl <= LANES and n_gate <= 2 * LANES and d == 2 * c and pw == c

    mod = _adaln(c8, w_ada, b_ada)[:bsz]
    sh1, sc1, g1, sh2, sc2, g2 = [t.reshape(bsz, 1, d) for t in jnp.split(mod, 6, axis=-1)]

    o = 3 * c
    segs = [(o, n_dec, LANES), (o + n_dec, n_icl, LANES), (o + n_dec + n_icl, n_gate, 2 * LANES)]
    n_shift = o + n_dec + n_icl + n_gate
    zpad = 2 * d - 3 * c - 4 * LANES
    w_shift = jnp.concatenate([w_in[:, :o]] + [_pad_cols(w_in[:, s:s + n], wd) for s, n, wd in segs]
                              + [jnp.zeros((d, zpad), F32)], axis=1)
    mu = jnp.concatenate([mu_shift[:o]] + [jnp.pad(mu_shift[s:s + n], (0, wd - n)) for s, n, wd in segs]
                         + [jnp.zeros((zpad,), F32)]).reshape(1, -1)
    w_cat = jnp.concatenate([w_in[:, n_shift + pw:], w_shift, w_in[:, n_shift:n_shift + pw]],
                            axis=1).astype(BF16)
    p = _inproj(x2, sc1, sh1, w_cat, seq)

    ones_bd = jnp.kron(jnp.eye(LANES // HEAD_SIZE, dtype=F32), jnp.ones((HEAD_SIZE, HEAD_SIZE), F32)).astype(BF16)
    r, lw, kmod, v, kkn, bvec, g = _prep(
        p, mu, w0, a0, k_k, k_a, _pad_rows(w_decay_up, LANES).astype(BF16),
        _pad_rows(w_iclr_up, LANES).astype(BF16), _pad_rows(w_glora_up, 2 * LANES).astype(BF16),
        ones_bd, seq, c, shift_blk=1)
    y = _wkv(r, lw, kmod, v, kkn, bvec, bsz, seq, c)

    w_r = _pad_cols(w_router, LANES).astype(BF16)
    b_r = jnp.pad(b_router, (0, LANES - n_exp))
    x1, h2, logits = _post(y, r, kmod, v, g, p, x2, g1, sc2, sh2, r_k.reshape(-1), lnx_g, lnx_b, ones_bd,
                           w_rwkv_out.astype(BF16), w_pool.astype(BF16), pool_scale, w_out.astype(BF16),
                           ln1_g, ln1_b, w_r, b_r, seq, alpha, gate_blk=0, pool_blk=(4 * d) // pw)

    idx, prob, rank, cnt = _route(logits, n_exp)
    counts = cnt[0, :n_exp].astype(jnp.int32)
    padded = (counts + MOE_SUB - 1) // MOE_SUB * MOE_SUB
    pad_end = jnp.cumsum(padded)
    pad_start = pad_end - padded
    mk = m * TOP_K
    n_rows = (mk // MOE_SUB + n_exp) * MOE_SUB
    dest = (pad_start[idx[:, :TOP_K]] + rank[:, :TOP_K]).reshape(mk).astype(jnp.int32)
    src_tok = jnp.zeros((n_rows + MOE_RMAX,), jnp.int32).at[dest].set(jnp.arange(mk, dtype=jnp.int32) // TOP_K)
    n_items = mk // MOE_RMAX + n_exp
    per_e = (padded + MOE_RMAX - 1) // MOE_RMAX
    it_end = jnp.cumsum(per_e)
    it_start = it_end - per_e
    ids = jnp.arange(n_items, dtype=jnp.int32)
    item_e = jnp.minimum(jnp.searchsorted(it_end, ids, side='right'), n_exp - 1).astype(jnp.int32)
    local = ids - it_start[item_e]
    used = ids < it_end[-1]
    item_start = jnp.where(used, pad_start[item_e] + local * MOE_RMAX, n_rows).astype(jnp.int32)
    item_nsub = jnp.where(used, jnp.clip((padded[item_e] - local * MOE_RMAX) // MOE_SUB, 0,
                                         MOE_RMAX // MOE_SUB), 0).astype(jnp.int32)

    x_buf = _dispatch(src_tok, h2, n_rows + MOE_RMAX)
    y_buf = _experts(item_e, item_start, item_nsub, pad_end[-1:].astype(jnp.int32), x_buf,
                     w_gu, b_gu, w_down, b_down, n_rows)
    return _combine(dest, y_buf, prob, x1, g2, ln2_g, ln2_b, seq, alpha)


def kernel(x, c, w_ada, b_ada, w_in, mu_shift, w0, w_decay_up, a0, w_iclr_up, w_glora_up, k_k, k_a, r_k, lnx_g, lnx_b, w_rwkv_out, w_pool, pool_scale, w_out, ln1_g, ln1_b, w_router, b_router, w_gu, b_gu, w_down, b_down, ln2_g, ln2_b):
    bsz, seq, d = x.shape
    depth = w_ada.shape[0]
    alpha = float((2 * depth) ** 0.25)
    c8 = jnp.pad(c, ((0, 8 - bsz), (0, 0)))
    x2 = x.reshape(bsz * seq, d)
    weights = (w_ada, b_ada, w_in, mu_shift, w0, w_decay_up, a0, w_iclr_up, w_glora_up, k_k, k_a, r_k,
               lnx_g, lnx_b, w_rwkv_out, w_pool, pool_scale, w_out, ln1_g, ln1_b, w_router, b_router,
               w_gu, b_gu, w_down, b_down, ln2_g, ln2_b)
    for l in range(depth):
        x2 = _layer(x2, c8, bsz, seq, alpha, *[w[l] for w in weights])
    return x2.reshape(bsz, seq, d)
```

```python
import functools

import jax
import jax.numpy as jnp
from jax import lax
from jax.experimental import pallas as pl
from jax.experimental.pallas import tpu as pltpu

F32 = jnp.float32
BF16 = jnp.bfloat16

HEAD_SIZE = 64
LANES = 128
TOP_K = 4
LN_EPS = 1e-5
GN_EPS = 64e-5
POOL_WINDOWS = (2, 4, 8, 16)
SWIGLU_LIMIT = 7.0
SWIGLU_ALPHA = 1.702
WKV_CHUNK = 64
MOE_SUB = 256
MOE_RMAX = 1536
MOE_TF = 512
VMEM_LIMIT = 56 * 1024 * 1024

_NT = (((1,), (1,)), ((), ()))
_TN = (((0,), (0,)), ((), ()))


def _cparams(*sem):
    return pltpu.CompilerParams(dimension_semantics=sem, vmem_limit_bytes=VMEM_LIMIT)


def _sigmoid(z):
    return 1.0 / (1.0 + jnp.exp(-z))


def _dot(a, b):
    return jnp.dot(a, b, preferred_element_type=F32)


def _split_dot(xv, w):
    hi = xv.astype(BF16)
    lo = (xv - hi.astype(F32)).astype(BF16)
    return _dot(hi, w) + _dot(lo, w)


def _head_sum(xv, ones_bd):
    outs = []
    for p in range(xv.shape[1] // LANES):
        outs.append(_split_dot(xv[:, p * LANES:(p + 1) * LANES], ones_bd))
    return jnp.concatenate(outs, axis=1)


def _adaln_kernel(c_ref, w_ref, b_ref, o_ref):
    c = c_ref[...]
    cond = c * _sigmoid(c)
    o_ref[...] = _dot(cond.astype(BF16), w_ref[...].astype(BF16)) + b_ref[...]


def _adaln(c8, w, b):
    d, n = w.shape
    tn = 1024
    return pl.pallas_call(
        _adaln_kernel,
        grid=(n // tn,),
        in_specs=[pl.BlockSpec((8, d), lambda j: (0, 0)),
                  pl.BlockSpec((d, tn), lambda j: (0, j)),
                  pl.BlockSpec((1, tn), lambda j: (0, j))],
        out_specs=pl.BlockSpec((8, tn), lambda j: (0, j)),
        out_shape=jax.ShapeDtypeStruct((8, n), F32),
        compiler_params=_cparams("arbitrary"),
        name="adaln",
    )(c8, w, b.reshape(1, n))


def _inproj_kernel(x_ref, sc_ref, sh_ref, w_ref, o_ref, h_scr):
    @pl.when(pl.program_id(1) == 0)
    def _():
        h_scr[...] = (x_ref[...] * (1.0 + sc_ref[0]) + sh_ref[0]).astype(BF16)

    o_ref[...] = _dot(h_scr[...], w_ref[...])


def _inproj(x2, sc, sh, w, seq):
    m, d = x2.shape
    n = w.shape[1]
    tm, tn = 1024, 1024
    tpb = seq // tm
    return pl.pallas_call(
        _inproj_kernel,
        grid=(m // tm, n // tn),
        in_specs=[pl.BlockSpec((tm, d), lambda i, j: (i, 0)),
                  pl.BlockSpec((1, 1, d), lambda i, j: (i // tpb, 0, 0)),
                  pl.BlockSpec((1, 1, d), lambda i, j: (i // tpb, 0, 0)),
                  pl.BlockSpec((d, tn), lambda i, j: (0, j))],
        out_specs=pl.BlockSpec((tm, tn), lambda i, j: (i, j)),
        out_shape=jax.ShapeDtypeStruct((m, n), F32),
        scratch_shapes=[pltpu.VMEM((tm, d), BF16)],
        compiler_params=_cparams("arbitrary", "arbitrary"),
        name="inproj",
    )(x2, sc, sh, w)


def _prep_kernel(p_ref, prev_ref, mu_ref, w0_ref, a0_ref, kk_ref, ka_ref, wd_ref, wa_ref, wg_ref,
                 ones_ref, r_out, lw_out, k_out, v_out, kkn_out, b_out, g_out, *, tpb, c):
    cur = p_ref[...]
    first = (pl.program_id(0) % tpb) == 0
    prev_row = jnp.where(first, 0.0, prev_ref[7:8, :])
    shifted = pltpu.roll(cur, 1, axis=0)
    row0 = lax.broadcasted_iota(jnp.int32, cur.shape, 0) == 0
    shifted = jnp.where(row0, prev_row, shifted)
    ps = cur + (shifted - cur) * mu_ref[...]
    r = ps[:, 0:c]
    k = ps[:, c:2 * c]
    v = ps[:, 2 * c:3 * c]
    o = 3 * c
    wdp = ps[:, o:o + LANES]
    adp = ps[:, o + LANES:o + 2 * LANES]
    gdp = ps[:, o + 2 * LANES:o + 4 * LANES]
    z = w0_ref[...] + _dot(jnp.tanh(wdp).astype(BF16), wd_ref[...])
    lw = -jnp.exp(-0.5) * _sigmoid(z)
    iclr = _sigmoid(a0_ref[...] + _dot(adp.astype(BF16), wa_ref[...]))
    g = _dot(_sigmoid(gdp).astype(BF16), wg_ref[...])
    kk0 = k * kk_ref[...]
    ss = _head_sum(kk0 * kk0, ones_ref[...])
    kkn = kk0 / jnp.maximum(jnp.sqrt(ss), 1e-12)
    kmod = k * (1.0 + (iclr - 1.0) * ka_ref[...])
    r_out[...] = r
    lw_out[...] = lw
    k_out[...] = kmod
    v_out[...] = v
    kkn_out[...] = kkn
    b_out[...] = kkn * iclr
    g_out[...] = g


def _prep(p, mu, w0, a0, k_k, k_a, wd, wa, wg, ones_bd, seq, c, shift_blk):
    m = p.shape[0]
    ns = mu.shape[1]
    tm = 256
    tpb = seq // tm
    row = lambda a: a.reshape(1, -1)
    full = lambda a: pl.BlockSpec(a.shape, lambda i: (0,) * a.ndim)
    out = jax.ShapeDtypeStruct((m, c), F32)
    ospec = pl.BlockSpec((tm, c), lambda i: (i, 0))
    args = (p, p, mu, row(w0), row(a0), row(k_k), row(k_a), wd, wa, wg, ones_bd)
    in_specs = [pl.BlockSpec((tm, ns), lambda i: (i, shift_blk)),
                pl.BlockSpec((8, ns), lambda i: (jnp.maximum(i * (tm // 8) - 1, 0), shift_blk))]
    in_specs += [full(a) for a in args[2:]]
    return pl.pallas_call(
        functools.partial(_prep_kernel, tpb=tpb, c=c),
        grid=(m // tm,),
        in_specs=in_specs,
        out_specs=[ospec] * 7,
        out_shape=[out] * 7,
        compiler_params=_cparams("arbitrary"),
        name="rwkv_prep",
    )(*args)


def _wkv_kernel(r_ref, lw_ref, k_ref, v_ref, kk_ref, b_ref, y_ref, s_scr, *, chunk, c):
    L = chunk

    @pl.when(pl.program_id(1) == 0)
    def _():
        s_scr[...] = jnp.zeros_like(s_scr)

    ri = lax.broadcasted_iota(jnp.int32, (L, L), 0)
    ci = lax.broadcasted_iota(jnp.int32, (L, L), 1)
    tri = (ri >= ci).astype(BF16)
    lw = lw_ref[...]
    cs = _split_dot_lhs(tri, lw)
    cs_last = cs[L - 1:L, :]
    g_in = jnp.exp(cs)
    g_ex = jnp.exp(cs - lw)
    g_inv = jnp.exp(-cs)
    g_tail = jnp.exp(cs_last - cs)
    g_last = jnp.exp(cs_last)
    kk = kk_ref[...]
    bb = b_ref[...]
    kx = k_ref[...]
    vv = v_ref[...]
    a_t = -kk * g_ex
    r_t = r_ref[...] * g_in
    b_t = bb * g_inv
    k_t = kx * g_inv
    b_p = bb * g_tail
    k_p = kx * g_tail

    r2 = lax.broadcasted_iota(jnp.int32, (2 * L, 2 * L), 0)
    c2 = lax.broadcasted_iota(jnp.int32, (2 * L, 2 * L), 1)
    same = (r2 >= L) == (c2 >= L)
    m_strict = same & (r2 > c2)
    m_incl = same & (r2 >= c2)
    lane = lax.broadcasted_iota(jnp.int32, (L, LANES), 1)
    low = lane < HEAD_SIZE

    def stack(zv):
        return jnp.concatenate([jnp.where(low, zv, 0.0), jnp.where(low, 0.0, zv)], axis=0)

    pairs = range(c // LANES)
    sls = [slice(p * LANES, (p + 1) * LANES) for p in pairs]
    a_s = [stack(a_t[:, sl]) for sl in sls]
    r_s = [stack(r_t[:, sl]) for sl in sls]
    v_s = [stack(vv[:, sl]) for sl in sls]
    ar = [jnp.concatenate([a_s[p], r_s[p]], axis=0).astype(BF16) for p in pairs]
    qb = [lax.dot_general(ar[p], stack(b_t[:, sls[p]]).astype(BF16), _NT, preferred_element_type=F32)
          for p in pairs]
    qk = [lax.dot_general(ar[p], stack(k_t[:, sls[p]]).astype(BF16), _NT, preferred_element_type=F32)
          for p in pairs]
    a_ab = [jnp.where(m_strict, q[:2 * L], 0.0) for q in qb]
    a_rb = [jnp.where(m_incl, q[2 * L:], 0.0) for q in qb]
    a_ak = [jnp.where(m_strict, q[:2 * L], 0.0) for q in qk]
    a_rk = [jnp.where(m_incl, q[2 * L:], 0.0) for q in qk]
    v_b = [x.astype(BF16) for x in v_s]
    z = [jnp.concatenate([_dot(a_ak[p].astype(BF16), v_b[p]), a_s[p]], axis=1) for p in pairs]
    amat = a_ab
    n_sq = max(L.bit_length() - 1, 1)
    for it in range(n_sq):
        ab = [x.astype(BF16) for x in amat]
        z = [z[p] + _dot(ab[p], z[p].astype(BF16)) for p in pairs]
        if it + 1 < n_sq:
            amat = [_dot(x, x) for x in ab]
    s_old = [s_scr[p] for p in pairs]
    s_b = [x.astype(BF16) for x in s_old]
    u_s = [_dot(z[p][:, LANES:].astype(BF16), s_b[p]) + z[p][:, :LANES] for p in pairs]
    uv = [jnp.concatenate([u_s[p], v_s[p]], axis=0).astype(BF16) for p in pairs]
    y_s = [_dot(r_s[p].astype(BF16), s_b[p])
           + _dot(jnp.concatenate([a_rb[p], a_rk[p]], axis=1).astype(BF16), uv[p]) for p in pairs]
    for p in pairs:
        y_ref[:, sls[p]] = y_s[p][:L] + y_s[p][L:]
    upd = [lax.dot_general(jnp.concatenate([stack(b_p[:, sls[p]]), stack(k_p[:, sls[p]])], axis=0).astype(BF16),
                           uv[p], _TN, preferred_element_type=F32) for p in pairs]
    for p in pairs:
        g_col = jnp.transpose(jnp.broadcast_to(g_last[:, sls[p]], (LANES, LANES)))
        s_scr[p] = s_old[p] * g_col + upd[p]


def _split_dot_lhs(w, xv):
    hi = xv.astype(BF16)
    lo = (xv - hi.astype(F32)).astype(BF16)
    return _dot(w, hi) + _dot(w, lo)


def _wkv(r, lw, k, v, kk, b, bsz, seq, c):
    L = WKV_CHUNK
    nc = seq // L
    spec = pl.BlockSpec((L, c), lambda bi, ci: (bi * nc + ci, 0))
    return pl.pallas_call(
        functools.partial(_wkv_kernel, chunk=L, c=c),
        grid=(bsz, nc),
        in_specs=[spec] * 6,
        out_specs=spec,
        out_shape=jax.ShapeDtypeStruct((bsz * seq, c), F32),
        scratch_shapes=[pltpu.VMEM((c // LANES, LANES, LANES), F32)],
        compiler_params=_cparams("arbitrary", "arbitrary"),
        name="wkv7",
    )(r, lw, k, v, kk, b)


def _post_kernel(y_ref, r_ref, k_ref, v_ref, g_ref, gate_ref, pool_ref, halo_ref, x_ref,
                 g1_ref, sc2_ref, sh2_ref, rk_ref, lng_ref, lnb_ref, ones_ref, wro_ref, wp_ref,
                 psc_ref, wo_ref, l1g_ref, l1b_ref, wr_ref, br_ref,
                 x1_out, h2_out, lg_out, pool_scr, *, tpb, alpha, d, tm):
    ones_bd = ones_ref[...]
    inv_n = 1.0 / HEAD_SIZE
    y = y_ref[...]
    mu = _head_sum(y, ones_bd) * inv_n
    dy = y - mu
    var = _head_sum(dy * dy, ones_bd) * inv_n
    yn = dy * lax.rsqrt(var + GN_EPS) * lng_ref[...] + lnb_ref[...]
    vv = v_ref[...]
    bonus = _head_sum(r_ref[...] * k_ref[...] * rk_ref[...], ones_bd) * vv
    y_rwkv = ((yn + bonus) * g_ref[...]).astype(BF16)
    branch_a = _dot(y_rwkv, wro_ref[...])

    first = (pl.program_id(0) % tpb) == 0
    halo = 16
    pool_scr[0:halo, :] = jnp.where(first, 0.0, halo_ref[...])
    pool_scr[halo:, :] = pool_ref[...]
    pos = (pl.program_id(0) % tpb) * tm + lax.broadcasted_iota(jnp.int32, (tm, 1), 0)
    gw = pool_ref.shape[1] // len(POOL_WINDOWS)
    dg = d // len(POOL_WINDOWS)
    outs = []
    for gi, win in enumerate(POOL_WINDOWS):
        cols = slice(gi * gw, (gi + 1) * gw)
        u = pool_scr[halo:, cols]
        wsum = u
        for s in range(1, win):
            wsum = wsum + pool_scr[halo - s:halo - s + tm, cols]
        cnt = jnp.minimum(pos + 1, win).astype(F32)
        dd = (wsum / cnt - u).astype(BF16)
        outs.append(_dot(dd, wp_ref[gi]))
    branch_b = jnp.concatenate(outs, axis=1) * psc_ref[...]

    gates = gate_ref[...]
    mixin = _sigmoid(gates[:, :d]) * branch_a + _sigmoid(gates[:, d:]) * branch_b
    mix = _dot(mixin.astype(BF16), wo_ref[...])
    xr = alpha * x_ref[...] + g1_ref[0] * mix
    m1 = jnp.mean(xr, axis=-1, keepdims=True)
    xc = xr - m1
    v1 = jnp.mean(xc * xc, axis=-1, keepdims=True)
    x1 = xc * lax.rsqrt(v1 + LN_EPS) * l1g_ref[...] + l1b_ref[...]
    x1_out[...] = x1
    h2 = x1 * (1.0 + sc2_ref[0]) + sh2_ref[0]
    h2_out[...] = h2
    lg_out[...] = _dot(h2.astype(BF16), wr_ref[...]) + br_ref[...]


def _post(y, r, k, v, g, p, x2, g1, sc2, sh2, r_k, lnx_g, lnx_b, ones_bd, w_ro, w_pool, pool_scale,
          w_out, ln1_g, ln1_b, w_r, b_r, seq, alpha, gate_blk, pool_blk):
    m, d = x2.shape
    c = y.shape[1]
    pw = w_pool.shape[0] * w_pool.shape[1]
    tm = 256
    tpb = seq // tm
    row = lambda a: a.reshape(1, -1)
    const = lambda a: pl.BlockSpec(a.shape, lambda i: (0,) * a.ndim, pipeline_mode=pl.Buffered(1))
    tile = lambda w: pl.BlockSpec((tm, w), lambda i: (i, 0))
    bvec = pl.BlockSpec((1, 1, d), lambda i: (i // tpb, 0, 0))
    consts = (row(r_k), row(lnx_g), row(lnx_b), ones_bd, w_ro, w_pool, row(pool_scale), w_out,
              row(ln1_g), row(ln1_b), w_r, row(b_r))
    in_specs = [tile(c)] * 5 + [
        pl.BlockSpec((tm, 2 * d), lambda i: (i, gate_blk)),
        pl.BlockSpec((tm, pw), lambda i: (i, pool_blk)),
        pl.BlockSpec((16, pw), lambda i: (jnp.maximum(i * (tm // 16) - 1, 0), pool_blk)),
        tile(d), bvec, bvec, bvec] + [const(a) for a in consts]
    return pl.pallas_call(
        functools.partial(_post_kernel, tpb=tpb, alpha=alpha, d=d, tm=tm),
        grid=(m // tm,),
        in_specs=in_specs,
        out_specs=[tile(d), tile(d), tile(LANES)],
        out_shape=[jax.ShapeDtypeStruct((m, d), F32), jax.ShapeDtypeStruct((m, d), F32),
                   jax.ShapeDtypeStruct((m, LANES), F32)],
        scratch_shapes=[pltpu.VMEM((tm + 16, pw), F32)],
        compiler_params=_cparams("arbitrary"),
        name="merge_ln1",
    )(y, r, k, v, g, p, p, p, x2, g1, sc2, sh2, *consts)


def _route_kernel(lg_ref, idx_out, prob_out, rank_out, cnt_out, carry, *, n_exp, tm):
    @pl.when(pl.program_id(0) == 0)
    def _():
        carry[...] = jnp.zeros_like(carry)

    lane = lax.broadcasted_iota(jnp.int32, (tm, LANES), 1)
    lane_f = lane.astype(F32)
    neg = jnp.float32(-jnp.inf)
    work = jnp.where(lane < n_exp, lg_ref[...], neg)
    vals, sels = [], []
    idx_acc = jnp.zeros((tm, LANES), jnp.int32)
    hot = jnp.zeros((tm, LANES), F32)
    for kk in range(TOP_K):
        mx = jnp.max(work, axis=-1, keepdims=True)
        idx = jnp.min(jnp.where(work == mx, lane_f, float(LANES)), axis=-1, keepdims=True).astype(jnp.int32)
        sel = lane == idx
        work = jnp.where(sel, neg, work)
        vals.append(mx)
        sels.append(sel)
        idx_acc = jnp.where(lane == kk, idx, idx_acc)
        hot = hot + sel.astype(F32)
    es = [jnp.exp(vv - vals[0]) for vv in vals]
    den = es[0] + es[1] + es[2] + es[3]
    prob = jnp.zeros((tm, LANES), F32)
    for kk in range(TOP_K):
        prob = jnp.where(lane == kk, es[kk] / den, prob)
    ri = lax.broadcasted_iota(jnp.int32, (tm, tm), 0)
    ci = lax.broadcasted_iota(jnp.int32, (tm, tm), 1)
    before = _dot((ri > ci).astype(BF16), hot.astype(BF16)) + carry[0:1, :]
    rank = jnp.zeros((tm, LANES), jnp.int32)
    for kk in range(TOP_K):
        rk = jnp.sum(jnp.where(sels[kk], before, 0.0), axis=-1, keepdims=True).astype(jnp.int32)
        rank = jnp.where(lane == kk, rk, rank)
    carry[...] = carry[...] + jnp.sum(hot, axis=0, keepdims=True)
    idx_out[...] = idx_acc
    prob_out[...] = prob
    rank_out[...] = rank
    cnt_out[...] = carry[...]


def _route(logits, n_exp):
    m = logits.shape[0]
    tm = 512
    tile = pl.BlockSpec((tm, LANES), lambda i: (i, 0))
    return pl.pallas_call(
        functools.partial(_route_kernel, n_exp=n_exp, tm=tm),
        grid=(m // tm,),
        in_specs=[tile],
        out_specs=[tile, tile, tile, pl.BlockSpec((8, LANES), lambda i: (0, 0))],
        out_shape=[jax.ShapeDtypeStruct((m, LANES), jnp.int32), jax.ShapeDtypeStruct((m, LANES), F32),
                   jax.ShapeDtypeStruct((m, LANES), jnp.int32), jax.ShapeDtypeStruct((8, LANES), F32)],
        scratch_shapes=[pltpu.VMEM((8, LANES), F32)],
        compiler_params=_cparams("arbitrary"),
        name="route",
    )(logits)


def _row_copy(src_hbm, dst, src_row, dst_row, sem):
    return pltpu.make_async_copy(src_hbm.at[pl.ds(src_row, 1), :], dst.at[pl.ds(dst_row, 1), :], sem)


def _dispatch_kernel(src_ref, h_hbm, o_ref, buf, sems, *, rows):
    i = pl.program_id(0)
    slot = i % 2

    def issue_block(blk, slot_):
        def body(t, carry):
            _row_copy(h_hbm, buf.at[slot_], src_ref[blk * rows + t], t, sems.at[slot_]).start()
            return carry

        lax.fori_loop(0, rows, body, 0, unroll=8)

    @pl.when(i == 0)
    def _():
        issue_block(0, 0)

    @pl.when(i + 1 < pl.num_programs(0))
    def _():
        issue_block(i + 1, 1 - slot)

    def drain(t, carry):
        _row_copy(h_hbm, buf.at[slot], 0, t, sems.at[slot]).wait()
        return carry

    lax.fori_loop(0, rows, drain, 0, unroll=8)
    o_ref[...] = buf[slot].astype(BF16)


def _dispatch(src_tok, h2, n_rows):
    d = h2.shape[1]
    rows = MOE_SUB
    return pl.pallas_call(
        functools.partial(_dispatch_kernel, rows=rows),
        grid_spec=pltpu.PrefetchScalarGridSpec(
            num_scalar_prefetch=1,
            grid=(n_rows // rows,),
            in_specs=[pl.BlockSpec(memory_space=pl.ANY)],
            out_specs=pl.BlockSpec((rows, d), lambda i, src: (i, 0)),
            scratch_shapes=[pltpu.VMEM((2, rows, d), F32), pltpu.SemaphoreType.DMA((2,))]),
        out_shape=jax.ShapeDtypeStruct((n_rows, d), BF16),
        compiler_params=_cparams("arbitrary"),
        name="dispatch",
    )(src_tok, h2)


def _roll_lanes(v, shift):
    return jnp.concatenate([pltpu.roll(v[:, g * LANES:(g + 1) * LANES], shift, axis=1)
                            for g in range(v.shape[1] // LANES)], axis=1)


def _expert_kernel(e_ref, st_ref, ns_ref, tot_ref, x_ref, wgu_ref, bgu_ref, wd_ref, bd_ref,
                   y_hbm, acc, wgu_b, wd_i, wd_b, sem, *, n_ff, rmax, sub, n_rows):
    i = pl.program_id(0)
    j = pl.program_id(1)
    nsub = ns_ref[i]
    n_sub_max = rmax // sub
    tfc = wd_ref.shape[1]
    half = tfc // 2

    def out_copy(row0, s):
        return pltpu.make_async_copy(acc.at[pl.ds(s * sub, sub), :],
                                     y_hbm.at[pl.ds(pl.multiple_of(row0, sub), sub), :], sem)

    def for_valid_subs(item, fn):
        for s in range(n_sub_max):
            @pl.when(s < ns_ref[item])
            def _():
                fn(out_copy(st_ref[item] + s * sub, s))

    @pl.when(j == 0)
    def _():
        @pl.when(i > 0)
        def _():
            for_valid_subs(i - 1, lambda cp: cp.wait())

        acc[...] = jnp.broadcast_to(bd_ref[0], acc.shape)

    @pl.when(nsub > 0)
    def _():
        wgu_b[...] = wgu_ref[0].astype(BF16)
        for g in range(wd_b.shape[1] // LANES):
            cols = slice(g * LANES, (g + 1) * LANES)
            wd_i[g % 2, pl.ds(0, half, stride=2), :] = wd_ref[0, 0:half, cols]
            wd_i[g % 2, pl.ds(1, half, stride=2), :] = wd_ref[0, half:tfc, cols]
            wd_b[:, cols] = wd_i[g % 2].astype(BF16)

    even = (lax.broadcasted_iota(jnp.int32, (sub, tfc), 1) % 2) == 0
    for s in range(n_sub_max):
        @pl.when(s < nsub)
        def _():
            rows = slice(s * sub, (s + 1) * sub)
            gu = _dot(x_ref[rows, :], wgu_b[...]) + bgu_ref[0]
            gate = jnp.minimum(gu, SWIGLU_LIMIT)
            glu = gate * _sigmoid(gate * SWIGLU_ALPHA)
            up1 = jnp.clip(gu, -SWIGLU_LIMIT, SWIGLU_LIMIT) + 1.0
            prod = glu * _roll_lanes(up1, LANES - 1)
            act = jnp.where(even, prod[:, :tfc], _roll_lanes(prod[:, tfc:], 1))
            acc[rows, :] += _dot(act.astype(BF16), wd_b[...])

    @pl.when(j == n_ff - 1)
    def _():
        for_valid_subs(i, lambda cp: cp.start())

        @pl.when(i == pl.num_programs(0) - 1)
        def _():
            for_valid_subs(i, lambda cp: cp.wait())
            acc[0:sub, :] = jnp.zeros((sub, acc.shape[1]), F32)
            first = tot_ref[0] // sub

            def fill_start(blk, carry):
                out_copy(blk * sub, 0).start()
                return carry

            def fill_wait(blk, carry):
                out_copy(blk * sub, 0).wait()
                return carry

            lax.fori_loop(first, n_rows // sub, fill_start, 0)
            lax.fori_loop(first, n_rows // sub, fill_wait, 0)


def _experts(item_e, item_start, item_nsub, total_rows, x_buf, w_gu, b_gu, w_down, b_down, n_rows):
    n_items = item_e.shape[0]
    n_exp, dff, d = w_down.shape
    tf = MOE_TF
    n_ff = dff // tf
    rmax = MOE_RMAX

    def jj(i, j, ns):
        return jnp.where(ns[i] > 0, j, n_ff - 1)

    in_specs = [
        pl.BlockSpec((pl.Element(rmax), pl.Element(d)),
                     lambda i, j, e, st, ns, tot: (pl.multiple_of(st[i], MOE_SUB), 0),
                     pipeline_mode=pl.Buffered(1)),
        pl.BlockSpec((1, d, 2 * tf), lambda i, j, e, st, ns, tot: (e[i], 0, jj(i, j, ns))),
        pl.BlockSpec((1, 1, 2 * tf), lambda i, j, e, st, ns, tot: (e[i], 0, jj(i, j, ns))),
        pl.BlockSpec((1, tf, d), lambda i, j, e, st, ns, tot: (e[i], jj(i, j, ns), 0)),
        pl.BlockSpec((1, 1, d), lambda i, j, e, st, ns, tot: (e[i], 0, 0)),
    ]
    return pl.pallas_call(
        functools.partial(_expert_kernel, n_ff=n_ff, rmax=rmax, sub=MOE_SUB, n_rows=n_rows),
        grid_spec=pltpu.PrefetchScalarGridSpec(
            num_scalar_prefetch=4,
            grid=(n_items, n_ff),
            in_specs=in_specs,
            out_specs=pl.BlockSpec(memory_space=pl.ANY),
            scratch_shapes=[pltpu.VMEM((rmax, d), F32), pltpu.VMEM((d, 2 * tf), BF16),
                            pltpu.VMEM((2, tf, LANES), F32), pltpu.VMEM((tf, d), BF16),
                            pltpu.SemaphoreType.DMA(())]),
        out_shape=jax.ShapeDtypeStruct((n_rows, d), F32),
        compiler_params=_cparams("arbitrary", "arbitrary"),
        name="experts",
    )(item_e, item_start, item_nsub, total_rows, x_buf, w_gu, b_gu.reshape(n_exp, 1, 2 * dff),
      w_down, b_down.reshape(n_exp, 1, d))


def _combine_kernel(dest_ref, y_hbm, prob_ref, x1_ref, g2_ref, lg_ref, lb_ref, o_ref, buf, sems,
                    *, tm, alpha):
    i = pl.program_id(0)
    slot = i % 2

    def issue_block(blk, slot_):
        def body(t, carry):
            for kk in range(TOP_K):
                _row_copy(y_hbm, buf.at[slot_, kk], dest_ref[(blk * tm + t) * TOP_K + kk], t,
                          sems.at[slot_]).start()
            return carry

        lax.fori_loop(0, tm, body, 0, unroll=2)

    @pl.when(i == 0)
    def _():
        issue_block(0, 0)

    @pl.when(i + 1 < pl.num_programs(0))
    def _():
        issue_block(i + 1, 1 - slot)

    def drain(t, carry):
        for kk in range(TOP_K):
            _row_copy(y_hbm, buf.at[slot, kk], 0, t, sems.at[slot]).wait()
        return carry

    lax.fori_loop(0, tm, drain, 0, unroll=2)
    prob = prob_ref[...]
    f = prob[:, 0:1] * buf[slot, 0]
    for kk in range(1, TOP_K):
        f = f + prob[:, kk:kk + 1] * buf[slot, kk]
    xr = alpha * x1_ref[...] + g2_ref[0] * f
    m2 = jnp.mean(xr, axis=-1, keepdims=True)
    xc = xr - m2
    v2 = jnp.mean(xc * xc, axis=-1, keepdims=True)
    o_ref[...] = xc * lax.rsqrt(v2 + LN_EPS) * lg_ref[...] + lb_ref[...]


def _combine(dest, y_buf, prob, x1, g2, ln_g, ln_b, seq, alpha):
    m, d = x1.shape
    tm = 128
    tpb = seq // tm
    return pl.pallas_call(
        functools.partial(_combine_kernel, tm=tm, alpha=alpha),
        grid_spec=pltpu.PrefetchScalarGridSpec(
            num_scalar_prefetch=1,
            grid=(m // tm,),
            in_specs=[pl.BlockSpec(memory_space=pl.ANY),
                      pl.BlockSpec((tm, LANES), lambda i, dst: (i, 0)),
                      pl.BlockSpec((tm, d), lambda i, dst: (i, 0)),
                      pl.BlockSpec((1, 1, d), lambda i, dst: (i // tpb, 0, 0)),
                      pl.BlockSpec((1, d), lambda i, dst: (0, 0)),
                      pl.BlockSpec((1, d), lambda i, dst: (0, 0))],
            out_specs=pl.BlockSpec((tm, d), lambda i, dst: (i, 0)),
            scratch_shapes=[pltpu.VMEM((2, TOP_K, tm, d), F32), pltpu.SemaphoreType.DMA((2,))]),
        out_shape=jax.ShapeDtypeStruct((m, d), F32),
        compiler_params=_cparams("arbitrary"),
        name="combine_ln2",
    )(dest, y_buf, prob, x1, g2, ln_g.reshape(1, d), ln_b.reshape(1, d))


def _pad_rows(w, rows):
    return jnp.pad(w, ((0, rows - w.shape[0]), (0, 0)))


def _pad_cols(w, cols):
    return jnp.pad(w, ((0, 0), (0, cols - w.shape[1])))


def _layer(x2, c8, bsz, seq, alpha, w_ada, b_ada, w_in, mu_shift, w0, w_decay_up, a0, w_iclr_up,
           w_glora_up, k_k, k_a, r_k, lnx_g, lnx_b, w_rwkv_out, w_pool, pool_scale, w_out, ln1_g,
           ln1_b, w_router, b_router, w_gu, b_gu, w_down, b_down, ln2_g, ln2_b):
    m, d = x2.shape
    c = w0.shape[0]
    n_dec, n_icl, n_gate = w_decay_up.shape[0], w_iclr_up.shape[0], w_glora_up.shape[0]
    pw = w_pool.shape[0] * w_pool.shape[1]
    n_exp = w_router.shape[1]
    assert n_dec <= LANES and n_icl <= LANES and n_gate <= 2 * LANES and d == 2 * c and pw == c

    mod = _adaln(c8, w_ada, b_ada)[:bsz]
    sh1, sc1, g1, sh2, sc2, g2 = [t.reshape(bsz, 1, d) for t in jnp.split(mod, 6, axis=-1)]

    o = 3 * c
    segs = [(o, n_dec, LANES), (o + n_dec, n_icl, LANES), (o + n_dec + n_icl, n_gate, 2 * LANES)]
    n_shift = o + n_dec + n_icl + n_gate
    zpad = 2 * d - 3 * c - 4 * LANES
    w_shift = jnp.concatenate([w_in[:, :o]] + [_pad_cols(w_in[:, s:s + n], wd) for s, n, wd in segs]
                              + [jnp.zeros((d, zpad), F32)], axis=1)
    mu = jnp.concatenate([mu_shift[:o]] + [jnp.pad(mu_shift[s:s + n], (0, wd - n)) for s, n, wd in segs]
                         + [jnp.zeros((zpad,), F32)]).reshape(1, -1)
    w_cat = jnp.concatenate([w_in[:, n_shift + pw:], w_shift, w_in[:, n_shift:n_shift + pw]],
                            axis=1).astype(BF16)
    p = _inproj(x2, sc1, sh1, w_cat, seq)

    ones_bd = jnp.kron(jnp.eye(LANES // HEAD_SIZE, dtype=F32), jnp.ones((HEAD_SIZE, HEAD_SIZE), F32)).astype(BF16)
    r, lw, kmod, v, kkn, bvec, g = _prep(
        p, mu, w0, a0, k_k, k_a, _pad_rows(w_decay_up, LANES).astype(BF16),
        _pad_rows(w_iclr_up, LANES).astype(BF16), _pad_rows(w_glora_up, 2 * LANES).astype(BF16),
        ones_bd, seq, c, shift_blk=1)
    y = _wkv(r, lw, kmod, v, kkn, bvec, bsz, seq, c)

    w_r = _pad_cols(w_router, LANES).astype(BF16)
    b_r = jnp.pad(b_router, (0, LANES - n_exp))
    x1, h2, logits = _post(y, r, kmod, v, g, p, x2, g1, sc2, sh2, r_k.reshape(-1), lnx_g, lnx_b, ones_bd,
                           w_rwkv_out.astype(BF16), w_pool.astype(BF16), pool_scale, w_out.astype(BF16),
                           ln1_g, ln1_b, w_r, b_r, seq, alpha, gate_blk=0, pool_blk=(4 * d) // pw)

    idx, prob, rank, cnt = _route(logits, n_exp)
    counts = cnt[0, :n_exp].astype(jnp.int32)
    padded = (counts + MOE_SUB - 1) // MOE_SUB * MOE_SUB
    pad_end = jnp.cumsum(padded)
    pad_start = pad_end - padded
    mk = m * TOP_K
    n_rows = (mk // MOE_SUB + n_exp) * MOE_SUB
    dest = (pad_start[idx[:, :TOP_K]] + rank[:, :TOP_K]).reshape(mk).astype(jnp.int32)
    src_tok = jnp.zeros((n_rows + MOE_RMAX,), jnp.int32).at[dest].set(jnp.arange(mk, dtype=jnp.int32) // TOP_K)
    n_items = mk // MOE_RMAX + n_exp
    per_e = (padded + MOE_RMAX - 1) // MOE_RMAX
    it_end = jnp.cumsum(per_e)
    it_start = it_end - per_e
    ids = jnp.arange(n_items, dtype=jnp.int32)
    item_e = jnp.minimum(jnp.searchsorted(it_end, ids, side='right'), n_exp - 1).astype(jnp.int32)
    local = ids - it_start[item_e]
    used = ids < it_end[-1]
    item_start = jnp.where(used, pad_start[item_e] + local * MOE_RMAX, n_rows).astype(jnp.int32)
    item_nsub = jnp.where(used, jnp.clip((padded[item_e] - local * MOE_RMAX) // MOE_SUB, 0,
                                         MOE_RMAX // MOE_SUB), 0).astype(jnp.int32)

    x_buf = _dispatch(src_tok, h2, n_rows + MOE_RMAX)
    y_buf = _experts(item_e, item_start, item_nsub, pad_end[-1:].astype(jnp.int32), x_buf,
                     w_gu, b_gu, w_down, b_down, n_rows)
    return _combine(dest, y_buf, prob, x1, g2, ln2_g, ln2_b, seq, alpha)


def kernel(x, c, w_ada, b_ada, w_in, mu_shift, w0, w_decay_up, a0, w_iclr_up, w_glora_up, k_k, k_a, r_k, lnx_g, lnx_b, w_rwkv_out, w_pool, pool_scale, w_out, ln1_g, ln1_b, w_router, b_router, w_gu, b_gu, w_down, b_down, ln2_g, ln2_b):
    bsz, seq, d = x.shape
    depth = w_ada.shape[0]
    alpha = float((2 * depth) ** 0.25)
    c8 = jnp.pad(c, ((0, 8 - bsz), (0, 0)))
    x2 = x.reshape(bsz * seq, d)
    weights = (w_ada, b_ada, w_in, mu_shift, w0, w_decay_up, a0, w_iclr_up, w_glora_up, k_k, k_a, r_k,
               lnx_g, lnx_b, w_rwkv_out, w_pool, pool_scale, w_out, ln1_g, ln1_b, w_router, b_router,
               w_gu, b_gu, w_down, b_down, ln2_g, ln2_b)
    for l in range(depth):
        x2 = _layer(x2, c8, bsz, seq, alpha, *[w[l] for w in weights])
    return x2.reshape(bsz, seq, d)
```

```python
import functools

import jax
import jax.numpy as jnp
from jax import lax
from jax.experimental import pallas as pl
from jax.experimental.pallas import tpu as pltpu

F32 = jnp.float32
BF16 = jnp.bfloat16

HEAD_SIZE = 64
LANES = 128
TOP_K = 4
LN_EPS = 1e-5
GN_EPS = 64e-5
POOL_WINDOWS = (2, 4, 8, 16)
SWIGLU_LIMIT = 7.0
SWIGLU_ALPHA = 1.702
WKV_CHUNK = 64
MOE_SUB = 256
MOE_RMAX = 1536
MOE_TF = 512
VMEM_LIMIT = 56 * 1024 * 1024

_NT = (((1,), (1,)), ((), ()))
_TN = (((0,), (0,)), ((), ()))


def _cparams(*sem):
    return pltpu.CompilerParams(dimension_semantics=sem, vmem_limit_bytes=VMEM_LIMIT)


def _sigmoid(z):
    return 1.0 / (1.0 + jnp.exp(-z))


def _dot(a, b):
    return jnp.dot(a, b, preferred_element_type=F32)


def _split_dot(xv, w):
    hi = xv.astype(BF16)
    lo = (xv - hi.astype(F32)).astype(BF16)
    return _dot(hi, w) + _dot(lo, w)


def _head_sum(xv, ones_bd):
    outs = []
    for p in range(xv.shape[1] // LANES):
        outs.append(_split_dot(xv[:, p * LANES:(p + 1) * LANES], ones_bd))
    return jnp.concatenate(outs, axis=1)


def _adaln_kernel(c_ref, w_ref, b_ref, o_ref):
    c = c_ref[...]
    cond = c * _sigmoid(c)
    o_ref[...] = _dot(cond.astype(BF16), w_ref[...].astype(BF16)) + b_ref[...]


def _adaln(c8, w, b):
    d, n = w.shape
    tn = 1024
    return pl.pallas_call(
        _adaln_kernel,
        grid=(n // tn,),
        in_specs=[pl.BlockSpec((8, d), lambda j: (0, 0)),
                  pl.BlockSpec((d, tn), lambda j: (0, j)),
                  pl.BlockSpec((1, tn), lambda j: (0, j))],
        out_specs=pl.BlockSpec((8, tn), lambda j: (0, j)),
        out_shape=jax.ShapeDtypeStruct((8, n), F32),
        compiler_params=_cparams("arbitrary"),
        name="adaln",
    )(c8, w, b.reshape(1, n))


def _inproj_kernel(x_ref, sc_ref, sh_ref, w_ref, o_ref, h_scr):
    @pl.when(pl.program_id(1) == 0)
    def _():
        h_scr[...] = (x_ref[...] * (1.0 + sc_ref[0]) + sh_ref[0]).astype(BF16)

    o_ref[...] = _dot(h_scr[...], w_ref[...])


def _inproj(x2, sc, sh, w, seq):
    m, d = x2.shape
    n = w.shape[1]
    tm, tn = 1024, 1024
    tpb = seq // tm
    return pl.pallas_call(
        _inproj_kernel,
        grid=(m // tm, n // tn),
        in_specs=[pl.BlockSpec((tm, d), lambda i, j: (i, 0)),
                  pl.BlockSpec((1, 1, d), lambda i, j: (i // tpb, 0, 0)),
                  pl.BlockSpec((1, 1, d), lambda i, j: (i // tpb, 0, 0)),
                  pl.BlockSpec((d, tn), lambda i, j: (0, j))],
        out_specs=pl.BlockSpec((tm, tn), lambda i, j: (i, j)),
        out_shape=jax.ShapeDtypeStruct((m, n), F32),
        scratch_shapes=[pltpu.VMEM((tm, d), BF16)],
        compiler_params=_cparams("arbitrary", "arbitrary"),
        name="inproj",
    )(x2, sc, sh, w)


def _prep_kernel(p_ref, prev_ref, mu_ref, w0_ref, a0_ref, kk_ref, ka_ref, wd_ref, wa_ref, wg_ref,
                 ones_ref, r_out, lw_out, k_out, v_out, kkn_out, b_out, g_out, *, tpb, c):
    cur = p_ref[...]
    first = (pl.program_id(0) % tpb) == 0
    prev_row = jnp.where(first, 0.0, prev_ref[7:8, :])
    shifted = pltpu.roll(cur, 1, axis=0)
    row0 = lax.broadcasted_iota(jnp.int32, cur.shape, 0) == 0
    shifted = jnp.where(row0, prev_row, shifted)
    ps = cur + (shifted - cur) * mu_ref[...]
    r = ps[:, 0:c]
    k = ps[:, c:2 * c]
    v = ps[:, 2 * c:3 * c]
    o = 3 * c
    wdp = ps[:, o:o + LANES]
    adp = ps[:, o + LANES:o + 2 * LANES]
    gdp = ps[:, o + 2 * LANES:o + 4 * LANES]
    z = w0_ref[...] + _dot(jnp.tanh(wdp).astype(BF16), wd_ref[...])
    lw = -jnp.exp(-0.5) * _sigmoid(z)
    iclr = _sigmoid(a0_ref[...] + _dot(adp.astype(BF16), wa_ref[...]))
    g = _dot(_sigmoid(gdp).astype(BF16), wg_ref[...])
    kk0 = k * kk_ref[...]
    ss = _head_sum(kk0 * kk0, ones_ref[...])
    kkn = kk0 / jnp.maximum(jnp.sqrt(ss), 1e-12)
    kmod = k * (1.0 + (iclr - 1.0) * ka_ref[...])
    r_out[...] = r
    lw_out[...] = lw
    k_out[...] = kmod
    v_out[...] = v
    kkn_out[...] = kkn
    b_out[...] = kkn * iclr
    g_out[...] = g


def _prep(p, mu, w0, a0, k_k, k_a, wd, wa, wg, ones_bd, seq, c, shift_blk):
    m = p.shape[0]
    ns = mu.shape[1]
    tm = 256
    tpb = seq // tm
    row = lambda a: a.reshape(1, -1)
    full = lambda a: pl.BlockSpec(a.shape, lambda i: (0,) * a.ndim)
    out = jax.ShapeDtypeStruct((m, c), F32)
    ospec = pl.BlockSpec((tm, c), lambda i: (i, 0))
    args = (p, p, mu, row(w0), row(a0), row(k_k), row(k_a), wd, wa, wg, ones_bd)
    in_specs = [pl.BlockSpec((tm, ns), lambda i: (i, shift_blk)),
                pl.BlockSpec((8, ns), lambda i: (jnp.maximum(i * (tm // 8) - 1, 0), shift_blk))]
    in_specs += [full(a) for a in args[2:]]
    return pl.pallas_call(
        functools.partial(_prep_kernel, tpb=tpb, c=c),
        grid=(m // tm,),
        in_specs=in_specs,
        out_specs=[ospec] * 7,
        out_shape=[out] * 7,
        compiler_params=_cparams("arbitrary"),
        name="rwkv_prep",
    )(*args)


def _wkv_kernel(r_ref, lw_ref, k_ref, v_ref, kk_ref, b_ref, y_ref, s_scr, *, chunk, c):
    L = chunk

    @pl.when(pl.program_id(1) == 0)
    def _():
        s_scr[...] = jnp.zeros_like(s_scr)

    ri = lax.broadcasted_iota(jnp.int32, (L, L), 0)
    ci = lax.broadcasted_iota(jnp.int32, (L, L), 1)
    tri = (ri >= ci).astype(BF16)
    lw = lw_ref[...]
    cs = _split_dot_lhs(tri, lw)
    cs_last = cs[L - 1:L, :]
    g_in = jnp.exp(cs)
    g_ex = jnp.exp(cs - lw)
    g_inv = jnp.exp(-cs)
    g_tail = jnp.exp(cs_last - cs)
    g_last = jnp.exp(cs_last)
    kk = kk_ref[...]
    bb = b_ref[...]
    kx = k_ref[...]
    vv = v_ref[...]
    a_t = -kk * g_ex
    r_t = r_ref[...] * g_in
    b_t = bb * g_inv
    k_t = kx * g_inv
    b_p = bb * g_tail
    k_p = kx * g_tail

    r2 = lax.broadcasted_iota(jnp.int32, (2 * L, 2 * L), 0)
    c2 = lax.broadcasted_iota(jnp.int32, (2 * L, 2 * L), 1)
    same = (r2 >= L) == (c2 >= L)
    m_strict = same & (r2 > c2)
    m_incl = same & (r2 >= c2)
    lane = lax.broadcasted_iota(jnp.int32, (L, LANES), 1)
    low = lane < HEAD_SIZE

    def stack(zv):
        return jnp.concatenate([jnp.where(low, zv, 0.0), jnp.where(low, 0.0, zv)], axis=0)

    pairs = range(c // LANES)
    sls = [slice(p * LANES, (p + 1) * LANES) for p in pairs]
    a_s = [stack(a_t[:, sl]) for sl in sls]
    r_s = [stack(r_t[:, sl]) for sl in sls]
    v_s = [stack(vv[:, sl]) for sl in sls]
    ar = [jnp.concatenate([a_s[p], r_s[p]], axis=0).astype(BF16) for p in pairs]
    qb = [lax.dot_general(ar[p], stack(b_t[:, sls[p]]).astype(BF16), _NT, preferred_element_type=F32)
          for p in pairs]
    qk = [lax.dot_general(ar[p], stack(k_t[:, sls[p]]).astype(BF16), _NT, preferred_element_type=F32)
          for p in pairs]
    a_ab = [jnp.where(m_strict, q[:2 * L], 0.0) for q in qb]
    a_rb = [jnp.where(m_incl, q[2 * L:], 0.0) for q in qb]
    a_ak = [jnp.where(m_strict, q[:2 * L], 0.0) for q in qk]
    a_rk = [jnp.where(m_incl, q[2 * L:], 0.0) for q in qk]
    v_b = [x.astype(BF16) for x in v_s]
    z = [jnp.concatenate([_dot(a_ak[p].astype(BF16), v_b[p]), a_s[p]], axis=1) for p in pairs]
    amat = a_ab
    n_sq = max(L.bit_length() - 1, 1)
    for it in range(n_sq):
        ab = [x.astype(BF16) for x in amat]
        z = [z[p] + _dot(ab[p], z[p].astype(BF16)) for p in pairs]
        if it + 1 < n_sq:
            amat = [_dot(x, x) for x in ab]
    s_old = [s_scr[p] for p in pairs]
    s_b = [x.astype(BF16) for x in s_old]
    u_s = [_dot(z[p][:, LANES:].astype(BF16), s_b[p]) + z[p][:, :LANES] for p in pairs]
    uv = [jnp.concatenate([u_s[p], v_s[p]], axis=0).astype(BF16) for p in pairs]
    y_s = [_dot(r_s[p].astype(BF16), s_b[p])
           + _dot(jnp.concatenate([a_rb[p], a_rk[p]], axis=1).astype(BF16), uv[p]) for p in pairs]
    for p in pairs:
        y_ref[:, sls[p]] = y_s[p][:L] + y_s[p][L:]
    upd = [lax.dot_general(jnp.concatenate([stack(b_p[:, sls[p]]), stack(k_p[:, sls[p]])], axis=0).astype(BF16),
                           uv[p], _TN, preferred_element_type=F32) for p in pairs]
    for p in pairs:
        g_col = jnp.transpose(jnp.broadcast_to(g_last[:, sls[p]], (LANES, LANES)))
        s_scr[p] = s_old[p] * g_col + upd[p]


def _split_dot_lhs(w, xv):
    hi = xv.astype(BF16)
    lo = (xv - hi.astype(F32)).astype(BF16)
    return _dot(w, hi) + _dot(w, lo)


def _wkv(r, lw, k, v, kk, b, bsz, seq, c):
    L = WKV_CHUNK
    nc = seq // L
    spec = pl.BlockSpec((L, c), lambda bi, ci: (bi * nc + ci, 0))
    return pl.pallas_call(
        functools.partial(_wkv_kernel, chunk=L, c=c),
        grid=(bsz, nc),
        in_specs=[spec] * 6,
        out_specs=spec,
        out_shape=jax.ShapeDtypeStruct((bsz * seq, c), F32),
        scratch_shapes=[pltpu.VMEM((c // LANES, LANES, LANES), F32)],
        compiler_params=_cparams("arbitrary", "arbitrary"),
        name="wkv7",
    )(r, lw, k, v, kk, b)


def _post_kernel(y_ref, r_ref, k_ref, v_ref, g_ref, gate_ref, pool_ref, halo_ref, x_ref,
                 g1_ref, sc2_ref, sh2_ref, rk_ref, lng_ref, lnb_ref, ones_ref, wro_ref, wp_ref,
                 psc_ref, wo_ref, l1g_ref, l1b_ref, wr_ref, br_ref,
                 x1_out, h2_out, lg_out, pool_scr, *, tpb, alpha, d, tm):
    ones_bd = ones_ref[...]
    inv_n = 1.0 / HEAD_SIZE
    y = y_ref[...]
    mu = _head_sum(y, ones_bd) * inv_n
    dy = y - mu
    var = _head_sum(dy * dy, ones_bd) * inv_n
    yn = dy * lax.rsqrt(var + GN_EPS) * lng_ref[...] + lnb_ref[...]
    vv = v_ref[...]
    bonus = _head_sum(r_ref[...] * k_ref[...] * rk_ref[...], ones_bd) * vv
    y_rwkv = ((yn + bonus) * g_ref[...]).astype(BF16)
    branch_a = _dot(y_rwkv, wro_ref[...])

    first = (pl.program_id(0) % tpb) == 0
    halo = 16
    pool_scr[0:halo, :] = jnp.where(first, 0.0, halo_ref[...])
    pool_scr[halo:, :] = pool_ref[...]
    pos = (pl.program_id(0) % tpb) * tm + lax.broadcasted_iota(jnp.int32, (tm, 1), 0)
    gw = pool_ref.shape[1] // len(POOL_WINDOWS)
    dg = d // len(POOL_WINDOWS)
    outs = []
    for gi, win in enumerate(POOL_WINDOWS):
        cols = slice(gi * gw, (gi + 1) * gw)
        u = pool_scr[halo:, cols]
        wsum = u
        for s in range(1, win):
            wsum = wsum + pool_scr[halo - s:halo - s + tm, cols]
        cnt = jnp.minimum(pos + 1, win).astype(F32)
        dd = (wsum / cnt - u).astype(BF16)
        outs.append(_dot(dd, wp_ref[gi]))
    branch_b = jnp.concatenate(outs, axis=1) * psc_ref[...]

    gates = gate_ref[...]
    mixin = _sigmoid(gates[:, :d]) * branch_a + _sigmoid(gates[:, d:]) * branch_b
    mix = _dot(mixin.astype(BF16), wo_ref[...])
    xr = alpha * x_ref[...] + g1_ref[0] * mix
    m1 = jnp.mean(xr, axis=-1, keepdims=True)
    xc = xr - m1
    v1 = jnp.mean(xc * xc, axis=-1, keepdims=True)
    x1 = xc * lax.rsqrt(v1 + LN_EPS) * l1g_ref[...] + l1b_ref[...]
    x1_out[...] = x1
    h2 = x1 * (1.0 + sc2_ref[0]) + sh2_ref[0]
    h2_out[...] = h2
    lg_out[...] = _dot(h2.astype(BF16), wr_ref[...]) + br_ref[...]


def _post(y, r, k, v, g, p, x2, g1, sc2, sh2, r_k, lnx_g, lnx_b, ones_bd, w_ro, w_pool, pool_scale,
          w_out, ln1_g, ln1_b, w_r, b_r, seq, alpha, gate_blk, pool_blk):
    m, d = x2.shape
    c = y.shape[1]
    pw = w_pool.shape[0] * w_pool.shape[1]
    tm = 256
    tpb = seq // tm
    row = lambda a: a.reshape(1, -1)
    const = lambda a: pl.BlockSpec(a.shape, lambda i: (0,) * a.ndim, pipeline_mode=pl.Buffered(1))
    tile = lambda w: pl.BlockSpec((tm, w), lambda i: (i, 0))
    bvec = pl.BlockSpec((1, 1, d), lambda i: (i // tpb, 0, 0))
    consts = (row(r_k), row(lnx_g), row(lnx_b), ones_bd, w_ro, w_pool, row(pool_scale), w_out,
              row(ln1_g), row(ln1_b), w_r, row(b_r))
    in_specs = [tile(c)] * 5 + [
        pl.BlockSpec((tm, 2 * d), lambda i: (i, gate_blk)),
        pl.BlockSpec((tm, pw), lambda i: (i, pool_blk)),
        pl.BlockSpec((16, pw), lambda i: (jnp.maximum(i * (tm // 16) - 1, 0), pool_blk)),
        tile(d), bvec, bvec, bvec] + [const(a) for a in consts]
    return pl.pallas_call(
        functools.partial(_post_kernel, tpb=tpb, alpha=alpha, d=d, tm=tm),
        grid=(m // tm,),
        in_specs=in_specs,
        out_specs=[tile(d), tile(d), tile(LANES)],
        out_shape=[jax.ShapeDtypeStruct((m, d), F32), jax.ShapeDtypeStruct((m, d), F32),
                   jax.ShapeDtypeStruct((m, LANES), F32)],
        scratch_shapes=[pltpu.VMEM((tm + 16, pw), F32)],
        compiler_params=_cparams("arbitrary"),
        name="merge_ln1",
    )(y, r, k, v, g, p, p, p, x2, g1, sc2, sh2, *consts)


def _route_kernel(lg_ref, idx_out, prob_out, rank_out, cnt_out, carry, *, n_exp, tm):
    @pl.when(pl.program_id(0) == 0)
    def _():
        carry[...] = jnp.zeros_like(carry)

    lane = lax.broadcasted_iota(jnp.int32, (tm, LANES), 1)
    lane_f = lane.astype(F32)
    neg = jnp.float32(-jnp.inf)
    work = jnp.where(lane < n_exp, lg_ref[...], neg)
    vals, sels = [], []
    idx_acc = jnp.zeros((tm, LANES), jnp.int32)
    hot = jnp.zeros((tm, LANES), F32)
    for kk in range(TOP_K):
        mx = jnp.max(work, axis=-1, keepdims=True)
        idx = jnp.min(jnp.where(work == mx, lane_f, float(LANES)), axis=-1, keepdims=True).astype(jnp.int32)
        sel = lane == idx
        work = jnp.where(sel, neg, work)
        vals.append(mx)
        sels.append(sel)
        idx_acc = jnp.where(lane == kk, idx, idx_acc)
        hot = hot + sel.astype(F32)
    es = [jnp.exp(vv - vals[0]) for vv in vals]
    den = es[0] + es[1] + es[2] + es[3]
    prob = jnp.zeros((tm, LANES), F32)
    for kk in range(TOP_K):
        prob = jnp.where(lane == kk, es[kk] / den, prob)
    ri = lax.broadcasted_iota(jnp.int32, (tm, tm), 0)
    ci = lax.broadcasted_iota(jnp.int32, (tm, tm), 1)
    before = _dot((ri > ci).astype(BF16), hot.astype(BF16)) + carry[0:1, :]
    rank = jnp.zeros((tm, LANES), jnp.int32)
    for kk in range(TOP_K):
        rk = jnp.sum(jnp.where(sels[kk], before, 0.0), axis=-1, keepdims=True).astype(jnp.int32)
        rank = jnp.where(lane == kk, rk, rank)
    carry[...] = carry[...] + jnp.sum(hot, axis=0, keepdims=True)
    idx_out[...] = idx_acc
    prob_out[...] = prob
    rank_out[...] = rank
    cnt_out[...] = carry[...]


def _route(logits, n_exp):
    m = logits.shape[0]
    tm = 512
    tile = pl.BlockSpec((tm, LANES), lambda i: (i, 0))
    return pl.pallas_call(
        functools.partial(_route_kernel, n_exp=n_exp, tm=tm),
        grid=(m // tm,),
        in_specs=[tile],
        out_specs=[tile, tile, tile, pl.BlockSpec((8, LANES), lambda i: (0, 0))],
        out_shape=[jax.ShapeDtypeStruct((m, LANES), jnp.int32), jax.ShapeDtypeStruct((m, LANES), F32),
                   jax.ShapeDtypeStruct((m, LANES), jnp.int32), jax.ShapeDtypeStruct((8, LANES), F32)],
        scratch_shapes=[pltpu.VMEM((8, LANES), F32)],
        compiler_params=_cparams("arbitrary"),
        name="route",
    )(logits)


def _row_copy(src_hbm, dst, src_row, dst_row, sem):
    return pltpu.make_async_copy(src_hbm.at[pl.ds(src_row, 1), :], dst.at[pl.ds(dst_row, 1), :], sem)


def _dispatch_kernel(src_ref, h_hbm, o_ref, buf, sems, *, rows):
    i = pl.program_id(0)
    slot = i % 2

    def issue_block(blk, slot_):
        def body(t, carry):
            _row_copy(h_hbm, buf.at[slot_], src_ref[blk * rows + t], t, sems.at[slot_]).start()
            return carry

        lax.fori_loop(0, rows, body, 0, unroll=8)

    @pl.when(i == 0)
    def _():
        issue_block(0, 0)

    @pl.when(i + 1 < pl.num_programs(0))
    def _():
        issue_block(i + 1, 1 - slot)

    def drain(t, carry):
        _row_copy(h_hbm, buf.at[slot], 0, t, sems.at[slot]).wait()
        return carry

    lax.fori_loop(0, rows, drain, 0, unroll=8)
    o_ref[...] = buf[slot].astype(BF16)


def _dispatch(src_tok, h2, n_rows):
    d = h2.shape[1]
    rows = MOE_SUB
    return pl.pallas_call(
        functools.partial(_dispatch_kernel, rows=rows),
        grid_spec=pltpu.PrefetchScalarGridSpec(
            num_scalar_prefetch=1,
            grid=(n_rows // rows,),
            in_specs=[pl.BlockSpec(memory_space=pl.ANY)],
            out_specs=pl.BlockSpec((rows, d), lambda i, src: (i, 0)),
            scratch_shapes=[pltpu.VMEM((2, rows, d), F32), pltpu.SemaphoreType.DMA((2,))]),
        out_shape=jax.ShapeDtypeStruct((n_rows, d), BF16),
        compiler_params=_cparams("arbitrary"),
        name="dispatch",
    )(src_tok, h2)


def _roll_lanes(v, shift):
    return jnp.concatenate([pltpu.roll(v[:, g * LANES:(g + 1) * LANES], shift, axis=1)
                            for g in range(v.shape[1] // LANES)], axis=1)


def _expert_kernel(e_ref, st_ref, ns_ref, tot_ref, x_ref, wgu_ref, bgu_ref, wd_ref, bd_ref,
                   y_hbm, acc, wd_i, wd_b, act, sem, *, n_ff, rmax, sub, n_rows):
    i = pl.program_id(0)
    j = pl.program_id(1)
    nsub = ns_ref[i]
    n_sub_max = rmax // sub
    tfc = wd_ref.shape[1]
    half = tfc // 2

    def out_copy(row0, s):
        return pltpu.make_async_copy(acc.at[pl.ds(s * sub, sub), :],
                                     y_hbm.at[pl.ds(pl.multiple_of(row0, sub), sub), :], sem)

    def for_valid_subs(item, fn):
        for s in range(n_sub_max):
            @pl.when(s < ns_ref[item])
            def _():
                fn(out_copy(st_ref[item] + s * sub, s))

    @pl.when(j == 0)
    def _():
        @pl.when(i > 0)
        def _():
            for_valid_subs(i - 1, lambda cp: cp.wait())

        acc[...] = jnp.broadcast_to(bd_ref[0], acc.shape)

    @pl.when(nsub > 0)
    def _():
        for g in range(wd_b.shape[1] // LANES):
            cols = slice(g * LANES, (g + 1) * LANES)
            wd_i[g % 2, pl.ds(0, half, stride=2), :] = wd_ref[0, 0:half, cols]
            wd_i[g % 2, pl.ds(1, half, stride=2), :] = wd_ref[0, half:tfc, cols]
            wd_b[:, cols] = wd_i[g % 2].astype(BF16)

    cw = 2 * LANES

    def paired(gu):
        gate = jnp.minimum(gu, SWIGLU_LIMIT)
        glu = gate * _sigmoid(gate * SWIGLU_ALPHA)
        up1 = jnp.clip(gu, -SWIGLU_LIMIT, SWIGLU_LIMIT) + 1.0
        return glu * _roll_lanes(up1, LANES - 1)

    def run(m_rows):
        xm = x_ref[0:m_rows, :]
        even = (lax.broadcasted_iota(jnp.int32, (m_rows, cw), 1) % 2) == 0
        for c in range(tfc // cw):
            lo = slice(c * cw, (c + 1) * cw)
            hi = slice(tfc + c * cw, tfc + (c + 1) * cw)
            pa = paired(_dot(xm, wgu_ref[0, :, lo].astype(BF16)) + bgu_ref[0, :, lo])
            pb = paired(_dot(xm, wgu_ref[0, :, hi].astype(BF16)) + bgu_ref[0, :, hi])
            act[0:m_rows, lo] = jnp.where(even, pa, _roll_lanes(pb, 1)).astype(BF16)
        acc[0:m_rows, :] += _dot(act[0:m_rows, :], wd_b[...])

    for n in range(1, n_sub_max + 1):
        @pl.when(nsub == n)
        def _():
            run(n * sub)

    @pl.when(j == n_ff - 1)
    def _():
        for_valid_subs(i, lambda cp: cp.start())

        @pl.when(i == pl.num_programs(0) - 1)
        def _():
            for_valid_subs(i, lambda cp: cp.wait())
            acc[0:sub, :] = jnp.zeros((sub, acc.shape[1]), F32)
            first = tot_ref[0] // sub

            def fill_start(blk, carry):
                out_copy(blk * sub, 0).start()
                return carry

            def fill_wait(blk, carry):
                out_copy(blk * sub, 0).wait()
                return carry

            lax.fori_loop(first, n_rows // sub, fill_start, 0)
            lax.fori_loop(first, n_rows // sub, fill_wait, 0)


def _experts(item_e, item_start, item_nsub, total_rows, x_buf, w_gu, b_gu, w_down, b_down, n_rows):
    n_items = item_e.shape[0]
    n_exp, dff, d = w_down.shape
    tf = MOE_TF
    n_ff = dff // tf
    rmax = MOE_RMAX

    def jj(i, j, ns):
        return jnp.where(ns[i] > 0, j, n_ff - 1)

    in_specs = [
        pl.BlockSpec((pl.Element(rmax), pl.Element(d)),
                     lambda i, j, e, st, ns, tot: (pl.multiple_of(st[i], MOE_SUB), 0),
                     pipeline_mode=pl.Buffered(1)),
        pl.BlockSpec((1, d, 2 * tf), lambda i, j, e, st, ns, tot: (e[i], 0, jj(i, j, ns))),
        pl.BlockSpec((1, 1, 2 * tf), lambda i, j, e, st, ns, tot: (e[i], 0, jj(i, j, ns))),
        pl.BlockSpec((1, tf, d), lambda i, j, e, st, ns, tot: (e[i], jj(i, j, ns), 0)),
        pl.BlockSpec((1, 1, d), lambda i, j, e, st, ns, tot: (e[i], 0, 0)),
    ]
    return pl.pallas_call(
        functools.partial(_expert_kernel, n_ff=n_ff, rmax=rmax, sub=MOE_SUB, n_rows=n_rows),
        grid_spec=pltpu.PrefetchScalarGridSpec(
            num_scalar_prefetch=4,
            grid=(n_items, n_ff),
            in_specs=in_specs,
            out_specs=pl.BlockSpec(memory_space=pl.ANY),
            scratch_shapes=[pltpu.VMEM((rmax, d), F32), pltpu.VMEM((2, tf, LANES), F32), pltpu.VMEM((tf, d), BF16),
                            pltpu.VMEM((rmax, tf), BF16), pltpu.SemaphoreType.DMA(())]),
        out_shape=jax.ShapeDtypeStruct((n_rows, d), F32),
        compiler_params=_cparams("arbitrary", "arbitrary"),
        name="experts",
    )(item_e, item_start, item_nsub, total_rows, x_buf, w_gu, b_gu.reshape(n_exp, 1, 2 * dff),
      w_down, b_down.reshape(n_exp, 1, d))


def _combine_kernel(dest_ref, y_hbm, prob_ref, x1_ref, g2_ref, lg_ref, lb_ref, o_ref, buf, sems,
                    *, tm, alpha):
    i = pl.program_id(0)
    slot = i % 2

    def issue_block(blk, slot_):
        def body(t, carry):
            for kk in range(TOP_K):
                _row_copy(y_hbm, buf.at[slot_, kk], dest_ref[(blk * tm + t) * TOP_K + kk], t,
                          sems.at[slot_]).start()
            return carry

        lax.fori_loop(0, tm, body, 0, unroll=2)

    @pl.when(i == 0)
    def _():
        issue_block(0, 0)

    @pl.when(i + 1 < pl.num_programs(0))
    def _():
        issue_block(i + 1, 1 - slot)

    def drain(t, carry):
        for kk in range(TOP_K):
            _row_copy(y_hbm, buf.at[slot, kk], 0, t, sems.at[slot]).wait()
        return carry

    lax.fori_loop(0, tm, drain, 0, unroll=2)
    prob = prob_ref[...]
    f = prob[:, 0:1] * buf[slot, 0]
    for kk in range(1, TOP_K):
        f = f + prob[:, kk:kk + 1] * buf[slot, kk]
    xr = alpha * x1_ref[...] + g2_ref[0] * f
    m2 = jnp.mean(xr, axis=-1, keepdims=True)
    xc = xr - m2
    v2 = jnp.mean(xc * xc, axis=-1, keepdims=True)
    o_ref[...] = xc * lax.rsqrt(v2 + LN_EPS) * lg_ref[...] + lb_ref[...]


def _combine(dest, y_buf, prob, x1, g2, ln_g, ln_b, seq, alpha):
    m, d = x1.shape
    tm = 128
    tpb = seq // tm
    return pl.pallas_call(
        functools.partial(_combine_kernel, tm=tm, alpha=alpha),
        grid_spec=pltpu.PrefetchScalarGridSpec(
            num_scalar_prefetch=1,
            grid=(m // tm,),
            in_specs=[pl.BlockSpec(memory_space=pl.ANY),
                      pl.BlockSpec((tm, LANES), lambda i, dst: (i, 0)),
                      pl.BlockSpec((tm, d), lambda i, dst: (i, 0)),
                      pl.BlockSpec((1, 1, d), lambda i, dst: (i // tpb, 0, 0)),
                      pl.BlockSpec((1, d), lambda i, dst: (0, 0)),
                      pl.BlockSpec((1, d), lambda i, dst: (0, 0))],
            out_specs=pl.BlockSpec((tm, d), lambda i, dst: (i, 0)),
            scratch_shapes=[pltpu.VMEM((2, TOP_K, tm, d), F32), pltpu.SemaphoreType.DMA((2,))]),
        out_shape=jax.ShapeDtypeStruct((m, d), F32),
        compiler_params=_cparams("arbitrary"),
        name="combine_ln2",
    )(dest, y_buf, prob, x1, g2, ln_g.reshape(1, d), ln_b.reshape(1, d))


def _pad_rows(w, rows):
    return jnp.pad(w, ((0, rows - w.shape[0]), (0, 0)))


def _pad_cols(w, cols):
    return jnp.pad(w, ((0, 0), (0, cols - w.shape[1])))


def _layer(x2, c8, bsz, seq, alpha, w_ada, b_ada, w_in, mu_shift, w0, w_decay_up, a0, w_iclr_up,
           w_glora_up, k_k, k_a, r_k, lnx_g, lnx_b, w_rwkv_out, w_pool, pool_scale, w_out, ln1_g,
           ln1_b, w_router, b_router, w_gu, b_gu, w_down, b_down, ln2_g, ln2_b):
    m, d = x2.shape
    c = w0.shape[0]
    n_dec, n_icl, n_gate = w_decay_up.shape[0], w_iclr_up.shape[0], w_glora_up.shape[0]
    pw = w_pool.shape[0] * w_pool.shape[1]
    n_exp = w_router.shape[1]
    assert n_dec <= LANES and n_icl <= LANES and n_gate <= 2 * LANES and d == 2 * c and pw == c

    mod = _adaln(c8, w_ada, b_ada)[:bsz]
    sh1, sc1, g1, sh2, sc2, g2 = [t.reshape(bsz, 1, d) for t in jnp.split(mod, 6, axis=-1)]

    o = 3 * c
    segs = [(o, n_dec, LANES), (o + n_dec, n_icl, LANES), (o + n_dec + n_icl, n_gate, 2 * LANES)]
    n_shift = o + n_dec + n_icl + n_gate
    zpad = 2 * d - 3 * c - 4 * LANES
    w_shift = jnp.concatenate([w_in[:, :o]] + [_pad_cols(w_in[:, s:s + n], wd) for s, n, wd in segs]
                              + [jnp.zeros((d, zpad), F32)], axis=1)
    mu = jnp.concatenate([mu_shift[:o]] + [jnp.pad(mu_shift[s:s + n], (0, wd - n)) for s, n, wd in segs]
                         + [jnp.zeros((zpad,), F32)]).reshape(1, -1)
    w_cat = jnp.concatenate([w_in[:, n_shift + pw:], w_shift, w_in[:, n_shift:n_shift + pw]],
                            axis=1).astype(BF16)
    p = _inproj(x2, sc1, sh1, w_cat, seq)

    ones_bd = jnp.kron(jnp.eye(LANES // HEAD_SIZE, dtype=F32), jnp.ones((HEAD_SIZE, HEAD_SIZE), F32)).astype(BF16)
    r, lw, kmod, v, kkn, bvec, g = _prep(
        p, mu, w0, a0, k_k, k_a, _pad_rows(w_decay_up, LANES).astype(BF16),
        _pad_rows(w_iclr_up, LANES).astype(BF16), _pad_rows(w_glora_up, 2 * LANES).astype(BF16),
        ones_bd, seq, c, shift_blk=1)
    y = _wkv(r, lw, kmod, v, kkn, bvec, bsz, seq, c)

    w_r = _pad_cols(w_router, LANES).astype(BF16)
    b_r = jnp.pad(b_router, (0, LANES - n_exp))
    x1, h2, logits = _post(y, r, kmod, v, g, p, x2, g1, sc2, sh2, r_k.reshape(-1), lnx_g, lnx_b, ones_bd,
                           w_rwkv_out.astype(BF16), w_pool.astype(BF16), pool_scale, w_out.astype(BF16),
                           ln1_g, ln1_b, w_r, b_r, seq, alpha, gate_blk=0, pool_blk=(4 * d) // pw)

    idx, prob, rank, cnt = _route(logits, n_exp)
    counts = cnt[0, :n_exp].astype(jnp.int32)
    padded = (counts + MOE_SUB - 1) // MOE_SUB * MOE_SUB
    pad_end = jnp.cumsum(padded)
    pad_start = pad_end - padded
    mk = m * TOP_K
    n_rows = (mk // MOE_SUB + n_exp) * MOE_SUB
    dest = (pad_start[idx[:, :TOP_K]] + rank[:, :TOP_K]).reshape(mk).astype(jnp.int32)
    src_tok = (jnp.arange(n_rows + MOE_RMAX, dtype=jnp.int32) % m).at[dest].set(
        jnp.arange(mk, dtype=jnp.int32) // TOP_K)
    n_items = mk // MOE_RMAX + n_exp
    per_e = (padded + MOE_RMAX - 1) // MOE_RMAX
    it_end = jnp.cumsum(per_e)
    it_start = it_end - per_e
    ids = jnp.arange(n_items, dtype=jnp.int32)
    item_e = jnp.minimum(jnp.searchsorted(it_end, ids, side='right'), n_exp - 1).astype(jnp.int32)
    local = ids - it_start[item_e]
    used = ids < it_end[-1]
    item_start = jnp.where(used, pad_start[item_e] + local * MOE_RMAX, n_rows).astype(jnp.int32)
    item_nsub = jnp.where(used, jnp.clip((padded[item_e] - local * MOE_RMAX) // MOE_SUB, 0,
                                         MOE_RMAX // MOE_SUB), 0).astype(jnp.int32)

    x_buf = _dispatch(src_tok, h2, n_rows + MOE_RMAX)
    y_buf = _experts(item_e, item_start, item_nsub, pad_end[-1:].astype(jnp.int32), x_buf,
                     w_gu, b_gu, w_down, b_down, n_rows)
    return _combine(dest, y_buf, prob, x1, g2, ln2_g, ln2_b, seq, alpha)


def kernel(x, c, w_ada, b_ada, w_in, mu_shift, w0, w_decay_up, a0, w_iclr_up, w_glora_up, k_k, k_a, r_k, lnx_g, lnx_b, w_rwkv_out, w_pool, pool_scale, w_out, ln1_g, ln1_b, w_router, b_router, w_gu, b_gu, w_down, b_down, ln2_g, ln2_b):
    bsz, seq, d = x.shape
    depth = w_ada.shape[0]
    alpha = float((2 * depth) ** 0.25)
    c8 = jnp.pad(c, ((0, 8 - bsz), (0, 0)))
    x2 = x.reshape(bsz * seq, d)
    weights = (w_ada, b_ada, w_in, mu_shift, w0, w_decay_up, a0, w_iclr_up, w_glora_up, k_k, k_a, r_k,
               lnx_g, lnx_b, w_rwkv_out, w_pool, pool_scale, w_out, ln1_g, ln1_b, w_router, b_router,
               w_gu, b_gu, w_down, b_down, ln2_g, ln2_b)
    for l in range(depth):
        x2 = _layer(x2, c8, bsz, seq, alpha, *[w[l] for w in weights])
    return x2.reshape(bsz, seq, d)
```

```python
import functools

import jax
import jax.numpy as jnp
from jax import lax
from jax.experimental import pallas as pl
from jax.experimental.pallas import tpu as pltpu

F32 = jnp.float32
BF16 = jnp.bfloat16

HEAD_SIZE = 64
LANES = 128
TOP_K = 4
LN_EPS = 1e-5
GN_EPS = 64e-5
POOL_WINDOWS = (2, 4, 8, 16)
SWIGLU_LIMIT = 7.0
SWIGLU_ALPHA = 1.702
WKV_CHUNK = 64
MOE_SUB = 256
MOE_RMAX = 1536
MOE_TF = 512
VMEM_LIMIT = 56 * 1024 * 1024

_NT = (((1,), (1,)), ((), ()))
_TN = (((0,), (0,)), ((), ()))


def _cparams(*sem):
    return pltpu.CompilerParams(dimension_semantics=sem, vmem_limit_bytes=VMEM_LIMIT)


def _sigmoid(z):
    return 1.0 / (1.0 + jnp.exp(-z))


def _dot(a, b):
    return jnp.dot(a, b, preferred_element_type=F32)


def _split_dot(xv, w):
    hi = xv.astype(BF16)
    lo = (xv - hi.astype(F32)).astype(BF16)
    return _dot(hi, w) + _dot(lo, w)


def _head_sum(xv, ones_bd):
    outs = []
    for p in range(xv.shape[1] // LANES):
        outs.append(_split_dot(xv[:, p * LANES:(p + 1) * LANES], ones_bd))
    return jnp.concatenate(outs, axis=1)


def _adaln_kernel(c_ref, w_ref, b_ref, o_ref):
    c = c_ref[...]
    cond = c * _sigmoid(c)
    o_ref[...] = _dot(cond.astype(BF16), w_ref[...].astype(BF16)) + b_ref[...]


def _adaln(c8, w, b):
    d, n = w.shape
    tn = 1024
    return pl.pallas_call(
        _adaln_kernel,
        grid=(n // tn,),
        in_specs=[pl.BlockSpec((8, d), lambda j: (0, 0)),
                  pl.BlockSpec((d, tn), lambda j: (0, j)),
                  pl.BlockSpec((1, tn), lambda j: (0, j))],
        out_specs=pl.BlockSpec((8, tn), lambda j: (0, j)),
        out_shape=jax.ShapeDtypeStruct((8, n), F32),
        compiler_params=_cparams("arbitrary"),
        name="adaln",
    )(c8, w, b.reshape(1, n))


def _inproj_kernel(x_ref, sc_ref, sh_ref, w_ref, o_ref, h_scr):
    @pl.when(pl.program_id(1) == 0)
    def _():
        h_scr[...] = (x_ref[...] * (1.0 + sc_ref[0]) + sh_ref[0]).astype(BF16)

    o_ref[...] = _dot(h_scr[...], w_ref[...])


def _inproj(x2, sc, sh, w, seq):
    m, d = x2.shape
    n = w.shape[1]
    tm, tn = 1024, 1024
    tpb = seq // tm
    return pl.pallas_call(
        _inproj_kernel,
        grid=(m // tm, n // tn),
        in_specs=[pl.BlockSpec((tm, d), lambda i, j: (i, 0)),
                  pl.BlockSpec((1, 1, d), lambda i, j: (i // tpb, 0, 0)),
                  pl.BlockSpec((1, 1, d), lambda i, j: (i // tpb, 0, 0)),
                  pl.BlockSpec((d, tn), lambda i, j: (0, j))],
        out_specs=pl.BlockSpec((tm, tn), lambda i, j: (i, j)),
        out_shape=jax.ShapeDtypeStruct((m, n), F32),
        scratch_shapes=[pltpu.VMEM((tm, d), BF16)],
        compiler_params=_cparams("arbitrary", "arbitrary"),
        name="inproj",
    )(x2, sc, sh, w)


def _prep_kernel(p_ref, prev_ref, mu_ref, w0_ref, a0_ref, kk_ref, ka_ref, wd_ref, wa_ref, wg_ref,
                 ones_ref, r_out, lw_out, k_out, v_out, kkn_out, b_out, g_out, *, tpb, c):
    cur = p_ref[...]
    first = (pl.program_id(0) % tpb) == 0
    prev_row = jnp.where(first, 0.0, prev_ref[7:8, :])
    shifted = pltpu.roll(cur, 1, axis=0)
    row0 = lax.broadcasted_iota(jnp.int32, cur.shape, 0) == 0
    shifted = jnp.where(row0, prev_row, shifted)
    ps = cur + (shifted - cur) * mu_ref[...]
    r = ps[:, 0:c]
    k = ps[:, c:2 * c]
    v = ps[:, 2 * c:3 * c]
    o = 3 * c
    wdp = ps[:, o:o + LANES]
    adp = ps[:, o + LANES:o + 2 * LANES]
    gdp = ps[:, o + 2 * LANES:o + 4 * LANES]
    z = w0_ref[...] + _dot(jnp.tanh(wdp).astype(BF16), wd_ref[...])
    lw = -jnp.exp(-0.5) * _sigmoid(z)
    iclr = _sigmoid(a0_ref[...] + _dot(adp.astype(BF16), wa_ref[...]))
    g = _dot(_sigmoid(gdp).astype(BF16), wg_ref[...])
    kk0 = k * kk_ref[...]
    ss = _head_sum(kk0 * kk0, ones_ref[...])
    kkn = kk0 / jnp.maximum(jnp.sqrt(ss), 1e-12)
    kmod = k * (1.0 + (iclr - 1.0) * ka_ref[...])
    r_out[...] = r
    lw_out[...] = lw
    k_out[...] = kmod
    v_out[...] = v
    kkn_out[...] = kkn
    b_out[...] = kkn * iclr
    g_out[...] = g


def _prep(p, mu, w0, a0, k_k, k_a, wd, wa, wg, ones_bd, seq, c, shift_blk):
    m = p.shape[0]
    ns = mu.shape[1]
    tm = 256
    tpb = seq // tm
    row = lambda a: a.reshape(1, -1)
    full = lambda a: pl.BlockSpec(a.shape, lambda i: (0,) * a.ndim)
    out = jax.ShapeDtypeStruct((m, c), F32)
    ospec = pl.BlockSpec((tm, c), lambda i: (i, 0))
    args = (p, p, mu, row(w0), row(a0), row(k_k), row(k_a), wd, wa, wg, ones_bd)
    in_specs = [pl.BlockSpec((tm, ns), lambda i: (i, shift_blk)),
                pl.BlockSpec((8, ns), lambda i: (jnp.maximum(i * (tm // 8) - 1, 0), shift_blk))]
    in_specs += [full(a) for a in args[2:]]
    return pl.pallas_call(
        functools.partial(_prep_kernel, tpb=tpb, c=c),
        grid=(m // tm,),
        in_specs=in_specs,
        out_specs=[ospec] * 7,
        out_shape=[out] * 7,
        compiler_params=_cparams("arbitrary"),
        name="rwkv_prep",
    )(*args)


def _wkv_kernel(r_ref, lw_ref, k_ref, v_ref, kk_ref, b_ref, y_ref, s_scr, *, chunk, c):
    L = chunk

    @pl.when(pl.program_id(1) == 0)
    def _():
        s_scr[...] = jnp.zeros_like(s_scr)

    ri = lax.broadcasted_iota(jnp.int32, (L, L), 0)
    ci = lax.broadcasted_iota(jnp.int32, (L, L), 1)
    tri = (ri >= ci).astype(BF16)
    lw = lw_ref[...]
    cs = _split_dot_lhs(tri, lw)
    cs_last = cs[L - 1:L, :]
    g_in = jnp.exp(cs)
    g_ex = jnp.exp(cs - lw)
    g_inv = jnp.exp(-cs)
    g_tail = jnp.exp(cs_last - cs)
    g_last = jnp.exp(cs_last)
    kk = kk_ref[...]
    bb = b_ref[...]
    kx = k_ref[...]
    vv = v_ref[...]
    a_t = -kk * g_ex
    r_t = r_ref[...] * g_in
    b_t = bb * g_inv
    k_t = kx * g_inv
    b_p = bb * g_tail
    k_p = kx * g_tail

    r2 = lax.broadcasted_iota(jnp.int32, (2 * L, 2 * L), 0)
    c2 = lax.broadcasted_iota(jnp.int32, (2 * L, 2 * L), 1)
    same = (r2 >= L) == (c2 >= L)
    m_strict = same & (r2 > c2)
    m_incl = same & (r2 >= c2)
    lane = lax.broadcasted_iota(jnp.int32, (L, LANES), 1)
    low = lane < HEAD_SIZE

    def stack(zv):
        return jnp.concatenate([jnp.where(low, zv, 0.0), jnp.where(low, 0.0, zv)], axis=0)

    pairs = range(c // LANES)
    sls = [slice(p * LANES, (p + 1) * LANES) for p in pairs]
    a_s = [stack(a_t[:, sl]) for sl in sls]
    r_s = [stack(r_t[:, sl]) for sl in sls]
    v_s = [stack(vv[:, sl]) for sl in sls]
    ar = [jnp.concatenate([a_s[p], r_s[p]], axis=0).astype(BF16) for p in pairs]
    qb = [lax.dot_general(ar[p], stack(b_t[:, sls[p]]).astype(BF16), _NT, preferred_element_type=F32)
          for p in pairs]
    qk = [lax.dot_general(ar[p], stack(k_t[:, sls[p]]).astype(BF16), _NT, preferred_element_type=F32)
          for p in pairs]
    a_ab = [jnp.where(m_strict, q[:2 * L], 0.0) for q in qb]
    a_rb = [jnp.where(m_incl, q[2 * L:], 0.0) for q in qb]
    a_ak = [jnp.where(m_strict, q[:2 * L], 0.0) for q in qk]
    a_rk = [jnp.where(m_incl, q[2 * L:], 0.0) for q in qk]
    v_b = [x.astype(BF16) for x in v_s]
    z = [jnp.concatenate([_dot(a_ak[p].astype(BF16), v_b[p]), a_s[p]], axis=1) for p in pairs]
    amat = a_ab
    n_sq = max(L.bit_length() - 1, 1)
    for it in range(n_sq):
        ab = [x.astype(BF16) for x in amat]
        z = [z[p] + _dot(ab[p], z[p].astype(BF16)) for p in pairs]
        if it + 1 < n_sq:
            amat = [_dot(x, x) for x in ab]
    s_old = [s_scr[p] for p in pairs]
    s_b = [x.astype(BF16) for x in s_old]
    u_s = [_dot(z[p][:, LANES:].astype(BF16), s_b[p]) + z[p][:, :LANES] for p in pairs]
    uv = [jnp.concatenate([u_s[p], v_s[p]], axis=0).astype(BF16) for p in pairs]
    y_s = [_dot(r_s[p].astype(BF16), s_b[p])
           + _dot(jnp.concatenate([a_rb[p], a_rk[p]], axis=1).astype(BF16), uv[p]) for p in pairs]
    for p in pairs:
        y_ref[:, sls[p]] = y_s[p][:L] + y_s[p][L:]
    upd = [lax.dot_general(jnp.concatenate([stack(b_p[:, sls[p]]), stack(k_p[:, sls[p]])], axis=0).astype(BF16),
                           uv[p], _TN, preferred_element_type=F32) for p in pairs]
    for p in pairs:
        g_col = jnp.transpose(jnp.broadcast_to(g_last[:, sls[p]], (LANES, LANES)))
        s_scr[p] = s_old[p] * g_col + upd[p]


def _split_dot_lhs(w, xv):
    hi = xv.astype(BF16)
    lo = (xv - hi.astype(F32)).astype(BF16)
    return _dot(w, hi) + _dot(w, lo)


def _wkv(r, lw, k, v, kk, b, bsz, seq, c):
    L = WKV_CHUNK
    nc = seq // L
    spec = pl.BlockSpec((L, c), lambda bi, ci: (bi * nc + ci, 0))
    return pl.pallas_call(
        functools.partial(_wkv_kernel, chunk=L, c=c),
        grid=(bsz, nc),
        in_specs=[spec] * 6,
        out_specs=spec,
        out_shape=jax.ShapeDtypeStruct((bsz * seq, c), F32),
        scratch_shapes=[pltpu.VMEM((c // LANES, LANES, LANES), F32)],
        compiler_params=_cparams("arbitrary", "arbitrary"),
        name="wkv7",
    )(r, lw, k, v, kk, b)


def _post_kernel(y_ref, r_ref, k_ref, v_ref, g_ref, gate_ref, pool_ref, halo_ref, x_ref,
                 g1_ref, sc2_ref, sh2_ref, rk_ref, lng_ref, lnb_ref, ones_ref, wro_ref, wp_ref,
                 psc_ref, wo_ref, l1g_ref, l1b_ref, wr_ref, br_ref,
                 x1_out, h2_out, lg_out, pool_scr, *, tpb, alpha, d, tm):
    ones_bd = ones_ref[...]
    inv_n = 1.0 / HEAD_SIZE
    y = y_ref[...]
    mu = _head_sum(y, ones_bd) * inv_n
    dy = y - mu
    var = _head_sum(dy * dy, ones_bd) * inv_n
    yn = dy * lax.rsqrt(var + GN_EPS) * lng_ref[...] + lnb_ref[...]
    vv = v_ref[...]
    bonus = _head_sum(r_ref[...] * k_ref[...] * rk_ref[...], ones_bd) * vv
    y_rwkv = ((yn + bonus) * g_ref[...]).astype(BF16)
    branch_a = _dot(y_rwkv, wro_ref[...])

    first = (pl.program_id(0) % tpb) == 0
    halo = 16
    pool_scr[0:halo, :] = jnp.where(first, 0.0, halo_ref[...])
    pool_scr[halo:, :] = pool_ref[...]
    pos = (pl.program_id(0) % tpb) * tm + lax.broadcasted_iota(jnp.int32, (tm, 1), 0)
    gw = pool_ref.shape[1] // len(POOL_WINDOWS)
    dg = d // len(POOL_WINDOWS)
    outs = []
    for gi, win in enumerate(POOL_WINDOWS):
        cols = slice(gi * gw, (gi + 1) * gw)
        u = pool_scr[halo:, cols]
        wsum = u
        for s in range(1, win):
            wsum = wsum + pool_scr[halo - s:halo - s + tm, cols]
        cnt = jnp.minimum(pos + 1, win).astype(F32)
        dd = (wsum / cnt - u).astype(BF16)
        outs.append(_dot(dd, wp_ref[gi]))
    branch_b = jnp.concatenate(outs, axis=1) * psc_ref[...]

    gates = gate_ref[...]
    mixin = _sigmoid(gates[:, :d]) * branch_a + _sigmoid(gates[:, d:]) * branch_b
    mix = _dot(mixin.astype(BF16), wo_ref[...])
    xr = alpha * x_ref[...] + g1_ref[0] * mix
    m1 = jnp.mean(xr, axis=-1, keepdims=True)
    xc = xr - m1
    v1 = jnp.mean(xc * xc, axis=-1, keepdims=True)
    x1 = xc * lax.rsqrt(v1 + LN_EPS) * l1g_ref[...] + l1b_ref[...]
    x1_out[...] = x1
    h2 = x1 * (1.0 + sc2_ref[0]) + sh2_ref[0]
    h2_out[...] = h2
    lg_out[...] = _dot(h2.astype(BF16), wr_ref[...]) + br_ref[...]


def _post(y, r, k, v, g, p, x2, g1, sc2, sh2, r_k, lnx_g, lnx_b, ones_bd, w_ro, w_pool, pool_scale,
          w_out, ln1_g, ln1_b, w_r, b_r, seq, alpha, gate_blk, pool_blk):
    m, d = x2.shape
    c = y.shape[1]
    pw = w_pool.shape[0] * w_pool.shape[1]
    tm = 256
    tpb = seq // tm
    row = lambda a: a.reshape(1, -1)
    const = lambda a: pl.BlockSpec(a.shape, lambda i: (0,) * a.ndim, pipeline_mode=pl.Buffered(1))
    tile = lambda w: pl.BlockSpec((tm, w), lambda i: (i, 0))
    bvec = pl.BlockSpec((1, 1, d), lambda i: (i // tpb, 0, 0))
    consts = (row(r_k), row(lnx_g), row(lnx_b), ones_bd, w_ro, w_pool, row(pool_scale), w_out,
              row(ln1_g), row(ln1_b), w_r, row(b_r))
    in_specs = [tile(c)] * 5 + [
        pl.BlockSpec((tm, 2 * d), lambda i: (i, gate_blk)),
        pl.BlockSpec((tm, pw), lambda i: (i, pool_blk)),
        pl.BlockSpec((16, pw), lambda i: (jnp.maximum(i * (tm // 16) - 1, 0), pool_blk)),
        tile(d), bvec, bvec, bvec] + [const(a) for a in consts]
    return pl.pallas_call(
        functools.partial(_post_kernel, tpb=tpb, alpha=alpha, d=d, tm=tm),
        grid=(m // tm,),
        in_specs=in_specs,
        out_specs=[tile(d), tile(d), tile(LANES)],
        out_shape=[jax.ShapeDtypeStruct((m, d), F32), jax.ShapeDtypeStruct((m, d), F32),
                   jax.ShapeDtypeStruct((m, LANES), F32)],
        scratch_shapes=[pltpu.VMEM((tm + 16, pw), F32)],
        compiler_params=_cparams("arbitrary"),
        name="merge_ln1",
    )(y, r, k, v, g, p, p, p, x2, g1, sc2, sh2, *consts)


def _route_kernel(lg_ref, idx_out, prob_out, rank_out, cnt_out, carry, *, n_exp, tm):
    @pl.when(pl.program_id(0) == 0)
    def _():
        carry[...] = jnp.zeros_like(carry)

    lane = lax.broadcasted_iota(jnp.int32, (tm, LANES), 1)
    lane_f = lane.astype(F32)
    neg = jnp.float32(-jnp.inf)
    work = jnp.where(lane < n_exp, lg_ref[...], neg)
    vals, sels = [], []
    idx_acc = jnp.zeros((tm, LANES), jnp.int32)
    hot = jnp.zeros((tm, LANES), F32)
    for kk in range(TOP_K):
        mx = jnp.max(work, axis=-1, keepdims=True)
        idx = jnp.min(jnp.where(work == mx, lane_f, float(LANES)), axis=-1, keepdims=True).astype(jnp.int32)
        sel = lane == idx
        work = jnp.where(sel, neg, work)
        vals.append(mx)
        sels.append(sel)
        idx_acc = jnp.where(lane == kk, idx, idx_acc)
        hot = hot + sel.astype(F32)
    es = [jnp.exp(vv - vals[0]) for vv in vals]
    den = es[0] + es[1] + es[2] + es[3]
    prob = jnp.zeros((tm, LANES), F32)
    for kk in range(TOP_K):
        prob = jnp.where(lane == kk, es[kk] / den, prob)
    ri = lax.broadcasted_iota(jnp.int32, (tm, tm), 0)
    ci = lax.broadcasted_iota(jnp.int32, (tm, tm), 1)
    before = _dot((ri > ci).astype(BF16), hot.astype(BF16)) + carry[0:1, :]
    rank = jnp.zeros((tm, LANES), jnp.int32)
    for kk in range(TOP_K):
        rk = jnp.sum(jnp.where(sels[kk], before, 0.0), axis=-1, keepdims=True).astype(jnp.int32)
        rank = jnp.where(lane == kk, rk, rank)
    carry[...] = carry[...] + jnp.sum(hot, axis=0, keepdims=True)
    idx_out[...] = idx_acc
    prob_out[...] = prob
    rank_out[...] = rank
    cnt_out[...] = carry[...]


def _route(logits, n_exp):
    m = logits.shape[0]
    tm = 512
    tile = pl.BlockSpec((tm, LANES), lambda i: (i, 0))
    return pl.pallas_call(
        functools.partial(_route_kernel, n_exp=n_exp, tm=tm),
        grid=(m // tm,),
        in_specs=[tile],
        out_specs=[tile, tile, tile, pl.BlockSpec((8, LANES), lambda i: (0, 0))],
        out_shape=[jax.ShapeDtypeStruct((m, LANES), jnp.int32), jax.ShapeDtypeStruct((m, LANES), F32),
                   jax.ShapeDtypeStruct((m, LANES), jnp.int32), jax.ShapeDtypeStruct((8, LANES), F32)],
        scratch_shapes=[pltpu.VMEM((8, LANES), F32)],
        compiler_params=_cparams("arbitrary"),
        name="route",
    )(logits)


def _row_copy(src_hbm, dst, src_row, dst_row, sem):
    return pltpu.make_async_copy(src_hbm.at[pl.ds(src_row, 1), :], dst.at[pl.ds(dst_row, 1), :], sem)


def _dispatch_kernel(src_ref, h_hbm, o_ref, buf, sems, *, rows):
    i = pl.program_id(0)
    slot = i % 2

    def issue_block(blk, slot_):
        def body(t, carry):
            _row_copy(h_hbm, buf.at[slot_], src_ref[blk * rows + t], t, sems.at[slot_]).start()
            return carry

        lax.fori_loop(0, rows, body, 0, unroll=8)

    @pl.when(i == 0)
    def _():
        issue_block(0, 0)

    @pl.when(i + 1 < pl.num_programs(0))
    def _():
        issue_block(i + 1, 1 - slot)

    def drain(t, carry):
        _row_copy(h_hbm, buf.at[slot], 0, t, sems.at[slot]).wait()
        return carry

    lax.fori_loop(0, rows, drain, 0, unroll=8)
    o_ref[...] = buf[slot].astype(BF16)


def _dispatch(src_tok, h2, n_rows):
    d = h2.shape[1]
    rows = MOE_SUB
    return pl.pallas_call(
        functools.partial(_dispatch_kernel, rows=rows),
        grid_spec=pltpu.PrefetchScalarGridSpec(
            num_scalar_prefetch=1,
            grid=(n_rows // rows,),
            in_specs=[pl.BlockSpec(memory_space=pl.ANY)],
            out_specs=pl.BlockSpec((rows, d), lambda i, src: (i, 0)),
            scratch_shapes=[pltpu.VMEM((2, rows, d), F32), pltpu.SemaphoreType.DMA((2,))]),
        out_shape=jax.ShapeDtypeStruct((n_rows, d), BF16),
        compiler_params=_cparams("arbitrary"),
        name="dispatch",
    )(src_tok, h2)


def _roll_lanes(v, shift):
    return jnp.concatenate([pltpu.roll(v[:, g * LANES:(g + 1) * LANES], shift, axis=1)
                            for g in range(v.shape[1] // LANES)], axis=1)


def _expert_kernel(e_ref, st_ref, ns_ref, tot_ref, x_ref, wgu_ref, bgu_ref, wd_ref, bd_ref,
                   y_hbm, acc, wgu_b, wd_i, wd_b, act, sem, *, n_ff, rmax, sub, n_rows):
    i = pl.program_id(0)
    j = pl.program_id(1)
    nsub = ns_ref[i]
    n_sub_max = rmax // sub
    tfc = wd_ref.shape[1]
    half = tfc // 2

    def out_copy(row0, s):
        return pltpu.make_async_copy(acc.at[pl.ds(s * sub, sub), :],
                                     y_hbm.at[pl.ds(pl.multiple_of(row0, sub), sub), :], sem)

    def for_valid_subs(item, fn):
        for s in range(n_sub_max):
            @pl.when(s < ns_ref[item])
            def _():
                fn(out_copy(st_ref[item] + s * sub, s))

    @pl.when(nsub > 0)
    def _():
        wgu_b[...] = wgu_ref[0].astype(BF16)
        for g in range(wd_b.shape[1] // LANES):
            cols = slice(g * LANES, (g + 1) * LANES)
            wd_i[g % 2, pl.ds(0, half, stride=2), :] = wd_ref[0, 0:half, cols]
            wd_i[g % 2, pl.ds(1, half, stride=2), :] = wd_ref[0, half:tfc, cols]
            wd_b[:, cols] = wd_i[g % 2].astype(BF16)

    cw = 2 * LANES

    def paired(gu):
        gate = jnp.minimum(gu, SWIGLU_LIMIT)
        glu = gate * _sigmoid(gate * SWIGLU_ALPHA)
        up1 = jnp.clip(gu, -SWIGLU_LIMIT, SWIGLU_LIMIT) + 1.0
        return glu * _roll_lanes(up1, LANES - 1)

    def activate(row0, m_rows):
        rows = pl.ds(pl.multiple_of(row0, sub), m_rows)
        xm = x_ref[rows, :]
        even = (lax.broadcasted_iota(jnp.int32, (m_rows, cw), 1) % 2) == 0
        for c in range(tfc // cw):
            lo = slice(c * cw, (c + 1) * cw)
            hi = slice(tfc + c * cw, tfc + (c + 1) * cw)
            pa = paired(_dot(xm, wgu_b[:, lo]) + bgu_ref[0, :, lo])
            pb = paired(_dot(xm, wgu_b[:, hi]) + bgu_ref[0, :, hi])
            act[rows, lo] = jnp.where(even, pa, _roll_lanes(pb, 1)).astype(BF16)

    def project(row0, m_rows):
        rows = pl.ds(pl.multiple_of(row0, sub), m_rows)
        acc[rows, :] += _dot(act[rows, :], wd_b[...])

    def over_row_blocks(fn):
        def pair_body(b, carry):
            fn(b * (2 * sub), 2 * sub)
            return carry

        lax.fori_loop(0, nsub // 2, pair_body, 0)

        @pl.when(nsub % 2 == 1)
        def _():
            fn((nsub - 1) * sub, sub)

    over_row_blocks(activate)

    @pl.when(j == 0)
    def _():
        @pl.when(i > 0)
        def _():
            for_valid_subs(i - 1, lambda cp: cp.wait())

        acc[...] = jnp.broadcast_to(bd_ref[0], acc.shape)

    over_row_blocks(project)

    @pl.when(j == n_ff - 1)
    def _():
        for_valid_subs(i, lambda cp: cp.start())

        @pl.when(i == pl.num_programs(0) - 1)
        def _():
            for_valid_subs(i, lambda cp: cp.wait())
            acc[0:sub, :] = jnp.zeros((sub, acc.shape[1]), F32)
            first = tot_ref[0] // sub

            def fill_start(blk, carry):
                out_copy(blk * sub, 0).start()
                return carry

            def fill_wait(blk, carry):
                out_copy(blk * sub, 0).wait()
                return carry

            lax.fori_loop(first, n_rows // sub, fill_start, 0)
            lax.fori_loop(first, n_rows // sub, fill_wait, 0)


def _experts(item_e, item_start, item_nsub, total_rows, x_buf, w_gu, b_gu, w_down, b_down, n_rows):
    n_items = item_e.shape[0]
    n_exp, dff, d = w_down.shape
    tf = MOE_TF
    n_ff = dff // tf
    rmax = MOE_RMAX

    def jj(i, j, ns):
        return jnp.where(ns[i] > 0, j, n_ff - 1)

    in_specs = [
        pl.BlockSpec((pl.Element(rmax), pl.Element(d)),
                     lambda i, j, e, st, ns, tot: (pl.multiple_of(st[i], MOE_SUB), 0),
                     pipeline_mode=pl.Buffered(1)),
        pl.BlockSpec((1, d, 2 * tf), lambda i, j, e, st, ns, tot: (e[i], 0, jj(i, j, ns))),
        pl.BlockSpec((1, 1, 2 * tf), lambda i, j, e, st, ns, tot: (e[i], 0, jj(i, j, ns))),
        pl.BlockSpec((1, tf, d), lambda i, j, e, st, ns, tot: (e[i], jj(i, j, ns), 0)),
        pl.BlockSpec((1, 1, d), lambda i, j, e, st, ns, tot: (e[i], 0, 0)),
    ]
    return pl.pallas_call(
        functools.partial(_expert_kernel, n_ff=n_ff, rmax=rmax, sub=MOE_SUB, n_rows=n_rows),
        grid_spec=pltpu.PrefetchScalarGridSpec(
            num_scalar_prefetch=4,
            grid=(n_items, n_ff),
            in_specs=in_specs,
            out_specs=pl.BlockSpec(memory_space=pl.ANY),
            scratch_shapes=[pltpu.VMEM((rmax, d), F32), pltpu.VMEM((d, 2 * tf), BF16),
                            pltpu.VMEM((2, tf, LANES), F32), pltpu.VMEM((tf, d), BF16),
                            pltpu.VMEM((rmax, tf), BF16), pltpu.SemaphoreType.DMA(())]),
        out_shape=jax.ShapeDtypeStruct((n_rows, d), F32),
        compiler_params=_cparams("arbitrary", "arbitrary"),
        name="experts",
    )(item_e, item_start, item_nsub, total_rows, x_buf, w_gu, b_gu.reshape(n_exp, 1, 2 * dff),
      w_down, b_down.reshape(n_exp, 1, d))


def _combine_kernel(dest_ref, y_hbm, prob_ref, x1_ref, g2_ref, lg_ref, lb_ref, o_ref, buf, sems,
                    *, tm, alpha):
    i = pl.program_id(0)
    slot = i % 2

    def issue_block(blk, slot_):
        def body(t, carry):
            for kk in range(TOP_K):
                _row_copy(y_hbm, buf.at[slot_, kk], dest_ref[(blk * tm + t) * TOP_K + kk], t,
                          sems.at[slot_]).start()
            return carry

        lax.fori_loop(0, tm, body, 0, unroll=2)

    @pl.when(i == 0)
    def _():
        issue_block(0, 0)

    @pl.when(i + 1 < pl.num_programs(0))
    def _():
        issue_block(i + 1, 1 - slot)

    def drain(t, carry):
        for kk in range(TOP_K):
            _row_copy(y_hbm, buf.at[slot, kk], 0, t, sems.at[slot]).wait()
        return carry

    lax.fori_loop(0, tm, drain, 0, unroll=2)
    prob = prob_ref[...]
    f = prob[:, 0:1] * buf[slot, 0]
    for kk in range(1, TOP_K):
        f = f + prob[:, kk:kk + 1] * buf[slot, kk]
    xr = alpha * x1_ref[...] + g2_ref[0] * f
    m2 = jnp.mean(xr, axis=-1, keepdims=True)
    xc = xr - m2
    v2 = jnp.mean(xc * xc, axis=-1, keepdims=True)
    o_ref[...] = xc * lax.rsqrt(v2 + LN_EPS) * lg_ref[...] + lb_ref[...]


def _combine(dest, y_buf, prob, x1, g2, ln_g, ln_b, seq, alpha):
    m, d = x1.shape
    tm = 128
    tpb = seq // tm
    return pl.pallas_call(
        functools.partial(_combine_kernel, tm=tm, alpha=alpha),
        grid_spec=pltpu.PrefetchScalarGridSpec(
            num_scalar_prefetch=1,
            grid=(m // tm,),
            in_specs=[pl.BlockSpec(memory_space=pl.ANY),
                      pl.BlockSpec((tm, LANES), lambda i, dst: (i, 0)),
                      pl.BlockSpec((tm, d), lambda i, dst: (i, 0)),
                      pl.BlockSpec((1, 1, d), lambda i, dst: (i // tpb, 0, 0)),
                      pl.BlockSpec((1, d), lambda i, dst: (0, 0)),
                      pl.BlockSpec((1, d), lambda i, dst: (0, 0))],
            out_specs=pl.BlockSpec((tm, d), lambda i, dst: (i, 0)),
            scratch_shapes=[pltpu.VMEM((2, TOP_K, tm, d), F32), pltpu.SemaphoreType.DMA((2,))]),
        out_shape=jax.ShapeDtypeStruct((m, d), F32),
        compiler_params=_cparams("arbitrary"),
        name="combine_ln2",
    )(dest, y_buf, prob, x1, g2, ln_g.reshape(1, d), ln_b.reshape(1, d))


def _pad_rows(w, rows):
    return jnp.pad(w, ((0, rows - w.shape[0]), (0, 0)))


def _pad_cols(w, cols):
    return jnp.pad(w, ((0, 0), (0, cols - w.shape[1])))


def _layer(x2, c8, bsz, seq, alpha, w_ada, b_ada, w_in, mu_shift, w0, w_decay_up, a0, w_iclr_up,
           w_glora_up, k_k, k_a, r_k, lnx_g, lnx_b, w_rwkv_out, w_pool, pool_scale, w_out, ln1_g,
           ln1_b, w_router, b_router, w_gu, b_gu, w_down, b_down, ln2_g, ln2_b):
    m, d = x2.shape
    c = w0.shape[0]
    n_dec, n_icl, n_gate = w_decay_up.shape[0], w_iclr_up.shape[0], w_glora_up.shape[0]
    pw = w_pool.shape[0] * w_pool.shape[1]
    n_exp = w_router.shape[1]
    assert n_dec <= LANES and n_icl <= LANES and n_gate <= 2 * LANES and d == 2 * c and pw == c

    mod = _adaln(c8, w_ada, b_ada)[:bsz]
    sh1, sc1, g1, sh2, sc2, g2 = [t.reshape(bsz, 1, d) for t in jnp.split(mod, 6, axis=-1)]

    o = 3 * c
    segs = [(o, n_dec, LANES), (o + n_dec, n_icl, LANES), (o + n_dec + n_icl, n_gate, 2 * LANES)]
    n_shift = o + n_dec + n_icl + n_gate
    zpad = 2 * d - 3 * c - 4 * LANES
    w_shift = jnp.concatenate([w_in[:, :o]] + [_pad_cols(w_in[:, s:s + n], wd) for s, n, wd in segs]
                              + [jnp.zeros((d, zpad), F32)], axis=1)
    mu = jnp.concatenate([mu_shift[:o]] + [jnp.pad(mu_shift[s:s + n], (0, wd - n)) for s, n, wd in segs]
                         + [jnp.zeros((zpad,), F32)]).reshape(1, -1)
    w_cat = jnp.concatenate([w_in[:, n_shift + pw:], w_shift, w_in[:, n_shift:n_shift + pw]],
                            axis=1).astype(BF16)
    p = _inproj(x2, sc1, sh1, w_cat, seq)

    ones_bd = jnp.kron(jnp.eye(LANES // HEAD_SIZE, dtype=F32), jnp.ones((HEAD_SIZE, HEAD_SIZE), F32)).astype(BF16)
    r, lw, kmod, v, kkn, bvec, g = _prep(
        p, mu, w0, a0, k_k, k_a, _pad_rows(w_decay_up, LANES).astype(BF16),
        _pad_rows(w_iclr_up, LANES).astype(BF16), _pad_rows(w_glora_up, 2 * LANES).astype(BF16),
        ones_bd, seq, c, shift_blk=1)
    y = _wkv(r, lw, kmod, v, kkn, bvec, bsz, seq, c)

    w_r = _pad_cols(w_router, LANES).astype(BF16)
    b_r = jnp.pad(b_router, (0, LANES - n_exp))
    x1, h2, logits = _post(y, r, kmod, v, g, p, x2, g1, sc2, sh2, r_k.reshape(-1), lnx_g, lnx_b, ones_bd,
                           w_rwkv_out.astype(BF16), w_pool.astype(BF16), pool_scale, w_out.astype(BF16),
                           ln1_g, ln1_b, w_r, b_r, seq, alpha, gate_blk=0, pool_blk=(4 * d) // pw)

    idx, prob, rank, cnt = _route(logits, n_exp)
    counts = cnt[0, :n_exp].astype(jnp.int32)
    padded = (counts + MOE_SUB - 1) // MOE_SUB * MOE_SUB
    pad_end = jnp.cumsum(padded)
    pad_start = pad_end - padded
    mk = m * TOP_K
    n_rows = (mk // MOE_SUB + n_exp) * MOE_SUB
    dest = (pad_start[idx[:, :TOP_K]] + rank[:, :TOP_K]).reshape(mk).astype(jnp.int32)
    src_tok = (jnp.arange(n_rows + MOE_RMAX, dtype=jnp.int32) % m).at[dest].set(
        jnp.arange(mk, dtype=jnp.int32) // TOP_K)
    n_items = mk // MOE_RMAX + n_exp
    per_e = (padded + MOE_RMAX - 1) // MOE_RMAX
    it_end = jnp.cumsum(per_e)
    it_start = it_end - per_e
    ids = jnp.arange(n_items, dtype=jnp.int32)
    item_e = jnp.minimum(jnp.searchsorted(it_end, ids, side='right'), n_exp - 1).astype(jnp.int32)
    local = ids - it_start[item_e]
    used = ids < it_end[-1]
    item_start = jnp.where(used, pad_start[item_e] + local * MOE_RMAX, n_rows).astype(jnp.int32)
    item_nsub = jnp.where(used, jnp.clip((padded[item_e] - local * MOE_RMAX) // MOE_SUB, 0,
                                         MOE_RMAX // MOE_SUB), 0).astype(jnp.int32)

    x_buf = _dispatch(src_tok, h2, n_rows + MOE_RMAX)
    y_buf = _experts(item_e, item_start, item_nsub, pad_end[-1:].astype(jnp.int32), x_buf,
                     w_gu, b_gu, w_down, b_down, n_rows)
    return _combine(dest, y_buf, prob, x1, g2, ln2_g, ln2_b, seq, alpha)


def kernel(x, c, w_ada, b_ada, w_in, mu_shift, w0, w_decay_up, a0, w_iclr_up, w_glora_up, k_k, k_a, r_k, lnx_g, lnx_b, w_rwkv_out, w_pool, pool_scale, w_out, ln1_g, ln1_b, w_router, b_router, w_gu, b_gu, w_down, b_down, ln2_g, ln2_b):
    bsz, seq, d = x.shape
    depth = w_ada.shape[0]
    alpha = float((2 * depth) ** 0.25)
    c8 = jnp.pad(c, ((0, 8 - bsz), (0, 0)))
    x2 = x.reshape(bsz * seq, d)
    weights = (w_ada, b_ada, w_in, mu_shift, w0, w_decay_up, a0, w_iclr_up, w_glora_up, k_k, k_a, r_k,
               lnx_g, lnx_b, w_rwkv_out, w_pool, pool_scale, w_out, ln1_g, ln1_b, w_router, b_router,
               w_gu, b_gu, w_down, b_down, ln2_g, ln2_b)
    for l in range(depth):
        x2 = _layer(x2, c8, bsz, seq, alpha, *[w[l] for w in weights])
    return x2.reshape(bsz, seq, d)
```

```python
import functools

import jax
import jax.numpy as jnp
from jax import lax
from jax.experimental import pallas as pl
from jax.experimental.pallas import tpu as pltpu

F32 = jnp.float32
BF16 = jnp.bfloat16

HEAD_SIZE = 64
LANES = 128
TOP_K = 4
LN_EPS = 1e-5
GN_EPS = 64e-5
POOL_WINDOWS = (2, 4, 8, 16)
SWIGLU_LIMIT = 7.0
SWIGLU_ALPHA = 1.702
WKV_CHUNK = 64
MOE_SUB = 256
MOE_RMAX = 1536
MOE_TF = 512
VMEM_LIMIT = 56 * 1024 * 1024

_NT = (((1,), (1,)), ((), ()))
_TN = (((0,), (0,)), ((), ()))


def _cparams(*sem):
    return pltpu.CompilerParams(dimension_semantics=sem, vmem_limit_bytes=VMEM_LIMIT)


def _sigmoid(z):
    return 1.0 / (1.0 + jnp.exp(-z))


def _dot(a, b):
    return jnp.dot(a, b, preferred_element_type=F32)


def _split_dot(xv, w):
    hi = xv.astype(BF16)
    lo = (xv - hi.astype(F32)).astype(BF16)
    return _dot(hi, w) + _dot(lo, w)


def _head_sum(xv, ones_bd):
    outs = []
    for p in range(xv.shape[1] // LANES):
        outs.append(_split_dot(xv[:, p * LANES:(p + 1) * LANES], ones_bd))
    return jnp.concatenate(outs, axis=1)


def _adaln_kernel(c_ref, w_ref, b_ref, o_ref):
    c = c_ref[...]
    cond = c * _sigmoid(c)
    o_ref[...] = _dot(cond.astype(BF16), w_ref[...].astype(BF16)) + b_ref[...]


def _adaln(c8, w, b):
    d, n = w.shape
    tn = 1024
    return pl.pallas_call(
        _adaln_kernel,
        grid=(n // tn,),
        in_specs=[pl.BlockSpec((8, d), lambda j: (0, 0)),
                  pl.BlockSpec((d, tn), lambda j: (0, j)),
                  pl.BlockSpec((1, tn), lambda j: (0, j))],
        out_specs=pl.BlockSpec((8, tn), lambda j: (0, j)),
        out_shape=jax.ShapeDtypeStruct((8, n), F32),
        compiler_params=_cparams("arbitrary"),
        name="adaln",
    )(c8, w, b.reshape(1, n))


def _inproj_kernel(x_ref, sc_ref, sh_ref, w_ref, o_ref, h_scr):
    @pl.when(pl.program_id(1) == 0)
    def _():
        h_scr[...] = (x_ref[...] * (1.0 + sc_ref[0]) + sh_ref[0]).astype(BF16)

    o_ref[...] = _dot(h_scr[...], w_ref[...])


def _inproj(x2, sc, sh, w, seq):
    m, d = x2.shape
    n = w.shape[1]
    tm, tn = 1024, 1024
    tpb = seq // tm
    return pl.pallas_call(
        _inproj_kernel,
        grid=(m // tm, n // tn),
        in_specs=[pl.BlockSpec((tm, d), lambda i, j: (i, 0)),
                  pl.BlockSpec((1, 1, d), lambda i, j: (i // tpb, 0, 0)),
                  pl.BlockSpec((1, 1, d), lambda i, j: (i // tpb, 0, 0)),
                  pl.BlockSpec((d, tn), lambda i, j: (0, j))],
        out_specs=pl.BlockSpec((tm, tn), lambda i, j: (i, j)),
        out_shape=jax.ShapeDtypeStruct((m, n), F32),
        scratch_shapes=[pltpu.VMEM((tm, d), BF16)],
        compiler_params=_cparams("arbitrary", "arbitrary"),
        name="inproj",
    )(x2, sc, sh, w)


def _prep_kernel(p_ref, prev_ref, mu_ref, w0_ref, a0_ref, kk_ref, ka_ref, wd_ref, wa_ref, wg_ref,
                 ones_ref, r_out, lw_out, k_out, v_out, kkn_out, b_out, g_out, *, tpb, c):
    cur = p_ref[...]
    first = (pl.program_id(0) % tpb) == 0
    prev_row = jnp.where(first, 0.0, prev_ref[7:8, :])
    shifted = pltpu.roll(cur, 1, axis=0)
    row0 = lax.broadcasted_iota(jnp.int32, cur.shape, 0) == 0
    shifted = jnp.where(row0, prev_row, shifted)
    ps = cur + (shifted - cur) * mu_ref[...]
    r = ps[:, 0:c]
    k = ps[:, c:2 * c]
    v = ps[:, 2 * c:3 * c]
    o = 3 * c
    wdp = ps[:, o:o + LANES]
    adp = ps[:, o + LANES:o + 2 * LANES]
    gdp = ps[:, o + 2 * LANES:o + 4 * LANES]
    z = w0_ref[...] + _dot(jnp.tanh(wdp).astype(BF16), wd_ref[...])
    lw = -jnp.exp(-0.5) * _sigmoid(z)
    iclr = _sigmoid(a0_ref[...] + _dot(adp.astype(BF16), wa_ref[...]))
    g = _dot(_sigmoid(gdp).astype(BF16), wg_ref[...])
    kk0 = k * kk_ref[...]
    ss = _head_sum(kk0 * kk0, ones_ref[...])
    kkn = kk0 / jnp.maximum(jnp.sqrt(ss), 1e-12)
    kmod = k * (1.0 + (iclr - 1.0) * ka_ref[...])
    r_out[...] = r
    lw_out[...] = lw
    k_out[...] = kmod
    v_out[...] = v
    kkn_out[...] = kkn
    b_out[...] = kkn * iclr
    g_out[...] = g


def _prep(p, mu, w0, a0, k_k, k_a, wd, wa, wg, ones_bd, seq, c, shift_blk):
    m = p.shape[0]
    ns = mu.shape[1]
    tm = 256
    tpb = seq // tm
    row = lambda a: a.reshape(1, -1)
    full = lambda a: pl.BlockSpec(a.shape, lambda i: (0,) * a.ndim)
    out = jax.ShapeDtypeStruct((m, c), F32)
    ospec = pl.BlockSpec((tm, c), lambda i: (i, 0))
    args = (p, p, mu, row(w0), row(a0), row(k_k), row(k_a), wd, wa, wg, ones_bd)
    in_specs = [pl.BlockSpec((tm, ns), lambda i: (i, shift_blk)),
                pl.BlockSpec((8, ns), lambda i: (jnp.maximum(i * (tm // 8) - 1, 0), shift_blk))]
    in_specs += [full(a) for a in args[2:]]
    return pl.pallas_call(
        functools.partial(_prep_kernel, tpb=tpb, c=c),
        grid=(m // tm,),
        in_specs=in_specs,
        out_specs=[ospec] * 7,
        out_shape=[out] * 7,
        compiler_params=_cparams("arbitrary"),
        name="rwkv_prep",
    )(*args)


def _wkv_kernel(r_ref, lw_ref, k_ref, v_ref, kk_ref, b_ref, y_ref, s_scr, *, chunk, c):
    L = chunk

    @pl.when(pl.program_id(1) == 0)
    def _():
        s_scr[...] = jnp.zeros_like(s_scr)

    ri = lax.broadcasted_iota(jnp.int32, (L, L), 0)
    ci = lax.broadcasted_iota(jnp.int32, (L, L), 1)
    tri = (ri >= ci).astype(BF16)
    lw = lw_ref[...]
    cs = _split_dot_lhs(tri, lw)
    cs_last = cs[L - 1:L, :]
    g_in = jnp.exp(cs)
    g_ex = jnp.exp(cs - lw)
    g_inv = jnp.exp(-cs)
    g_tail = jnp.exp(cs_last - cs)
    g_last = jnp.exp(cs_last)
    kk = kk_ref[...]
    bb = b_ref[...]
    kx = k_ref[...]
    vv = v_ref[...]
    a_t = -kk * g_ex
    r_t = r_ref[...] * g_in
    b_t = bb * g_inv
    k_t = kx * g_inv
    b_p = bb * g_tail
    k_p = kx * g_tail

    r2 = lax.broadcasted_iota(jnp.int32, (2 * L, 2 * L), 0)
    c2 = lax.broadcasted_iota(jnp.int32, (2 * L, 2 * L), 1)
    same = (r2 >= L) == (c2 >= L)
    m_strict = same & (r2 > c2)
    m_incl = same & (r2 >= c2)
    lane = lax.broadcasted_iota(jnp.int32, (L, LANES), 1)
    low = lane < HEAD_SIZE

    def stack(zv):
        return jnp.concatenate([jnp.where(low, zv, 0.0), jnp.where(low, 0.0, zv)], axis=0)

    pairs = range(c // LANES)
    sls = [slice(p * LANES, (p + 1) * LANES) for p in pairs]
    a_s = [stack(a_t[:, sl]) for sl in sls]
    r_s = [stack(r_t[:, sl]) for sl in sls]
    v_s = [stack(vv[:, sl]) for sl in sls]
    ar = [jnp.concatenate([a_s[p], r_s[p]], axis=0).astype(BF16) for p in pairs]
    qb = [lax.dot_general(ar[p], stack(b_t[:, sls[p]]).astype(BF16), _NT, preferred_element_type=F32)
          for p in pairs]
    qk = [lax.dot_general(ar[p], stack(k_t[:, sls[p]]).astype(BF16), _NT, preferred_element_type=F32)
          for p in pairs]
    a_ab = [jnp.where(m_strict, q[:2 * L], 0.0) for q in qb]
    a_rb = [jnp.where(m_incl, q[2 * L:], 0.0) for q in qb]
    a_ak = [jnp.where(m_strict, q[:2 * L], 0.0) for q in qk]
    a_rk = [jnp.where(m_incl, q[2 * L:], 0.0) for q in qk]
    v_b = [x.astype(BF16) for x in v_s]
    z = [jnp.concatenate([_dot(a_ak[p].astype(BF16), v_b[p]), a_s[p]], axis=1) for p in pairs]
    amat = a_ab
    n_sq = max(L.bit_length() - 1, 1)
    for it in range(n_sq):
        ab = [x.astype(BF16) for x in amat]
        z = [z[p] + _dot(ab[p], z[p].astype(BF16)) for p in pairs]
        if it + 1 < n_sq:
            amat = [_dot(x, x) for x in ab]
    s_old = [s_scr[p] for p in pairs]
    s_b = [x.astype(BF16) for x in s_old]
    u_s = [_dot(z[p][:, LANES:].astype(BF16), s_b[p]) + z[p][:, :LANES] for p in pairs]
    uv = [jnp.concatenate([u_s[p], v_s[p]], axis=0).astype(BF16) for p in pairs]
    y_s = [_dot(r_s[p].astype(BF16), s_b[p])
           + _dot(jnp.concatenate([a_rb[p], a_rk[p]], axis=1).astype(BF16), uv[p]) for p in pairs]
    for p in pairs:
        y_ref[:, sls[p]] = y_s[p][:L] + y_s[p][L:]
    upd = [lax.dot_general(jnp.concatenate([stack(b_p[:, sls[p]]), stack(k_p[:, sls[p]])], axis=0).astype(BF16),
                           uv[p], _TN, preferred_element_type=F32) for p in pairs]
    for p in pairs:
        g_col = jnp.transpose(jnp.broadcast_to(g_last[:, sls[p]], (LANES, LANES)))
        s_scr[p] = s_old[p] * g_col + upd[p]


def _split_dot_lhs(w, xv):
    hi = xv.astype(BF16)
    lo = (xv - hi.astype(F32)).astype(BF16)
    return _dot(w, hi) + _dot(w, lo)


def _wkv(r, lw, k, v, kk, b, bsz, seq, c):
    L = WKV_CHUNK
    nc = seq // L
    spec = pl.BlockSpec((L, c), lambda bi, ci: (bi * nc + ci, 0))
    return pl.pallas_call(
        functools.partial(_wkv_kernel, chunk=L, c=c),
        grid=(bsz, nc),
        in_specs=[spec] * 6,
        out_specs=spec,
        out_shape=jax.ShapeDtypeStruct((bsz * seq, c), F32),
        scratch_shapes=[pltpu.VMEM((c // LANES, LANES, LANES), F32)],
        compiler_params=_cparams("arbitrary", "arbitrary"),
        name="wkv7",
    )(r, lw, k, v, kk, b)


def _post_kernel(y_ref, r_ref, k_ref, v_ref, g_ref, gate_ref, pool_ref, halo_ref, x_ref,
                 g1_ref, sc2_ref, sh2_ref, rk_ref, lng_ref, lnb_ref, ones_ref, wro_ref, wp_ref,
                 psc_ref, wo_ref, l1g_ref, l1b_ref, wr_ref, br_ref,
                 x1_out, h2_out, lg_out, pool_scr, *, tpb, alpha, d, tm):
    ones_bd = ones_ref[...]
    inv_n = 1.0 / HEAD_SIZE
    y = y_ref[...]
    mu = _head_sum(y, ones_bd) * inv_n
    dy = y - mu
    var = _head_sum(dy * dy, ones_bd) * inv_n
    yn = dy * lax.rsqrt(var + GN_EPS) * lng_ref[...] + lnb_ref[...]
    vv = v_ref[...]
    bonus = _head_sum(r_ref[...] * k_ref[...] * rk_ref[...], ones_bd) * vv
    y_rwkv = ((yn + bonus) * g_ref[...]).astype(BF16)
    branch_a = _dot(y_rwkv, wro_ref[...])

    first = (pl.program_id(0) % tpb) == 0
    halo = 16
    pool_scr[0:halo, :] = jnp.where(first, 0.0, halo_ref[...])
    pool_scr[halo:, :] = pool_ref[...]
    pos = (pl.program_id(0) % tpb) * tm + lax.broadcasted_iota(jnp.int32, (tm, 1), 0)
    gw = pool_ref.shape[1] // len(POOL_WINDOWS)
    dg = d // len(POOL_WINDOWS)
    outs = []
    for gi, win in enumerate(POOL_WINDOWS):
        cols = slice(gi * gw, (gi + 1) * gw)
        u = pool_scr[halo:, cols]
        wsum = u
        for s in range(1, win):
            wsum = wsum + pool_scr[halo - s:halo - s + tm, cols]
        cnt = jnp.minimum(pos + 1, win).astype(F32)
        dd = (wsum / cnt - u).astype(BF16)
        outs.append(_dot(dd, wp_ref[gi]))
    branch_b = jnp.concatenate(outs, axis=1) * psc_ref[...]

    gates = gate_ref[...]
    mixin = _sigmoid(gates[:, :d]) * branch_a + _sigmoid(gates[:, d:]) * branch_b
    mix = _dot(mixin.astype(BF16), wo_ref[...])
    xr = alpha * x_ref[...] + g1_ref[0] * mix
    m1 = jnp.mean(xr, axis=-1, keepdims=True)
    xc = xr - m1
    v1 = jnp.mean(xc * xc, axis=-1, keepdims=True)
    x1 = xc * lax.rsqrt(v1 + LN_EPS) * l1g_ref[...] + l1b_ref[...]
    x1_out[...] = x1
    h2 = x1 * (1.0 + sc2_ref[0]) + sh2_ref[0]
    h2_out[...] = h2
    lg_out[...] = _dot(h2.astype(BF16), wr_ref[...]) + br_ref[...]


def _post(y, r, k, v, g, p, x2, g1, sc2, sh2, r_k, lnx_g, lnx_b, ones_bd, w_ro, w_pool, pool_scale,
          w_out, ln1_g, ln1_b, w_r, b_r, seq, alpha, gate_blk, pool_blk):
    m, d = x2.shape
    c = y.shape[1]
    pw = w_pool.shape[0] * w_pool.shape[1]
    tm = 256
    tpb = seq // tm
    row = lambda a: a.reshape(1, -1)
    const = lambda a: pl.BlockSpec(a.shape, lambda i: (0,) * a.ndim, pipeline_mode=pl.Buffered(1))
    tile = lambda w: pl.BlockSpec((tm, w), lambda i: (i, 0))
    bvec = pl.BlockSpec((1, 1, d), lambda i: (i // tpb, 0, 0))
    consts = (row(r_k), row(lnx_g), row(lnx_b), ones_bd, w_ro, w_pool, row(pool_scale), w_out,
              row(ln1_g), row(ln1_b), w_r, row(b_r))
    in_specs = [tile(c)] * 5 + [
        pl.BlockSpec((tm, 2 * d), lambda i: (i, gate_blk)),
        pl.BlockSpec((tm, pw), lambda i: (i, pool_blk)),
        pl.BlockSpec((16, pw), lambda i: (jnp.maximum(i * (tm // 16) - 1, 0), pool_blk)),
        tile(d), bvec, bvec, bvec] + [const(a) for a in consts]
    return pl.pallas_call(
        functools.partial(_post_kernel, tpb=tpb, alpha=alpha, d=d, tm=tm),
        grid=(m // tm,),
        in_specs=in_specs,
        out_specs=[tile(d), tile(d), tile(LANES)],
        out_shape=[jax.ShapeDtypeStruct((m, d), F32), jax.ShapeDtypeStruct((m, d), F32),
                   jax.ShapeDtypeStruct((m, LANES), F32)],
        scratch_shapes=[pltpu.VMEM((tm + 16, pw), F32)],
        compiler_params=_cparams("arbitrary"),
        name="merge_ln1",
    )(y, r, k, v, g, p, p, p, x2, g1, sc2, sh2, *consts)


def _route_kernel(lg_ref, idx_out, prob_out, rank_out, cnt_out, carry, *, n_exp, tm):
    @pl.when(pl.program_id(0) == 0)
    def _():
        carry[...] = jnp.zeros_like(carry)

    lane = lax.broadcasted_iota(jnp.int32, (tm, LANES), 1)
    lane_f = lane.astype(F32)
    neg = jnp.float32(-jnp.inf)
    work = jnp.where(lane < n_exp, lg_ref[...], neg)
    vals, sels = [], []
    idx_acc = jnp.zeros((tm, LANES), jnp.int32)
    hot = jnp.zeros((tm, LANES), F32)
    for kk in range(TOP_K):
        mx = jnp.max(work, axis=-1, keepdims=True)
        idx = jnp.min(jnp.where(work == mx, lane_f, float(LANES)), axis=-1, keepdims=True).astype(jnp.int32)
        sel = lane == idx
        work = jnp.where(sel, neg, work)
        vals.append(mx)
        sels.append(sel)
        idx_acc = jnp.where(lane == kk, idx, idx_acc)
        hot = hot + sel.astype(F32)
    es = [jnp.exp(vv - vals[0]) for vv in vals]
    den = es[0] + es[1] + es[2] + es[3]
    prob = jnp.zeros((tm, LANES), F32)
    for kk in range(TOP_K):
        prob = jnp.where(lane == kk, es[kk] / den, prob)
    ri = lax.broadcasted_iota(jnp.int32, (tm, tm), 0)
    ci = lax.broadcasted_iota(jnp.int32, (tm, tm), 1)
    before = _dot((ri > ci).astype(BF16), hot.astype(BF16)) + carry[0:1, :]
    rank = jnp.zeros((tm, LANES), jnp.int32)
    for kk in range(TOP_K):
        rk = jnp.sum(jnp.where(sels[kk], before, 0.0), axis=-1, keepdims=True).astype(jnp.int32)
        rank = jnp.where(lane == kk, rk, rank)
    carry[...] = carry[...] + jnp.sum(hot, axis=0, keepdims=True)
    idx_out[...] = idx_acc
    prob_out[...] = prob
    rank_out[...] = rank
    cnt_out[...] = carry[...]


def _route(logits, n_exp):
    m = logits.shape[0]
    tm = 512
    tile = pl.BlockSpec((tm, LANES), lambda i: (i, 0))
    return pl.pallas_call(
        functools.partial(_route_kernel, n_exp=n_exp, tm=tm),
        grid=(m // tm,),
        in_specs=[tile],
        out_specs=[tile, tile, tile, pl.BlockSpec((8, LANES), lambda i: (0, 0))],
        out_shape=[jax.ShapeDtypeStruct((m, LANES), jnp.int32), jax.ShapeDtypeStruct((m, LANES), F32),
                   jax.ShapeDtypeStruct((m, LANES), jnp.int32), jax.ShapeDtypeStruct((8, LANES), F32)],
        scratch_shapes=[pltpu.VMEM((8, LANES), F32)],
        compiler_params=_cparams("arbitrary"),
        name="route",
    )(logits)


def _row_copy(src_hbm, dst, src_row, dst_row, sem):
    return pltpu.make_async_copy(src_hbm.at[pl.ds(src_row, 1), :], dst.at[pl.ds(dst_row, 1), :], sem)


def _dispatch_kernel(src_ref, h_hbm, o_ref, buf, sems, *, rows):
    i = pl.program_id(0)
    slot = i % 2

    def issue_block(blk, slot_):
        def body(u, carry):
            for q in range(2):
                t = 2 * u + q
                _row_copy(h_hbm, buf.at[slot_], src_ref[blk * rows + t], t, sems.at[slot_]).start(priority=q)
            return carry

        lax.fori_loop(0, rows // 2, body, 0, unroll=4)

    @pl.when(i == 0)
    def _():
        issue_block(0, 0)

    @pl.when(i + 1 < pl.num_programs(0))
    def _():
        issue_block(i + 1, 1 - slot)

    def drain(t, carry):
        _row_copy(h_hbm, buf.at[slot], 0, t, sems.at[slot]).wait()
        return carry

    lax.fori_loop(0, rows, drain, 0, unroll=8)
    o_ref[...] = buf[slot].astype(BF16)


def _dispatch(src_tok, h2, n_rows):
    d = h2.shape[1]
    rows = MOE_SUB
    return pl.pallas_call(
        functools.partial(_dispatch_kernel, rows=rows),
        grid_spec=pltpu.PrefetchScalarGridSpec(
            num_scalar_prefetch=1,
            grid=(n_rows // rows,),
            in_specs=[pl.BlockSpec(memory_space=pl.ANY)],
            out_specs=pl.BlockSpec((rows, d), lambda i, src: (i, 0)),
            scratch_shapes=[pltpu.VMEM((2, rows, d), F32), pltpu.SemaphoreType.DMA((2,))]),
        out_shape=jax.ShapeDtypeStruct((n_rows, d), BF16),
        compiler_params=_cparams("arbitrary"),
        name="dispatch",
    )(src_tok, h2)


def _roll_lanes(v, shift):
    return jnp.concatenate([pltpu.roll(v[:, g * LANES:(g + 1) * LANES], shift, axis=1)
                            for g in range(v.shape[1] // LANES)], axis=1)


def _expert_kernel(e_ref, st_ref, ns_ref, tot_ref, x_hbm, wgu_ref, bgu_ref, wd_ref, bd_ref,
                   y_hbm, x_ref, acc, wgu_b, wd_i, wd_b, act, sem, x_sem, *, n_ff, rmax, sub, n_rows):
    i = pl.program_id(0)
    j = pl.program_id(1)
    nsub = ns_ref[i]
    n_sub_max = rmax // sub
    tfc = wd_ref.shape[1]
    half = tfc // 2

    def out_copy(row0, s):
        return pltpu.make_async_copy(acc.at[pl.ds(s * sub, sub), :],
                                     y_hbm.at[pl.ds(pl.multiple_of(row0, sub), sub), :], sem)

    def for_valid_subs(item, fn):
        for s in range(n_sub_max):
            @pl.when(s < ns_ref[item])
            def _():
                fn(out_copy(st_ref[item] + s * sub, s))

    def for_item_rows(fn):
        @pl.when(j == 0)
        def _():
            for s in range(n_sub_max):
                @pl.when(s < nsub)
                def _():
                    fn(pltpu.make_async_copy(
                        x_hbm.at[pl.ds(pl.multiple_of(st_ref[i] + s * sub, sub), sub), :],
                        x_ref.at[pl.ds(s * sub, sub), :], x_sem))

    for_item_rows(lambda cp: cp.start())

    @pl.when(nsub > 0)
    def _():
        wgu_b[...] = wgu_ref[0].astype(BF16)
        for g in range(wd_b.shape[1] // LANES):
            cols = slice(g * LANES, (g + 1) * LANES)
            wd_i[g % 2, pl.ds(0, half, stride=2), :] = wd_ref[0, 0:half, cols]
            wd_i[g % 2, pl.ds(1, half, stride=2), :] = wd_ref[0, half:tfc, cols]
            wd_b[:, cols] = wd_i[g % 2].astype(BF16)

    for_item_rows(lambda cp: cp.wait())

    cw = 2 * LANES

    def paired(gu):
        gate = jnp.minimum(gu, SWIGLU_LIMIT)
        glu = gate * _sigmoid(gate * SWIGLU_ALPHA)
        up1 = jnp.clip(gu, -SWIGLU_LIMIT, SWIGLU_LIMIT) + 1.0
        return glu * _roll_lanes(up1, LANES - 1)

    def activate(row0, m_rows):
        rows = pl.ds(pl.multiple_of(row0, sub), m_rows)
        xm = x_ref[rows, :]
        even = (lax.broadcasted_iota(jnp.int32, (m_rows, cw), 1) % 2) == 0
        for c in range(tfc // cw):
            lo = slice(c * cw, (c + 1) * cw)
            hi = slice(tfc + c * cw, tfc + (c + 1) * cw)
            pa = paired(_dot(xm, wgu_b[:, lo]) + bgu_ref[0, :, lo])
            pb = paired(_dot(xm, wgu_b[:, hi]) + bgu_ref[0, :, hi])
            act[rows, lo] = jnp.where(even, pa, _roll_lanes(pb, 1)).astype(BF16)

    def project(row0, m_rows):
        rows = pl.ds(pl.multiple_of(row0, sub), m_rows)
        acc[rows, :] += _dot(act[rows, :], wd_b[...])

    def over_row_blocks(fn):
        def pair_body(b, carry):
            fn(b * (2 * sub), 2 * sub)
            return carry

        lax.fori_loop(0, nsub // 2, pair_body, 0)

        @pl.when(nsub % 2 == 1)
        def _():
            fn((nsub - 1) * sub, sub)

    over_row_blocks(activate)

    @pl.when(j == 0)
    def _():
        @pl.when(i > 0)
        def _():
            for_valid_subs(i - 1, lambda cp: cp.wait())

        acc[...] = jnp.broadcast_to(bd_ref[0], acc.shape)

    over_row_blocks(project)

    @pl.when(j == n_ff - 1)
    def _():
        for_valid_subs(i, lambda cp: cp.start())

        @pl.when(i == pl.num_programs(0) - 1)
        def _():
            for_valid_subs(i, lambda cp: cp.wait())
            acc[0:sub, :] = jnp.zeros((sub, acc.shape[1]), F32)
            first = tot_ref[0] // sub

            def fill_start(blk, carry):
                out_copy(blk * sub, 0).start()
                return carry

            def fill_wait(blk, carry):
                out_copy(blk * sub, 0).wait()
                return carry

            lax.fori_loop(first, n_rows // sub, fill_start, 0)
            lax.fori_loop(first, n_rows // sub, fill_wait, 0)


def _experts(item_e, item_start, item_nsub, total_rows, x_buf, w_gu, b_gu, w_down, b_down, n_rows):
    n_items = item_e.shape[0]
    n_exp, dff, d = w_down.shape
    tf = MOE_TF
    n_ff = dff // tf
    rmax = MOE_RMAX

    def jj(i, j, ns):
        return jnp.where(ns[i] > 0, j, n_ff - 1)

    in_specs = [
        pl.BlockSpec(memory_space=pl.ANY),
        pl.BlockSpec((1, d, 2 * tf), lambda i, j, e, st, ns, tot: (e[i], 0, jj(i, j, ns))),
        pl.BlockSpec((1, 1, 2 * tf), lambda i, j, e, st, ns, tot: (e[i], 0, jj(i, j, ns))),
        pl.BlockSpec((1, tf, d), lambda i, j, e, st, ns, tot: (e[i], jj(i, j, ns), 0)),
        pl.BlockSpec((1, 1, d), lambda i, j, e, st, ns, tot: (e[i], 0, 0)),
    ]
    return pl.pallas_call(
        functools.partial(_expert_kernel, n_ff=n_ff, rmax=rmax, sub=MOE_SUB, n_rows=n_rows),
        grid_spec=pltpu.PrefetchScalarGridSpec(
            num_scalar_prefetch=4,
            grid=(n_items, n_ff),
            in_specs=in_specs,
            out_specs=pl.BlockSpec(memory_space=pl.ANY),
            scratch_shapes=[pltpu.VMEM((rmax, d), BF16), pltpu.VMEM((rmax, d), F32),
                            pltpu.VMEM((d, 2 * tf), BF16), pltpu.VMEM((2, tf, LANES), F32),
                            pltpu.VMEM((tf, d), BF16), pltpu.VMEM((rmax, tf), BF16),
                            pltpu.SemaphoreType.DMA(()), pltpu.SemaphoreType.DMA(())]),
        out_shape=jax.ShapeDtypeStruct((n_rows, d), F32),
        compiler_params=_cparams("arbitrary", "arbitrary"),
        name="experts",
    )(item_e, item_start, item_nsub, total_rows, x_buf, w_gu, b_gu.reshape(n_exp, 1, 2 * dff),
      w_down, b_down.reshape(n_exp, 1, d))


def _combine_kernel(dest_ref, y_hbm, prob_ref, x1_ref, g2_ref, lg_ref, lb_ref, o_ref, buf, sems,
                    *, tm, alpha):
    i = pl.program_id(0)
    slot = i % 2

    def issue_block(blk, slot_):
        def body(t, carry):
            for kk in range(TOP_K):
                _row_copy(y_hbm, buf.at[slot_, kk], dest_ref[(blk * tm + t) * TOP_K + kk], t,
                          sems.at[slot_]).start(priority=kk % 2)
            return carry

        lax.fori_loop(0, tm, body, 0, unroll=2)

    @pl.when(i == 0)
    def _():
        issue_block(0, 0)

    @pl.when(i + 1 < pl.num_programs(0))
    def _():
        issue_block(i + 1, 1 - slot)

    def drain(t, carry):
        for kk in range(TOP_K):
            _row_copy(y_hbm, buf.at[slot, kk], 0, t, sems.at[slot]).wait()
        return carry

    lax.fori_loop(0, tm, drain, 0, unroll=2)
    prob = prob_ref[...]
    f = prob[:, 0:1] * buf[slot, 0]
    for kk in range(1, TOP_K):
        f = f + prob[:, kk:kk + 1] * buf[slot, kk]
    xr = alpha * x1_ref[...] + g2_ref[0] * f
    m2 = jnp.mean(xr, axis=-1, keepdims=True)
    xc = xr - m2
    v2 = jnp.mean(xc * xc, axis=-1, keepdims=True)
    o_ref[...] = xc * lax.rsqrt(v2 + LN_EPS) * lg_ref[...] + lb_ref[...]


def _combine(dest, y_buf, prob, x1, g2, ln_g, ln_b, seq, alpha):
    m, d = x1.shape
    tm = 128
    tpb = seq // tm
    return pl.pallas_call(
        functools.partial(_combine_kernel, tm=tm, alpha=alpha),
        grid_spec=pltpu.PrefetchScalarGridSpec(
            num_scalar_prefetch=1,
            grid=(m // tm,),
            in_specs=[pl.BlockSpec(memory_space=pl.ANY),
                      pl.BlockSpec((tm, LANES), lambda i, dst: (i, 0)),
                      pl.BlockSpec((tm, d), lambda i, dst: (i, 0)),
                      pl.BlockSpec((1, 1, d), lambda i, dst: (i // tpb, 0, 0)),
                      pl.BlockSpec((1, d), lambda i, dst: (0, 0)),
                      pl.BlockSpec((1, d), lambda i, dst: (0, 0))],
            out_specs=pl.BlockSpec((tm, d), lambda i, dst: (i, 0)),
            scratch_shapes=[pltpu.VMEM((2, TOP_K, tm, d), F32), pltpu.SemaphoreType.DMA((2,))]),
        out_shape=jax.ShapeDtypeStruct((m, d), F32),
        compiler_params=_cparams("arbitrary"),
        name="combine_ln2",
    )(dest, y_buf, prob, x1, g2, ln_g.reshape(1, d), ln_b.reshape(1, d))


def _pad_rows(w, rows):
    return jnp.pad(w, ((0, rows - w.shape[0]), (0, 0)))


def _pad_cols(w, cols):
    return jnp.pad(w, ((0, 0), (0, cols - w.shape[1])))


def _layer(x2, c8, bsz, seq, alpha, w_ada, b_ada, w_in, mu_shift, w0, w_decay_up, a0, w_iclr_up,
           w_glora_up, k_k, k_a, r_k, lnx_g, lnx_b, w_rwkv_out, w_pool, pool_scale, w_out, ln1_g,
           ln1_b, w_router, b_router, w_gu, b_gu, w_down, b_down, ln2_g, ln2_b):
    m, d = x2.shape
    c = w0.shape[0]
    n_dec, n_icl, n_gate = w_decay_up.shape[0], w_iclr_up.shape[0], w_glora_up.shape[0]
    pw = w_pool.shape[0] * w_pool.shape[1]
    n_exp = w_router.shape[1]
    assert n_dec <= LANES and n_icl <= LANES and n_gate <= 2 * LANES and d == 2 * c and pw == c

    mod = _adaln(c8, w_ada, b_ada)[:bsz]
    sh1, sc1, g1, sh2, sc2, g2 = [t.reshape(bsz, 1, d) for t in jnp.split(mod, 6, axis=-1)]

    o = 3 * c
    segs = [(o, n_dec, LANES), (o + n_dec, n_icl, LANES), (o + n_dec + n_icl, n_gate, 2 * LANES)]
    n_shift = o + n_dec + n_icl + n_gate
    zpad = 2 * d - 3 * c - 4 * LANES
    w_shift = jnp.concatenate([w_in[:, :o]] + [_pad_cols(w_in[:, s:s + n], wd) for s, n, wd in segs]
                              + [jnp.zeros((d, zpad), F32)], axis=1)
    mu = jnp.concatenate([mu_shift[:o]] + [jnp.pad(mu_shift[s:s + n], (0, wd - n)) for s, n, wd in segs]
                         + [jnp.zeros((zpad,), F32)]).reshape(1, -1)
    w_cat = jnp.concatenate([w_in[:, n_shift + pw:], w_shift, w_in[:, n_shift:n_shift + pw]],
                            axis=1).astype(BF16)
    p = _inproj(x2, sc1, sh1, w_cat, seq)

    ones_bd = jnp.kron(jnp.eye(LANES // HEAD_SIZE, dtype=F32), jnp.ones((HEAD_SIZE, HEAD_SIZE), F32)).astype(BF16)
    r, lw, kmod, v, kkn, bvec, g = _prep(
        p, mu, w0, a0, k_k, k_a, _pad_rows(w_decay_up, LANES).astype(BF16),
        _pad_rows(w_iclr_up, LANES).astype(BF16), _pad_rows(w_glora_up, 2 * LANES).astype(BF16),
        ones_bd, seq, c, shift_blk=1)
    y = _wkv(r, lw, kmod, v, kkn, bvec, bsz, seq, c)

    w_r = _pad_cols(w_router, LANES).astype(BF16)
    b_r = jnp.pad(b_router, (0, LANES - n_exp))
    x1, h2, logits = _post(y, r, kmod, v, g, p, x2, g1, sc2, sh2, r_k.reshape(-1), lnx_g, lnx_b, ones_bd,
                           w_rwkv_out.astype(BF16), w_pool.astype(BF16), pool_scale, w_out.astype(BF16),
                           ln1_g, ln1_b, w_r, b_r, seq, alpha, gate_blk=0, pool_blk=(4 * d) // pw)

    idx, prob, rank, cnt = _route(logits, n_exp)
    counts = cnt[0, :n_exp].astype(jnp.int32)
    padded = (counts + MOE_SUB - 1) // MOE_SUB * MOE_SUB
    pad_end = jnp.cumsum(padded)
    pad_start = pad_end - padded
    mk = m * TOP_K
    n_rows = (mk // MOE_SUB + n_exp) * MOE_SUB
    dest = (pad_start[idx[:, :TOP_K]] + rank[:, :TOP_K]).reshape(mk).astype(jnp.int32)
    src_tok = (jnp.arange(n_rows, dtype=jnp.int32) % m).at[dest].set(
        jnp.arange(mk, dtype=jnp.int32) // TOP_K)
    n_items = mk // MOE_RMAX + n_exp
    per_e = (padded + MOE_RMAX - 1) // MOE_RMAX
    it_end = jnp.cumsum(per_e)
    it_start = it_end - per_e
    ids = jnp.arange(n_items, dtype=jnp.int32)
    item_e = jnp.minimum(jnp.searchsorted(it_end, ids, side='right'), n_exp - 1).astype(jnp.int32)
    local = ids - it_start[item_e]
    used = ids < it_end[-1]
    item_start = jnp.where(used, pad_start[item_e] + local * MOE_RMAX, 0).astype(jnp.int32)
    item_nsub = jnp.where(used, jnp.clip((padded[item_e] - local * MOE_RMAX) // MOE_SUB, 0,
                                         MOE_RMAX // MOE_SUB), 0).astype(jnp.int32)

    x_buf = _dispatch(src_tok, h2, n_rows)
    y_buf = _experts(item_e, item_start, item_nsub, pad_end[-1:].astype(jnp.int32), x_buf,
                     w_gu, b_gu, w_down, b_down, n_rows)
    return _combine(dest, y_buf, prob, x1, g2, ln2_g, ln2_b, seq, alpha)


def kernel(x, c, w_ada, b_ada, w_in, mu_shift, w0, w_decay_up, a0, w_iclr_up, w_glora_up, k_k, k_a, r_k, lnx_g, lnx_b, w_rwkv_out, w_pool, pool_scale, w_out, ln1_g, ln1_b, w_router, b_router, w_gu, b_gu, w_down, b_down, ln2_g, ln2_b):
    bsz, seq, d = x.shape
    depth = w_ada.shape[0]
    alpha = float((2 * depth) ** 0.25)
    c8 = jnp.pad(c, ((0, 8 - bsz), (0, 0)))
    x2 = x.reshape(bsz * seq, d)
    weights = (w_ada, b_ada, w_in, mu_shift, w0, w_decay_up, a0, w_iclr_up, w_glora_up, k_k, k_a, r_k,
               lnx_g, lnx_b, w_rwkv_out, w_pool, pool_scale, w_out, ln1_g, ln1_b, w_router, b_router,
               w_gu, b_gu, w_down, b_down, ln2_g, ln2_b)
    for l in range(depth):
        x2 = _layer(x2, c8, bsz, seq, alpha, *[w[l] for w in weights])
    return x2.reshape(bsz, seq, d)
```

```python
import functools

import jax
import jax.numpy as jnp
from jax import lax
from jax.experimental import pallas as pl
from jax.experimental.pallas import tpu as pltpu

F32 = jnp.float32
BF16 = jnp.bfloat16

HEAD_SIZE = 64
LANES = 128
TOP_K = 4
LN_EPS = 1e-5
GN_EPS = 64e-5
POOL_WINDOWS = (2, 4, 8, 16)
SWIGLU_LIMIT = 7.0
SWIGLU_ALPHA = 1.702
WKV_CHUNK = 64
MOE_SUB = 256
MOE_RMAX = 1280
MOE_TF = 512
VMEM_LIMIT = 54 * 1024 * 1024

_NT = (((1,), (1,)), ((), ()))
_TN = (((0,), (0,)), ((), ()))


def _cparams(*sem):
    return pltpu.CompilerParams(dimension_semantics=sem, vmem_limit_bytes=VMEM_LIMIT)


def _sigmoid(z):
    return 0.5 * jnp.tanh(0.5 * z) + 0.5


def _dot(a, b):
    return jnp.dot(a, b, preferred_element_type=F32)


def _split_dot(xv, w):
    hi = xv.astype(BF16)
    lo = (xv - hi.astype(F32)).astype(BF16)
    return _dot(hi, w) + _dot(lo, w)


def _head_sum(xv, ones_bd):
    outs = []
    for p in range(xv.shape[1] // LANES):
        outs.append(_split_dot(xv[:, p * LANES:(p + 1) * LANES], ones_bd))
    return jnp.concatenate(outs, axis=1)


def _adaln_kernel(c_ref, w_ref, b_ref, o_ref):
    c = c_ref[...]
    cond = c * _sigmoid(c)
    o_ref[...] = _dot(cond.astype(BF16), w_ref[...].astype(BF16)) + b_ref[...]


def _adaln(c8, w, b):
    d, n = w.shape
    tn = 1024
    return pl.pallas_call(
        _adaln_kernel,
        grid=(n // tn,),
        in_specs=[pl.BlockSpec((8, d), lambda j: (0, 0)),
                  pl.BlockSpec((d, tn), lambda j: (0, j)),
                  pl.BlockSpec((1, tn), lambda j: (0, j))],
        out_specs=pl.BlockSpec((8, tn), lambda j: (0, j)),
        out_shape=jax.ShapeDtypeStruct((8, n), F32),
        compiler_params=_cparams("arbitrary"),
        name="adaln",
    )(c8, w, b.reshape(1, n))


def _inproj_kernel(x_ref, sc_ref, sh_ref, w_ref, o_ref, h_scr):
    @pl.when(pl.program_id(1) == 0)
    def _():
        h_scr[...] = (x_ref[...] * (1.0 + sc_ref[0]) + sh_ref[0]).astype(BF16)

    o_ref[...] = _dot(h_scr[...], w_ref[...])


def _inproj(x2, sc, sh, w, seq):
    m, d = x2.shape
    n = w.shape[1]
    tm, tn = 1024, 1024
    tpb = seq // tm
    return pl.pallas_call(
        _inproj_kernel,
        grid=(m // tm, n // tn),
        in_specs=[pl.BlockSpec((tm, d), lambda i, j: (i, 0)),
                  pl.BlockSpec((1, 1, d), lambda i, j: (i // tpb, 0, 0)),
                  pl.BlockSpec((1, 1, d), lambda i, j: (i // tpb, 0, 0)),
                  pl.BlockSpec((d, tn), lambda i, j: (0, j))],
        out_specs=pl.BlockSpec((tm, tn), lambda i, j: (i, j)),
        out_shape=jax.ShapeDtypeStruct((m, n), F32),
        scratch_shapes=[pltpu.VMEM((tm, d), BF16)],
        compiler_params=_cparams("arbitrary", "arbitrary"),
        name="inproj",
    )(x2, sc, sh, w)


def _prep_kernel(p_ref, prev_ref, mu_ref, w0_ref, a0_ref, kk_ref, ka_ref, wd_ref, wa_ref, wg_ref,
                 ones_ref, r_out, lw_out, k_out, v_out, kkn_out, b_out, g_out, *, tpb, c):
    cur = p_ref[...]
    first = (pl.program_id(0) % tpb) == 0
    prev_row = jnp.where(first, 0.0, prev_ref[7:8, :])
    shifted = pltpu.roll(cur, 1, axis=0)
    row0 = lax.broadcasted_iota(jnp.int32, cur.shape, 0) == 0
    shifted = jnp.where(row0, prev_row, shifted)
    ps = cur + (shifted - cur) * mu_ref[...]
    r = ps[:, 0:c]
    k = ps[:, c:2 * c]
    v = ps[:, 2 * c:3 * c]
    o = 3 * c
    wdp = ps[:, o:o + LANES]
    adp = ps[:, o + LANES:o + 2 * LANES]
    gdp = ps[:, o + 2 * LANES:o + 4 * LANES]
    z = w0_ref[...] + _dot(jnp.tanh(wdp).astype(BF16), wd_ref[...])
    lw = -jnp.exp(-0.5) * _sigmoid(z)
    iclr = _sigmoid(a0_ref[...] + _dot(adp.astype(BF16), wa_ref[...]))
    g = _dot(_sigmoid(gdp).astype(BF16), wg_ref[...])
    kk0 = k * kk_ref[...]
    ss = _head_sum(kk0 * kk0, ones_ref[...])
    kkn = kk0 / jnp.maximum(jnp.sqrt(ss), 1e-12)
    kmod = k * (1.0 + (iclr - 1.0) * ka_ref[...])
    r_out[...] = r
    lw_out[...] = lw
    k_out[...] = kmod
    v_out[...] = v
    kkn_out[...] = kkn
    b_out[...] = kkn * iclr
    g_out[...] = g


def _prep(p, mu, w0, a0, k_k, k_a, wd, wa, wg, ones_bd, seq, c, shift_blk):
    m = p.shape[0]
    ns = mu.shape[1]
    tm = 256
    tpb = seq // tm
    row = lambda a: a.reshape(1, -1)
    full = lambda a: pl.BlockSpec(a.shape, lambda i: (0,) * a.ndim)
    out = jax.ShapeDtypeStruct((m, c), F32)
    ospec = pl.BlockSpec((tm, c), lambda i: (i, 0))
    args = (p, p, mu, row(w0), row(a0), row(k_k), row(k_a), wd, wa, wg, ones_bd)
    in_specs = [pl.BlockSpec((tm, ns), lambda i: (i, shift_blk)),
                pl.BlockSpec((8, ns), lambda i: (jnp.maximum(i * (tm // 8) - 1, 0), shift_blk))]
    in_specs += [full(a) for a in args[2:]]
    return pl.pallas_call(
        functools.partial(_prep_kernel, tpb=tpb, c=c),
        grid=(m // tm,),
        in_specs=in_specs,
        out_specs=[ospec] * 7,
        out_shape=[out] * 7,
        compiler_params=_cparams("arbitrary"),
        name="rwkv_prep",
    )(*args)


def _wkv_kernel(r_ref, lw_ref, k_ref, v_ref, kk_ref, b_ref, y_ref, s_scr, *, chunk, c):
    L = chunk

    @pl.when(pl.program_id(1) == 0)
    def _():
        s_scr[...] = jnp.zeros_like(s_scr)

    ri = lax.broadcasted_iota(jnp.int32, (L, L), 0)
    ci = lax.broadcasted_iota(jnp.int32, (L, L), 1)
    tri = (ri >= ci).astype(BF16)
    lw = lw_ref[...]
    cs = _split_dot_lhs(tri, lw)
    cs_last = cs[L - 1:L, :]
    g_in = jnp.exp(cs)
    g_ex = jnp.exp(cs - lw)
    g_inv = jnp.exp(-cs)
    g_tail = jnp.exp(cs_last - cs)
    g_last = jnp.exp(cs_last)
    kk = kk_ref[...]
    bb = b_ref[...]
    kx = k_ref[...]
    vv = v_ref[...]
    a_t = -kk * g_ex
    r_t = r_ref[...] * g_in
    b_t = bb * g_inv
    k_t = kx * g_inv
    b_p = bb * g_tail
    k_p = kx * g_tail

    r2 = lax.broadcasted_iota(jnp.int32, (2 * L, 2 * L), 0)
    c2 = lax.broadcasted_iota(jnp.int32, (2 * L, 2 * L), 1)
    same = (r2 >= L) == (c2 >= L)
    m_strict = same & (r2 > c2)
    m_incl = same & (r2 >= c2)
    lane = lax.broadcasted_iota(jnp.int32, (L, LANES), 1)
    low = lane < HEAD_SIZE

    def stack(zv):
        return jnp.concatenate([jnp.where(low, zv, 0.0), jnp.where(low, 0.0, zv)], axis=0)

    pairs = range(c // LANES)
    sls = [slice(p * LANES, (p + 1) * LANES) for p in pairs]
    a_s = [stack(a_t[:, sl]) for sl in sls]
    r_s = [stack(r_t[:, sl]) for sl in sls]
    v_s = [stack(vv[:, sl]) for sl in sls]
    ar = [jnp.concatenate([a_s[p], r_s[p]], axis=0).astype(BF16) for p in pairs]
    qb = [lax.dot_general(ar[p], stack(b_t[:, sls[p]]).astype(BF16), _NT, preferred_element_type=F32)
          for p in pairs]
    qk = [lax.dot_general(ar[p], stack(k_t[:, sls[p]]).astype(BF16), _NT, preferred_element_type=F32)
          for p in pairs]
    a_ab = [jnp.where(m_strict, q[:2 * L], 0.0) for q in qb]
    a_rb = [jnp.where(m_incl, q[2 * L:], 0.0) for q in qb]
    a_ak = [jnp.where(m_strict, q[:2 * L], 0.0) for q in qk]
    a_rk = [jnp.where(m_incl, q[2 * L:], 0.0) for q in qk]
    v_b = [x.astype(BF16) for x in v_s]
    z = [jnp.concatenate([_dot(a_ak[p].astype(BF16), v_b[p]), a_s[p]], axis=1) for p in pairs]
    amat = a_ab
    n_sq = max(L.bit_length() - 1, 1)
    for it in range(n_sq):
        ab = [x.astype(BF16) for x in amat]
        z = [z[p] + _dot(ab[p], z[p].astype(BF16)) for p in pairs]
        if it + 1 < n_sq:
            amat = [_dot(x, x) for x in ab]
    s_old = [s_scr[p] for p in pairs]
    s_b = [x.astype(BF16) for x in s_old]
    u_s = [_dot(z[p][:, LANES:].astype(BF16), s_b[p]) + z[p][:, :LANES] for p in pairs]
    uv = [jnp.concatenate([u_s[p], v_s[p]], axis=0).astype(BF16) for p in pairs]
    y_s = [_dot(r_s[p].astype(BF16), s_b[p])
           + _dot(jnp.concatenate([a_rb[p], a_rk[p]], axis=1).astype(BF16), uv[p]) for p in pairs]
    for p in pairs:
        y_ref[:, sls[p]] = y_s[p][:L] + y_s[p][L:]
    upd = [lax.dot_general(jnp.concatenate([stack(b_p[:, sls[p]]), stack(k_p[:, sls[p]])], axis=0).astype(BF16),
                           uv[p], _TN, preferred_element_type=F32) for p in pairs]
    for p in pairs:
        g_col = jnp.transpose(jnp.broadcast_to(g_last[:, sls[p]], (LANES, LANES)))
        s_scr[p] = s_old[p] * g_col + upd[p]


def _split_dot_lhs(w, xv):
    hi = xv.astype(BF16)
    lo = (xv - hi.astype(F32)).astype(BF16)
    return _dot(w, hi) + _dot(w, lo)


def _wkv(r, lw, k, v, kk, b, bsz, seq, c):
    L = WKV_CHUNK
    nc = seq // L
    spec = pl.BlockSpec((L, c), lambda bi, ci: (bi * nc + ci, 0))
    return pl.pallas_call(
        functools.partial(_wkv_kernel, chunk=L, c=c),
        grid=(bsz, nc),
        in_specs=[spec] * 6,
        out_specs=spec,
        out_shape=jax.ShapeDtypeStruct((bsz * seq, c), F32),
        scratch_shapes=[pltpu.VMEM((c // LANES, LANES, LANES), F32)],
        compiler_params=_cparams("arbitrary", "arbitrary"),
        name="wkv7",
    )(r, lw, k, v, kk, b)


def _post_kernel(y_ref, r_ref, k_ref, v_ref, g_ref, gate_ref, pool_ref, halo_ref, x_ref,
                 g1_ref, sc2_ref, sh2_ref, rk_ref, lng_ref, lnb_ref, ones_ref, wro_ref, wp_ref,
                 psc_ref, wo_ref, l1g_ref, l1b_ref, wr_ref, br_ref,
                 x1_out, h2_out, lg_out, pool_scr, *, tpb, alpha, d, tm):
    ones_bd = ones_ref[...]
    inv_n = 1.0 / HEAD_SIZE
    y = y_ref[...]
    mu = _head_sum(y, ones_bd) * inv_n
    dy = y - mu
    var = _head_sum(dy * dy, ones_bd) * inv_n
    yn = dy * lax.rsqrt(var + GN_EPS) * lng_ref[...] + lnb_ref[...]
    vv = v_ref[...]
    bonus = _head_sum(r_ref[...] * k_ref[...] * rk_ref[...], ones_bd) * vv
    y_rwkv = ((yn + bonus) * g_ref[...]).astype(BF16)
    branch_a = _dot(y_rwkv, wro_ref[...])

    first = (pl.program_id(0) % tpb) == 0
    halo = 16
    pool_scr[0:halo, :] = jnp.where(first, 0.0, halo_ref[...])
    pool_scr[halo:, :] = pool_ref[...]
    pos = (pl.program_id(0) % tpb) * tm + lax.broadcasted_iota(jnp.int32, (tm, 1), 0)
    gw = pool_ref.shape[1] // len(POOL_WINDOWS)
    dg = d // len(POOL_WINDOWS)
    outs = []
    for gi, win in enumerate(POOL_WINDOWS):
        cols = slice(gi * gw, (gi + 1) * gw)
        u = pool_scr[halo:, cols]
        wsum = u
        for s in range(1, win):
            wsum = wsum + pool_scr[halo - s:halo - s + tm, cols]
        cnt = jnp.minimum(pos + 1, win).astype(F32)
        dd = (wsum / cnt - u).astype(BF16)
        outs.append(_dot(dd, wp_ref[gi]))
    branch_b = jnp.concatenate(outs, axis=1) * psc_ref[...]

    gates = gate_ref[...]
    mixin = _sigmoid(gates[:, :d]) * branch_a + _sigmoid(gates[:, d:]) * branch_b
    mix = _dot(mixin.astype(BF16), wo_ref[...])
    xr = alpha * x_ref[...] + g1_ref[0] * mix
    m1 = jnp.mean(xr, axis=-1, keepdims=True)
    xc = xr - m1
    v1 = jnp.mean(xc * xc, axis=-1, keepdims=True)
    x1 = xc * lax.rsqrt(v1 + LN_EPS) * l1g_ref[...] + l1b_ref[...]
    x1_out[...] = x1
    h2 = x1 * (1.0 + sc2_ref[0]) + sh2_ref[0]
    h2_b = h2.astype(BF16)
    h2_r = h2_b.astype(F32)
    lo = lax.shift_right_logical(lax.bitcast_convert_type(h2_r[:, :d // 2], jnp.int32), 16)
    hi = lax.bitcast_convert_type(h2_r[:, d // 2:], jnp.int32) & jnp.int32(-65536)
    h2_out[...] = lo | hi
    lg_out[...] = _dot(h2_b, wr_ref[...]) + br_ref[...]


def _post(y, r, k, v, g, p, x2, g1, sc2, sh2, r_k, lnx_g, lnx_b, ones_bd, w_ro, w_pool, pool_scale,
          w_out, ln1_g, ln1_b, w_r, b_r, seq, alpha, gate_blk, pool_blk):
    m, d = x2.shape
    c = y.shape[1]
    pw = w_pool.shape[0] * w_pool.shape[1]
    tm = 256
    tpb = seq // tm
    row = lambda a: a.reshape(1, -1)
    const = lambda a: pl.BlockSpec(a.shape, lambda i: (0,) * a.ndim, pipeline_mode=pl.Buffered(1))
    tile = lambda w: pl.BlockSpec((tm, w), lambda i: (i, 0))
    bvec = pl.BlockSpec((1, 1, d), lambda i: (i // tpb, 0, 0))
    consts = (row(r_k), row(lnx_g), row(lnx_b), ones_bd, w_ro, w_pool, row(pool_scale), w_out,
              row(ln1_g), row(ln1_b), w_r, row(b_r))
    in_specs = [tile(c)] * 5 + [
        pl.BlockSpec((tm, 2 * d), lambda i: (i, gate_blk)),
        pl.BlockSpec((tm, pw), lambda i: (i, pool_blk)),
        pl.BlockSpec((16, pw), lambda i: (jnp.maximum(i * (tm // 16) - 1, 0), pool_blk)),
        tile(d), bvec, bvec, bvec] + [const(a) for a in consts]
    return pl.pallas_call(
        functools.partial(_post_kernel, tpb=tpb, alpha=alpha, d=d, tm=tm),
        grid=(m // tm,),
        in_specs=in_specs,
        out_specs=[tile(d), tile(d // 2), tile(LANES)],
        out_shape=[jax.ShapeDtypeStruct((m, d), F32), jax.ShapeDtypeStruct((m, d // 2), jnp.int32),
                   jax.ShapeDtypeStruct((m, LANES), F32)],
        scratch_shapes=[pltpu.VMEM((tm + 16, pw), F32)],
        compiler_params=_cparams("arbitrary"),
        name="merge_ln1",
    )(y, r, k, v, g, p, p, p, x2, g1, sc2, sh2, *consts)


def _route_kernel(lg_ref, idx_out, prob_out, rank_out, cnt_out, carry, *, n_exp, tm):
    @pl.when(pl.program_id(0) == 0)
    def _():
        carry[...] = jnp.zeros_like(carry)

    lane = lax.broadcasted_iota(jnp.int32, (tm, LANES), 1)
    lane_f = lane.astype(F32)
    neg = jnp.float32(-jnp.inf)
    work = jnp.where(lane < n_exp, lg_ref[...], neg)
    vals, sels = [], []
    idx_acc = jnp.zeros((tm, LANES), jnp.int32)
    hot = jnp.zeros((tm, LANES), F32)
    for kk in range(TOP_K):
        mx = jnp.max(work, axis=-1, keepdims=True)
        idx = jnp.min(jnp.where(work == mx, lane_f, float(LANES)), axis=-1, keepdims=True).astype(jnp.int32)
        sel = lane == idx
        work = jnp.where(sel, neg, work)
        vals.append(mx)
        sels.append(sel)
        idx_acc = jnp.where(lane == kk, idx, idx_acc)
        hot = hot + sel.astype(F32)
    es = [jnp.exp(vv - vals[0]) for vv in vals]
    den = es[0] + es[1] + es[2] + es[3]
    prob = jnp.zeros((tm, LANES), F32)
    for kk in range(TOP_K):
        prob = jnp.where(lane == kk, es[kk] / den, prob)
    ri = lax.broadcasted_iota(jnp.int32, (tm, tm), 0)
    ci = lax.broadcasted_iota(jnp.int32, (tm, tm), 1)
    before = _dot((ri > ci).astype(BF16), hot.astype(BF16)) + carry[0:1, :]
    rank = jnp.zeros((tm, LANES), jnp.int32)
    for kk in range(TOP_K):
        rk = jnp.sum(jnp.where(sels[kk], before, 0.0), axis=-1, keepdims=True).astype(jnp.int32)
        rank = jnp.where(lane == kk, rk, rank)
    carry[...] = carry[...] + jnp.sum(hot, axis=0, keepdims=True)
    idx_out[...] = idx_acc
    prob_out[...] = prob
    rank_out[...] = rank
    cnt_out[...] = carry[...]


def _route(logits, n_exp):
    m = logits.shape[0]
    tm = 512
    tile = pl.BlockSpec((tm, LANES), lambda i: (i, 0))
    return pl.pallas_call(
        functools.partial(_route_kernel, n_exp=n_exp, tm=tm),
        grid=(m // tm,),
        in_specs=[tile],
        out_specs=[tile, tile, tile, pl.BlockSpec((8, LANES), lambda i: (0, 0))],
        out_shape=[jax.ShapeDtypeStruct((m, LANES), jnp.int32), jax.ShapeDtypeStruct((m, LANES), F32),
                   jax.ShapeDtypeStruct((m, LANES), jnp.int32), jax.ShapeDtypeStruct((8, LANES), F32)],
        scratch_shapes=[pltpu.VMEM((8, LANES), F32)],
        compiler_params=_cparams("arbitrary"),
        name="route",
    )(logits)


def _row_copy(src_hbm, dst, src_row, dst_row, sem):
    return pltpu.make_async_copy(src_hbm.at[pl.ds(src_row, 1), :], dst.at[pl.ds(dst_row, 1), :], sem)


def _dispatch_kernel(dest_ref, pend_ref, h_ref, x_hbm, stage, zero, zsem, sems, *, tb, n_exp, sub, n_rows):
    i = pl.program_id(0)
    slot = i % 2
    stage[slot] = h_ref[...]

    def zero_copy(row0):
        return pltpu.make_async_copy(zero, x_hbm.at[pl.ds(pl.multiple_of(row0, sub), sub), :], zsem)

    def zero_fill(fn):
        for e in range(n_exp):
            lo_e = pend_ref[e - 1] if e > 0 else 0

            @pl.when(pend_ref[e] > lo_e)
            def _():
                fn(zero_copy(pend_ref[e] - sub))

        def body(blk, carry):
            fn(zero_copy(blk * sub))
            return carry

        lax.fori_loop(pend_ref[n_exp - 1] // sub, n_rows // sub, body, 0)

    @pl.when(i == 0)
    def _():
        zero[...] = jnp.zeros_like(zero)
        zero_fill(lambda cp: cp.start())
        zero_fill(lambda cp: cp.wait())

    def issue(u, carry):
        t = i * tb + u
        for kk in range(TOP_K):
            _row_copy(stage.at[slot], x_hbm, u, dest_ref[t * TOP_K + kk], sems.at[slot]).start(priority=kk % 2)
        return carry

    lax.fori_loop(0, tb, issue, 0, unroll=2)

    def drain_on(slot_):
        def body(u, carry):
            for kk in range(TOP_K):
                _row_copy(stage.at[slot_], x_hbm, 0, 0, sems.at[slot_]).wait()
            return carry

        lax.fori_loop(0, tb, body, 0, unroll=2)

    @pl.when(i > 0)
    def _():
        drain_on(1 - slot)

    @pl.when(i == pl.num_programs(0) - 1)
    def _():
        drain_on(slot)


def _dispatch(dest, pad_end, h2p, n_rows):
    m, w = h2p.shape
    tb = 256
    n_exp = pad_end.shape[0]
    return pl.pallas_call(
        functools.partial(_dispatch_kernel, tb=tb, n_exp=n_exp, sub=MOE_SUB, n_rows=n_rows),
        grid_spec=pltpu.PrefetchScalarGridSpec(
            num_scalar_prefetch=2,
            grid=(m // tb,),
            in_specs=[pl.BlockSpec((tb, w), lambda i, dst, pend: (i, 0))],
            out_specs=pl.BlockSpec(memory_space=pl.ANY),
            scratch_shapes=[pltpu.VMEM((2, tb, w), jnp.int32), pltpu.VMEM((MOE_SUB, w), jnp.int32),
                            pltpu.SemaphoreType.DMA(()), pltpu.SemaphoreType.DMA((2,))]),
        out_shape=jax.ShapeDtypeStruct((n_rows, w), jnp.int32),
        compiler_params=_cparams("arbitrary"),
        name="dispatch",
    )(dest, pad_end, h2p)


def _roll_lanes(v, shift):
    return jnp.concatenate([pltpu.roll(v[:, g * LANES:(g + 1) * LANES], shift, axis=1)
                            for g in range(v.shape[1] // LANES)], axis=1)


def _expert_kernel(e_ref, st_ref, ns_ref, tot_ref, x_hbm, wgu_ref, bgu_ref, wd_ref, bd_ref,
                   y_hbm, x_ref, acc, xw, wd_i, wd_b, act, sem, x_sems, *, n_ff, rmax, sub, n_rows):
    i = pl.program_id(0)
    j = pl.program_id(1)
    nsub = ns_ref[i]
    n_sub_max = rmax // sub
    tfc = wd_ref.shape[1]
    half = tfc // 2

    def out_copy(row0, s):
        return pltpu.make_async_copy(acc.at[pl.ds(s * sub, sub), :],
                                     y_hbm.at[pl.ds(pl.multiple_of(row0, sub), sub), :], sem)

    def for_valid_subs(item, fn):
        for s in range(n_sub_max):
            @pl.when(s < ns_ref[item])
            def _():
                fn(out_copy(st_ref[item] + s * sub, s))

    half_d = x_ref.shape[1] // 2

    def fetch(s):
        row0 = pl.multiple_of(st_ref[i] + s * sub, sub)
        return pltpu.make_async_copy(x_hbm.at[pl.ds(row0, sub), :], xw.at[s % 2], x_sems.at[s % 2])

    @pl.when(j == 0)
    def _():
        for s in range(2):
            @pl.when(s < nsub)
            def _():
                fetch(s).start()

    @pl.when(nsub > 0)
    def _():
        for g in range(wd_b.shape[1] // LANES):
            cols = slice(g * LANES, (g + 1) * LANES)
            wd_i[g % 2, pl.ds(0, half, stride=2), :] = wd_ref[0, 0:half, cols]
            wd_i[g % 2, pl.ds(1, half, stride=2), :] = wd_ref[0, half:tfc, cols]
            wd_b[:, cols] = wd_i[g % 2].astype(BF16)

    @pl.when(j == 0)
    def _():
        for s in range(n_sub_max):
            @pl.when(s < nsub)
            def _():
                fetch(s).wait()
                words = xw[s % 2]
                rows = slice(s * sub, (s + 1) * sub)
                x_ref[rows, 0:half_d] = lax.bitcast_convert_type(words << 16, F32).astype(BF16)
                x_ref[rows, half_d:] = lax.bitcast_convert_type(words & jnp.int32(-65536), F32).astype(BF16)

                @pl.when(s + 2 < nsub)
                def _():
                    fetch(s + 2).start()

    cw = 2 * LANES

    def paired(gu):
        gate = jnp.minimum(gu, SWIGLU_LIMIT)
        glu = gate * _sigmoid(gate * SWIGLU_ALPHA)
        up1 = jnp.clip(gu, -SWIGLU_LIMIT, SWIGLU_LIMIT) + 1.0
        return glu * _roll_lanes(up1, LANES - 1)

    def activate(m_rows):
        xm = x_ref[0:m_rows, :]
        even = (lax.broadcasted_iota(jnp.int32, (m_rows, cw), 1) % 2) == 0
        for c in range(tfc // cw):
            lo = slice(c * cw, (c + 1) * cw)
            hi = slice(tfc + c * cw, tfc + (c + 1) * cw)
            pa = paired(_dot(xm, wgu_ref[0, :, lo].astype(BF16)) + bgu_ref[0, :, lo])
            pb = paired(_dot(xm, wgu_ref[0, :, hi].astype(BF16)) + bgu_ref[0, :, hi])
            act[0:m_rows, lo] = jnp.where(even, pa, _roll_lanes(pb, 1)).astype(BF16)

    def project(m_rows):
        acc[0:m_rows, :] += _dot(act[0:m_rows, :], wd_b[...])

    def per_row_count(fn):
        for n in range(1, n_sub_max + 1):
            @pl.when(nsub == n)
            def _():
                fn(n * sub)

    per_row_count(activate)

    @pl.when(j == 0)
    def _():
        @pl.when(i > 0)
        def _():
            for_valid_subs(i - 1, lambda cp: cp.wait())

        acc[...] = jnp.broadcast_to(bd_ref[0], acc.shape)

    per_row_count(project)

    @pl.when(j == n_ff - 1)
    def _():
        for_valid_subs(i, lambda cp: cp.start())

        @pl.when(i == pl.num_programs(0) - 1)
        def _():
            for_valid_subs(i, lambda cp: cp.wait())
            acc[0:sub, :] = jnp.zeros((sub, acc.shape[1]), F32)
            first = tot_ref[0] // sub

            def fill_start(blk, carry):
                out_copy(blk * sub, 0).start()
                return carry

            def fill_wait(blk, carry):
                out_copy(blk * sub, 0).wait()
                return carry

            lax.fori_loop(first, n_rows // sub, fill_start, 0)
            lax.fori_loop(first, n_rows // sub, fill_wait, 0)


def _experts(item_e, item_start, item_nsub, total_rows, x_buf, w_gu, b_gu, w_down, b_down, n_rows):
    n_items = item_e.shape[0]
    n_exp, dff, d = w_down.shape
    tf = MOE_TF
    n_ff = dff // tf
    rmax = MOE_RMAX

    def jj(i, j, ns):
        return jnp.where(ns[i] > 0, j, n_ff - 1)

    in_specs = [
        pl.BlockSpec(memory_space=pl.ANY),
        pl.BlockSpec((1, d, 2 * tf), lambda i, j, e, st, ns, tot: (e[i], 0, jj(i, j, ns))),
        pl.BlockSpec((1, 1, 2 * tf), lambda i, j, e, st, ns, tot: (e[i], 0, jj(i, j, ns))),
        pl.BlockSpec((1, tf, d), lambda i, j, e, st, ns, tot: (e[i], jj(i, j, ns), 0)),
        pl.BlockSpec((1, 1, d), lambda i, j, e, st, ns, tot: (e[i], 0, 0)),
    ]
    return pl.pallas_call(
        functools.partial(_expert_kernel, n_ff=n_ff, rmax=rmax, sub=MOE_SUB, n_rows=n_rows),
        grid_spec=pltpu.PrefetchScalarGridSpec(
            num_scalar_prefetch=4,
            grid=(n_items, n_ff),
            in_specs=in_specs,
            out_specs=pl.BlockSpec(memory_space=pl.ANY),
            scratch_shapes=[pltpu.VMEM((rmax, d), BF16), pltpu.VMEM((rmax, d), F32),
                            pltpu.VMEM((2, MOE_SUB, d // 2), jnp.int32), pltpu.VMEM((2, tf, LANES), F32),
                            pltpu.VMEM((tf, d), BF16), pltpu.VMEM((rmax, tf), BF16),
                            pltpu.SemaphoreType.DMA(()), pltpu.SemaphoreType.DMA((2,))]),
        out_shape=jax.ShapeDtypeStruct((n_rows, d), F32),
        compiler_params=_cparams("arbitrary", "arbitrary"),
        name="experts",
    )(item_e, item_start, item_nsub, total_rows, x_buf, w_gu, b_gu.reshape(n_exp, 1, 2 * dff),
      w_down, b_down.reshape(n_exp, 1, d))


def _combine_kernel(dest_ref, y_hbm, prob_ref, x1_ref, g2_ref, lg_ref, lb_ref, o_ref, buf, sems,
                    *, tm, alpha):
    i = pl.program_id(0)
    slot = i % 2

    def issue_block(blk, slot_):
        def body(t, carry):
            for kk in range(TOP_K):
                _row_copy(y_hbm, buf.at[slot_, kk], dest_ref[(blk * tm + t) * TOP_K + kk], t,
                          sems.at[slot_]).start(priority=kk % 2)
            return carry

        lax.fori_loop(0, tm, body, 0, unroll=2)

    @pl.when(i == 0)
    def _():
        issue_block(0, 0)

    @pl.when(i + 1 < pl.num_programs(0))
    def _():
        issue_block(i + 1, 1 - slot)

    def drain(t, carry):
        for kk in range(TOP_K):
            _row_copy(y_hbm, buf.at[slot, kk], 0, t, sems.at[slot]).wait()
        return carry

    lax.fori_loop(0, tm, drain, 0, unroll=2)
    prob = prob_ref[...]
    f = prob[:, 0:1] * buf[slot, 0]
    for kk in range(1, TOP_K):
        f = f + prob[:, kk:kk + 1] * buf[slot, kk]
    xr = alpha * x1_ref[...] + g2_ref[0] * f
    m2 = jnp.mean(xr, axis=-1, keepdims=True)
    xc = xr - m2
    v2 = jnp.mean(xc * xc, axis=-1, keepdims=True)
    o_ref[...] = xc * lax.rsqrt(v2 + LN_EPS) * lg_ref[...] + lb_ref[...]


def _combine(dest, y_buf, prob, x1, g2, ln_g, ln_b, seq, alpha):
    m, d = x1.shape
    tm = 128
    tpb = seq // tm
    return pl.pallas_call(
        functools.partial(_combine_kernel, tm=tm, alpha=alpha),
        grid_spec=pltpu.PrefetchScalarGridSpec(
            num_scalar_prefetch=1,
            grid=(m // tm,),
            in_specs=[pl.BlockSpec(memory_space=pl.ANY),
                      pl.BlockSpec((tm, LANES), lambda i, dst: (i, 0)),
                      pl.BlockSpec((tm, d), lambda i, dst: (i, 0)),
                      pl.BlockSpec((1, 1, d), lambda i, dst: (i // tpb, 0, 0)),
                      pl.BlockSpec((1, d), lambda i, dst: (0, 0)),
                      pl.BlockSpec((1, d), lambda i, dst: (0, 0))],
            out_specs=pl.BlockSpec((tm, d), lambda i, dst: (i, 0)),
            scratch_shapes=[pltpu.VMEM((2, TOP_K, tm, d), F32), pltpu.SemaphoreType.DMA((2,))]),
        out_shape=jax.ShapeDtypeStruct((m, d), F32),
        compiler_params=_cparams("arbitrary"),
        name="combine_ln2",
    )(dest, y_buf, prob, x1, g2, ln_g.reshape(1, d), ln_b.reshape(1, d))


def _pad_rows(w, rows):
    return jnp.pad(w, ((0, rows - w.shape[0]), (0, 0)))


def _pad_cols(w, cols):
    return jnp.pad(w, ((0, 0), (0, cols - w.shape[1])))


def _layer(x2, c8, bsz, seq, alpha, w_ada, b_ada, w_in, mu_shift, w0, w_decay_up, a0, w_iclr_up,
           w_glora_up, k_k, k_a, r_k, lnx_g, lnx_b, w_rwkv_out, w_pool, pool_scale, w_out, ln1_g,
           ln1_b, w_router, b_router, w_gu, b_gu, w_down, b_down, ln2_g, ln2_b):
    m, d = x2.shape
    c = w0.shape[0]
    n_dec, n_icl, n_gate = w_decay_up.shape[0], w_iclr_up.shape[0], w_glora_up.shape[0]
    pw = w_pool.shape[0] * w_pool.shape[1]
    n_exp = w_router.shape[1]
    assert n_dec <= LANES and n_icl <= LANES and n_gate <= 2 * LANES and d == 2 * c and pw == c

    mod = _adaln(c8, w_ada, b_ada)[:bsz]
    sh1, sc1, g1, sh2, sc2, g2 = [t.reshape(bsz, 1, d) for t in jnp.split(mod, 6, axis=-1)]

    o = 3 * c
    segs = [(o, n_dec, LANES), (o + n_dec, n_icl, LANES), (o + n_dec + n_icl, n_gate, 2 * LANES)]
    n_shift = o + n_dec + n_icl + n_gate
    zpad = 2 * d - 3 * c - 4 * LANES
    w_shift = jnp.concatenate([w_in[:, :o]] + [_pad_cols(w_in[:, s:s + n], wd) for s, n, wd in segs]
                              + [jnp.zeros((d, zpad), F32)], axis=1)
    mu = jnp.concatenate([mu_shift[:o]] + [jnp.pad(mu_shift[s:s + n], (0, wd - n)) for s, n, wd in segs]
                         + [jnp.zeros((zpad,), F32)]).reshape(1, -1)
    w_cat = jnp.concatenate([w_in[:, n_shift + pw:], w_shift, w_in[:, n_shift:n_shift + pw]],
                            axis=1).astype(BF16)
    p = _inproj(x2, sc1, sh1, w_cat, seq)

    ones_bd = jnp.kron(jnp.eye(LANES // HEAD_SIZE, dtype=F32), jnp.ones((HEAD_SIZE, HEAD_SIZE), F32)).astype(BF16)
    r, lw, kmod, v, kkn, bvec, g = _prep(
        p, mu, w0, a0, k_k, k_a, _pad_rows(w_decay_up, LANES).astype(BF16),
        _pad_rows(w_iclr_up, LANES).astype(BF16), _pad_rows(w_glora_up, 2 * LANES).astype(BF16),
        ones_bd, seq, c, shift_blk=1)
    y = _wkv(r, lw, kmod, v, kkn, bvec, bsz, seq, c)

    w_r = _pad_cols(w_router, LANES).astype(BF16)
    b_r = jnp.pad(b_router, (0, LANES - n_exp))
    x1, h2, logits = _post(y, r, kmod, v, g, p, x2, g1, sc2, sh2, r_k.reshape(-1), lnx_g, lnx_b, ones_bd,
                           w_rwkv_out.astype(BF16), w_pool.astype(BF16), pool_scale, w_out.astype(BF16),
                           ln1_g, ln1_b, w_r, b_r, seq, alpha, gate_blk=0, pool_blk=(4 * d) // pw)

    idx, prob, rank, cnt = _route(logits, n_exp)
    counts = cnt[0, :n_exp].astype(jnp.int32)
    padded = (counts + MOE_SUB - 1) // MOE_SUB * MOE_SUB
    pad_end = jnp.cumsum(padded)
    pad_start = pad_end - padded
    mk = m * TOP_K
    n_rows = (mk // MOE_SUB + n_exp) * MOE_SUB
    dest = (pad_start[idx[:, :TOP_K]] + rank[:, :TOP_K]).reshape(mk).astype(jnp.int32)
    n_items = mk // MOE_RMAX + n_exp
    per_e = (padded + MOE_RMAX - 1) // MOE_RMAX
    it_end = jnp.cumsum(per_e)
    it_start = it_end - per_e
    ids = jnp.arange(n_items, dtype=jnp.int32)
    item_e = jnp.minimum(jnp.searchsorted(it_end, ids, side='right'), n_exp - 1).astype(jnp.int32)
    local = ids - it_start[item_e]
    used = ids < it_end[-1]
    item_start = jnp.where(used, pad_start[item_e] + local * MOE_RMAX, 0).astype(jnp.int32)
    item_nsub = jnp.where(used, jnp.clip((padded[item_e] - local * MOE_RMAX) // MOE_SUB, 0,
                                         MOE_RMAX // MOE_SUB), 0).astype(jnp.int32)

    x_buf = _dispatch(dest, pad_end.astype(jnp.int32), h2, n_rows)
    y_buf = _experts(item_e, item_start, item_nsub, pad_end[-1:].astype(jnp.int32), x_buf,
                     w_gu, b_gu, w_down, b_down, n_rows)
    return _combine(dest, y_buf, prob, x1, g2, ln2_g, ln2_b, seq, alpha)


def kernel(x, c, w_ada, b_ada, w_in, mu_shift, w0, w_decay_up, a0, w_iclr_up, w_glora_up, k_k, k_a, r_k, lnx_g, lnx_b, w_rwkv_out, w_pool, pool_scale, w_out, ln1_g, ln1_b, w_router, b_router, w_gu, b_gu, w_down, b_down, ln2_g, ln2_b):
    bsz, seq, d = x.shape
    depth = w_ada.shape[0]
    alpha = float((2 * depth) ** 0.25)
    c8 = jnp.pad(c, ((0, 8 - bsz), (0, 0)))
    x2 = x.reshape(bsz * seq, d)
    weights = (w_ada, b_ada, w_in, mu_shift, w0, w_decay_up, a0, w_iclr_up, w_glora_up, k_k, k_a, r_k,
               lnx_g, lnx_b, w_rwkv_out, w_pool, pool_scale, w_out, ln1_g, ln1_b, w_router, b_router,
               w_gu, b_gu, w_down, b_down, ln2_g, ln2_b)
    for l in range(depth):
        x2 = _layer(x2, c8, bsz, seq, alpha, *[w[l] for w in weights])
    return x2.reshape(bsz, seq, d)
```

```python
import functools

import jax
import jax.numpy as jnp
from jax import lax
from jax.experimental import pallas as pl
from jax.experimental.pallas import tpu as pltpu

F32 = jnp.float32
BF16 = jnp.bfloat16

HEAD_SIZE = 64
LANES = 128
TOP_K = 4
LN_EPS = 1e-5
GN_EPS = 64e-5
POOL_WINDOWS = (2, 4, 8, 16)
SWIGLU_LIMIT = 7.0
SWIGLU_ALPHA = 1.702
WKV_CHUNK = 64
MOE_SUB = 256
MOE_RMAX = 1280
MOE_TF = 512
VMEM_LIMIT = 54 * 1024 * 1024

_NT = (((1,), (1,)), ((), ()))
_TN = (((0,), (0,)), ((), ()))


def _cparams(*sem):
    return pltpu.CompilerParams(dimension_semantics=sem, vmem_limit_bytes=VMEM_LIMIT)


def _sigmoid(z):
    return 0.5 * jnp.tanh(0.5 * z) + 0.5


def _dot(a, b):
    return jnp.dot(a, b, preferred_element_type=F32)


def _split_dot(xv, w):
    hi = xv.astype(BF16)
    lo = (xv - hi.astype(F32)).astype(BF16)
    return _dot(hi, w) + _dot(lo, w)


def _head_sum(xv, ones_bd):
    outs = []
    for p in range(xv.shape[1] // LANES):
        outs.append(_split_dot(xv[:, p * LANES:(p + 1) * LANES], ones_bd))
    return jnp.concatenate(outs, axis=1)


def _adaln_kernel(c_ref, w_ref, b_ref, o_ref):
    c = c_ref[...]
    cond = c * _sigmoid(c)
    o_ref[...] = _dot(cond.astype(BF16), w_ref[...].astype(BF16)) + b_ref[...]


def _adaln(c8, w, b):
    d, n = w.shape
    tn = 1024
    return pl.pallas_call(
        _adaln_kernel,
        grid=(n // tn,),
        in_specs=[pl.BlockSpec((8, d), lambda j: (0, 0)),
                  pl.BlockSpec((d, tn), lambda j: (0, j)),
                  pl.BlockSpec((1, tn), lambda j: (0, j))],
        out_specs=pl.BlockSpec((8, tn), lambda j: (0, j)),
        out_shape=jax.ShapeDtypeStruct((8, n), F32),
        compiler_params=_cparams("arbitrary"),
        name="adaln",
    )(c8, w, b.reshape(1, n))


def _inproj_kernel(x_ref, sc_ref, sh_ref, w_ref, o_ref, h_scr):
    @pl.when(pl.program_id(1) == 0)
    def _():
        h_scr[...] = (x_ref[...] * (1.0 + sc_ref[0]) + sh_ref[0]).astype(BF16)

    o_ref[...] = _dot(h_scr[...], w_ref[...])


def _inproj(x2, sc, sh, w, seq):
    m, d = x2.shape
    n = w.shape[1]
    tm, tn = 1024, 1024
    tpb = seq // tm
    return pl.pallas_call(
        _inproj_kernel,
        grid=(m // tm, n // tn),
        in_specs=[pl.BlockSpec((tm, d), lambda i, j: (i, 0)),
                  pl.BlockSpec((1, 1, d), lambda i, j: (i // tpb, 0, 0)),
                  pl.BlockSpec((1, 1, d), lambda i, j: (i // tpb, 0, 0)),
                  pl.BlockSpec((d, tn), lambda i, j: (0, j))],
        out_specs=pl.BlockSpec((tm, tn), lambda i, j: (i, j)),
        out_shape=jax.ShapeDtypeStruct((m, n), F32),
        scratch_shapes=[pltpu.VMEM((tm, d), BF16)],
        compiler_params=_cparams("arbitrary", "arbitrary"),
        name="inproj",
    )(x2, sc, sh, w)


def _prep_kernel(p_ref, prev_ref, mu_ref, w0_ref, a0_ref, kk_ref, ka_ref, wd_ref, wa_ref, wg_ref,
                 ones_ref, r_out, lw_out, k_out, v_out, kkn_out, b_out, g_out, *, tpb, c):
    cur = p_ref[...]
    first = (pl.program_id(0) % tpb) == 0
    prev_row = jnp.where(first, 0.0, prev_ref[7:8, :])
    shifted = pltpu.roll(cur, 1, axis=0)
    row0 = lax.broadcasted_iota(jnp.int32, cur.shape, 0) == 0
    shifted = jnp.where(row0, prev_row, shifted)
    ps = cur + (shifted - cur) * mu_ref[...]
    r = ps[:, 0:c]
    k = ps[:, c:2 * c]
    v = ps[:, 2 * c:3 * c]
    o = 3 * c
    wdp = ps[:, o:o + LANES]
    adp = ps[:, o + LANES:o + 2 * LANES]
    gdp = ps[:, o + 2 * LANES:o + 4 * LANES]
    z = w0_ref[...] + _dot(jnp.tanh(wdp).astype(BF16), wd_ref[...])
    lw = -jnp.exp(-0.5) * _sigmoid(z)
    iclr = _sigmoid(a0_ref[...] + _dot(adp.astype(BF16), wa_ref[...]))
    g = _dot(_sigmoid(gdp).astype(BF16), wg_ref[...])
    kk0 = k * kk_ref[...]
    ss = _head_sum(kk0 * kk0, ones_ref[...])
    kkn = kk0 / jnp.maximum(jnp.sqrt(ss), 1e-12)
    kmod = k * (1.0 + (iclr - 1.0) * ka_ref[...])
    r_out[...] = r
    lw_out[...] = lw
    k_out[...] = kmod
    v_out[...] = v
    kkn_out[...] = kkn
    b_out[...] = kkn * iclr
    g_out[...] = g


def _prep(p, mu, w0, a0, k_k, k_a, wd, wa, wg, ones_bd, seq, c, shift_blk):
    m = p.shape[0]
    ns = mu.shape[1]
    tm = 256
    tpb = seq // tm
    row = lambda a: a.reshape(1, -1)
    full = lambda a: pl.BlockSpec(a.shape, lambda i: (0,) * a.ndim)
    out = jax.ShapeDtypeStruct((m, c), F32)
    ospec = pl.BlockSpec((tm, c), lambda i: (i, 0))
    args = (p, p, mu, row(w0), row(a0), row(k_k), row(k_a), wd, wa, wg, ones_bd)
    in_specs = [pl.BlockSpec((tm, ns), lambda i: (i, shift_blk)),
                pl.BlockSpec((8, ns), lambda i: (jnp.maximum(i * (tm // 8) - 1, 0), shift_blk))]
    in_specs += [full(a) for a in args[2:]]
    return pl.pallas_call(
        functools.partial(_prep_kernel, tpb=tpb, c=c),
        grid=(m // tm,),
        in_specs=in_specs,
        out_specs=[ospec] * 7,
        out_shape=[out] * 7,
        compiler_params=_cparams("arbitrary"),
        name="rwkv_prep",
    )(*args)


def _wkv_kernel(r_ref, lw_ref, k_ref, v_ref, kk_ref, b_ref, y_ref, s_scr, *, chunk, c):
    L = chunk
    bsz = r_ref.shape[0]

    @pl.when(pl.program_id(0) == 0)
    def _():
        s_scr[...] = jnp.zeros_like(s_scr)

    def lanes(ref):
        return jnp.concatenate([ref[bi] for bi in range(bsz)], axis=1)

    ri = lax.broadcasted_iota(jnp.int32, (L, L), 0)
    ci = lax.broadcasted_iota(jnp.int32, (L, L), 1)
    tri = (ri >= ci).astype(BF16)
    lw = lanes(lw_ref)
    cs = _split_dot_lhs(tri, lw)
    cs_last = cs[L - 1:L, :]
    g_in = jnp.exp(cs)
    g_ex = jnp.exp(cs - lw)
    g_inv = jnp.exp(-cs)
    g_tail = jnp.exp(cs_last - cs)
    g_last = jnp.exp(cs_last)
    kk = lanes(kk_ref)
    bb = lanes(b_ref)
    kx = lanes(k_ref)
    vv = lanes(v_ref)
    a_t = -kk * g_ex
    r_t = lanes(r_ref) * g_in
    b_t = bb * g_inv
    k_t = kx * g_inv
    b_p = bb * g_tail
    k_p = kx * g_tail

    r2 = lax.broadcasted_iota(jnp.int32, (2 * L, 2 * L), 0)
    c2 = lax.broadcasted_iota(jnp.int32, (2 * L, 2 * L), 1)
    same = (r2 >= L) == (c2 >= L)
    m_strict = same & (r2 > c2)
    m_incl = same & (r2 >= c2)
    lane = lax.broadcasted_iota(jnp.int32, (L, LANES), 1)
    low = lane < HEAD_SIZE

    def stack(zv):
        return jnp.concatenate([jnp.where(low, zv, 0.0), jnp.where(low, 0.0, zv)], axis=0)

    per_seq = c // LANES
    pairs = range(bsz * per_seq)
    sls = [slice(p * LANES, (p + 1) * LANES) for p in pairs]
    a_s = [stack(a_t[:, sl]) for sl in sls]
    r_s = [stack(r_t[:, sl]) for sl in sls]
    v_s = [stack(vv[:, sl]) for sl in sls]
    ar = [jnp.concatenate([a_s[p], r_s[p]], axis=0).astype(BF16) for p in pairs]
    qb = [lax.dot_general(ar[p], stack(b_t[:, sls[p]]).astype(BF16), _NT, preferred_element_type=F32)
          for p in pairs]
    qk = [lax.dot_general(ar[p], stack(k_t[:, sls[p]]).astype(BF16), _NT, preferred_element_type=F32)
          for p in pairs]
    a_ab = [jnp.where(m_strict, q[:2 * L], 0.0) for q in qb]
    a_rb = [jnp.where(m_incl, q[2 * L:], 0.0) for q in qb]
    a_ak = [jnp.where(m_strict, q[:2 * L], 0.0) for q in qk]
    a_rk = [jnp.where(m_incl, q[2 * L:], 0.0) for q in qk]
    v_b = [x.astype(BF16) for x in v_s]
    z = [jnp.concatenate([_dot(a_ak[p].astype(BF16), v_b[p]), a_s[p]], axis=1) for p in pairs]
    amat = a_ab
    n_sq = max(L.bit_length() - 1, 1)
    for it in range(n_sq):
        ab = [x.astype(BF16) for x in amat]
        z = [z[p] + _dot(ab[p], z[p].astype(BF16)) for p in pairs]
        if it + 1 < n_sq:
            amat = [_dot(x, x) for x in ab]
    s_old = [s_scr[p] for p in pairs]
    s_b = [x.astype(BF16) for x in s_old]
    u_s = [_dot(z[p][:, LANES:].astype(BF16), s_b[p]) + z[p][:, :LANES] for p in pairs]
    uv = [jnp.concatenate([u_s[p], v_s[p]], axis=0).astype(BF16) for p in pairs]
    y_s = [_dot(r_s[p].astype(BF16), s_b[p])
           + _dot(jnp.concatenate([a_rb[p], a_rk[p]], axis=1).astype(BF16), uv[p]) for p in pairs]
    for p in pairs:
        y_ref[p // per_seq, :, sls[p % per_seq]] = y_s[p][:L] + y_s[p][L:]
    upd = [lax.dot_general(jnp.concatenate([stack(b_p[:, sls[p]]), stack(k_p[:, sls[p]])], axis=0).astype(BF16),
                           uv[p], _TN, preferred_element_type=F32) for p in pairs]
    for p in pairs:
        g_col = jnp.transpose(jnp.broadcast_to(g_last[:, sls[p]], (LANES, LANES)))
        s_scr[p] = s_old[p] * g_col + upd[p]


def _split_dot_lhs(w, xv):
    hi = xv.astype(BF16)
    lo = (xv - hi.astype(F32)).astype(BF16)
    return _dot(w, hi) + _dot(w, lo)


def _wkv(r, lw, k, v, kk, b, bsz, seq, c):
    L = WKV_CHUNK
    spec = pl.BlockSpec((bsz, L, c), lambda ci: (0, ci, 0))
    seqs = lambda a: a.reshape(bsz, seq, c)
    return pl.pallas_call(
        functools.partial(_wkv_kernel, chunk=L, c=c),
        grid=(seq // L,),
        in_specs=[spec] * 6,
        out_specs=spec,
        out_shape=jax.ShapeDtypeStruct((bsz, seq, c), F32),
        scratch_shapes=[pltpu.VMEM((bsz * c // LANES, LANES, LANES), F32)],
        compiler_params=_cparams("arbitrary"),
        name="wkv7",
    )(seqs(r), seqs(lw), seqs(k), seqs(v), seqs(kk), seqs(b)).reshape(bsz * seq, c)


def _post_kernel(y_ref, r_ref, k_ref, v_ref, g_ref, gate_ref, pool_ref, halo_ref, x_ref,
                 g1_ref, sc2_ref, sh2_ref, rk_ref, lng_ref, lnb_ref, ones_ref, wro_ref, wp_ref,
                 psc_ref, wo_ref, l1g_ref, l1b_ref, wr_ref, br_ref,
                 x1_out, h2_out, lg_out, pool_scr, *, tpb, alpha, d, tm):
    ones_bd = ones_ref[...]
    inv_n = 1.0 / HEAD_SIZE
    y = y_ref[...]
    mu = _head_sum(y, ones_bd) * inv_n
    dy = y - mu
    var = _head_sum(dy * dy, ones_bd) * inv_n
    yn = dy * lax.rsqrt(var + GN_EPS) * lng_ref[...] + lnb_ref[...]
    vv = v_ref[...]
    bonus = _head_sum(r_ref[...] * k_ref[...] * rk_ref[...], ones_bd) * vv
    y_rwkv = ((yn + bonus) * g_ref[...]).astype(BF16)
    branch_a = _dot(y_rwkv, wro_ref[...])

    first = (pl.program_id(0) % tpb) == 0
    halo = 16
    pool_scr[0:halo, :] = jnp.where(first, 0.0, halo_ref[...])
    pool_scr[halo:, :] = pool_ref[...]
    pos = (pl.program_id(0) % tpb) * tm + lax.broadcasted_iota(jnp.int32, (tm, 1), 0)
    gw = pool_ref.shape[1] // len(POOL_WINDOWS)
    dg = d // len(POOL_WINDOWS)
    outs = []
    for gi, win in enumerate(POOL_WINDOWS):
        cols = slice(gi * gw, (gi + 1) * gw)
        u = pool_scr[halo:, cols]
        wsum = u
        for s in range(1, win):
            wsum = wsum + pool_scr[halo - s:halo - s + tm, cols]
        cnt = jnp.minimum(pos + 1, win).astype(F32)
        dd = (wsum / cnt - u).astype(BF16)
        outs.append(_dot(dd, wp_ref[gi]))
    branch_b = jnp.concatenate(outs, axis=1) * psc_ref[...]

    gates = gate_ref[...]
    mixin = _sigmoid(gates[:, :d]) * branch_a + _sigmoid(gates[:, d:]) * branch_b
    mix = _dot(mixin.astype(BF16), wo_ref[...])
    xr = alpha * x_ref[...] + g1_ref[0] * mix
    m1 = jnp.mean(xr, axis=-1, keepdims=True)
    xc = xr - m1
    v1 = jnp.mean(xc * xc, axis=-1, keepdims=True)
    x1 = xc * lax.rsqrt(v1 + LN_EPS) * l1g_ref[...] + l1b_ref[...]
    x1_out[...] = x1
    h2 = x1 * (1.0 + sc2_ref[0]) + sh2_ref[0]
    h2_b = h2.astype(BF16)
    h2_r = h2_b.astype(F32)
    lo = lax.shift_right_logical(lax.bitcast_convert_type(h2_r[:, :d // 2], jnp.int32), 16)
    hi = lax.bitcast_convert_type(h2_r[:, d // 2:], jnp.int32) & jnp.int32(-65536)
    h2_out[...] = lo | hi
    lg_out[...] = _dot(h2_b, wr_ref[...]) + br_ref[...]


def _post(y, r, k, v, g, p, x2, g1, sc2, sh2, r_k, lnx_g, lnx_b, ones_bd, w_ro, w_pool, pool_scale,
          w_out, ln1_g, ln1_b, w_r, b_r, seq, alpha, gate_blk, pool_blk):
    m, d = x2.shape
    c = y.shape[1]
    pw = w_pool.shape[0] * w_pool.shape[1]
    tm = 256
    tpb = seq // tm
    row = lambda a: a.reshape(1, -1)
    const = lambda a: pl.BlockSpec(a.shape, lambda i: (0,) * a.ndim, pipeline_mode=pl.Buffered(1))
    tile = lambda w: pl.BlockSpec((tm, w), lambda i: (i, 0))
    bvec = pl.BlockSpec((1, 1, d), lambda i: (i // tpb, 0, 0))
    consts = (row(r_k), row(lnx_g), row(lnx_b), ones_bd, w_ro, w_pool, row(pool_scale), w_out,
              row(ln1_g), row(ln1_b), w_r, row(b_r))
    in_specs = [tile(c)] * 5 + [
        pl.BlockSpec((tm, 2 * d), lambda i: (i, gate_blk)),
        pl.BlockSpec((tm, pw), lambda i: (i, pool_blk)),
        pl.BlockSpec((16, pw), lambda i: (jnp.maximum(i * (tm // 16) - 1, 0), pool_blk)),
        tile(d), bvec, bvec, bvec] + [const(a) for a in consts]
    return pl.pallas_call(
        functools.partial(_post_kernel, tpb=tpb, alpha=alpha, d=d, tm=tm),
        grid=(m // tm,),
        in_specs=in_specs,
        out_specs=[tile(d), tile(d // 2), tile(LANES)],
        out_shape=[jax.ShapeDtypeStruct((m, d), F32), jax.ShapeDtypeStruct((m, d // 2), jnp.int32),
                   jax.ShapeDtypeStruct((m, LANES), F32)],
        scratch_shapes=[pltpu.VMEM((tm + 16, pw), F32)],
        compiler_params=_cparams("arbitrary"),
        name="merge_ln1",
    )(y, r, k, v, g, p, p, p, x2, g1, sc2, sh2, *consts)


def _route_kernel(lg_ref, idx_out, prob_out, rank_out, cnt_out, carry, *, n_exp, tm):
    @pl.when(pl.program_id(0) == 0)
    def _():
        carry[...] = jnp.zeros_like(carry)

    lane = lax.broadcasted_iota(jnp.int32, (tm, LANES), 1)
    lane_f = lane.astype(F32)
    neg = jnp.float32(-jnp.inf)
    work = jnp.where(lane < n_exp, lg_ref[...], neg)
    vals, sels = [], []
    idx_acc = jnp.zeros((tm, LANES), jnp.int32)
    hot = jnp.zeros((tm, LANES), F32)
    for kk in range(TOP_K):
        mx = jnp.max(work, axis=-1, keepdims=True)
        idx = jnp.min(jnp.where(work == mx, lane_f, float(LANES)), axis=-1, keepdims=True).astype(jnp.int32)
        sel = lane == idx
        work = jnp.where(sel, neg, work)
        vals.append(mx)
        sels.append(sel)
        idx_acc = jnp.where(lane == kk, idx, idx_acc)
        hot = hot + sel.astype(F32)
    es = [jnp.exp(vv - vals[0]) for vv in vals]
    den = es[0] + es[1] + es[2] + es[3]
    prob = jnp.zeros((tm, LANES), F32)
    for kk in range(TOP_K):
        prob = jnp.where(lane == kk, es[kk] / den, prob)
    ri = lax.broadcasted_iota(jnp.int32, (tm, tm), 0)
    ci = lax.broadcasted_iota(jnp.int32, (tm, tm), 1)
    before = _dot((ri > ci).astype(BF16), hot.astype(BF16)) + carry[0:1, :]
    rank = jnp.zeros((tm, LANES), jnp.int32)
    for kk in range(TOP_K):
        rk = jnp.sum(jnp.where(sels[kk], before, 0.0), axis=-1, keepdims=True).astype(jnp.int32)
        rank = jnp.where(lane == kk, rk, rank)
    carry[...] = carry[...] + jnp.sum(hot, axis=0, keepdims=True)
    idx_out[...] = idx_acc
    prob_out[...] = prob
    rank_out[...] = rank
    cnt_out[...] = carry[...]


def _route(logits, n_exp):
    m = logits.shape[0]
    tm = 512
    tile = pl.BlockSpec((tm, LANES), lambda i: (i, 0))
    return pl.pallas_call(
        functools.partial(_route_kernel, n_exp=n_exp, tm=tm),
        grid=(m // tm,),
        in_specs=[tile],
        out_specs=[tile, tile, tile, pl.BlockSpec((8, LANES), lambda i: (0, 0))],
        out_shape=[jax.ShapeDtypeStruct((m, LANES), jnp.int32), jax.ShapeDtypeStruct((m, LANES), F32),
                   jax.ShapeDtypeStruct((m, LANES), jnp.int32), jax.ShapeDtypeStruct((8, LANES), F32)],
        scratch_shapes=[pltpu.VMEM((8, LANES), F32)],
        compiler_params=_cparams("arbitrary"),
        name="route",
    )(logits)


def _row_copy(src_hbm, dst, src_row, dst_row, sem):
    return pltpu.make_async_copy(src_hbm.at[pl.ds(src_row, 1), :], dst.at[pl.ds(dst_row, 1), :], sem)


def _dispatch_kernel(dest_ref, pend_ref, h_ref, x_hbm, stage, zero, zsem, sems, *, tb, n_exp, sub, n_rows):
    i = pl.program_id(0)
    slot = i % 2
    stage[slot] = h_ref[...]

    def zero_copy(row0):
        return pltpu.make_async_copy(zero, x_hbm.at[pl.ds(pl.multiple_of(row0, sub), sub), :], zsem)

    def zero_fill(fn):
        for e in range(n_exp):
            lo_e = pend_ref[e - 1] if e > 0 else 0

            @pl.when(pend_ref[e] > lo_e)
            def _():
                fn(zero_copy(pend_ref[e] - sub))

        def body(blk, carry):
            fn(zero_copy(blk * sub))
            return carry

        lax.fori_loop(pend_ref[n_exp - 1] // sub, n_rows // sub, body, 0)

    @pl.when(i == 0)
    def _():
        zero[...] = jnp.zeros_like(zero)
        zero_fill(lambda cp: cp.start())
        zero_fill(lambda cp: cp.wait())

    def issue(u, carry):
        t = i * tb + u
        for kk in range(TOP_K):
            _row_copy(stage.at[slot], x_hbm, u, dest_ref[t * TOP_K + kk], sems.at[slot]).start(priority=kk % 2)
        return carry

    lax.fori_loop(0, tb, issue, 0, unroll=2)

    def drain_on(slot_):
        def body(u, carry):
            for kk in range(TOP_K):
                _row_copy(stage.at[slot_], x_hbm, 0, 0, sems.at[slot_]).wait()
            return carry

        lax.fori_loop(0, tb, body, 0, unroll=2)

    @pl.when(i > 0)
    def _():
        drain_on(1 - slot)

    @pl.when(i == pl.num_programs(0) - 1)
    def _():
        drain_on(slot)


def _dispatch(dest, pad_end, h2p, n_rows):
    m, w = h2p.shape
    tb = 256
    n_exp = pad_end.shape[0]
    return pl.pallas_call(
        functools.partial(_dispatch_kernel, tb=tb, n_exp=n_exp, sub=MOE_SUB, n_rows=n_rows),
        grid_spec=pltpu.PrefetchScalarGridSpec(
            num_scalar_prefetch=2,
            grid=(m // tb,),
            in_specs=[pl.BlockSpec((tb, w), lambda i, dst, pend: (i, 0))],
            out_specs=pl.BlockSpec(memory_space=pl.ANY),
            scratch_shapes=[pltpu.VMEM((2, tb, w), jnp.int32), pltpu.VMEM((MOE_SUB, w), jnp.int32),
                            pltpu.SemaphoreType.DMA(()), pltpu.SemaphoreType.DMA((2,))]),
        out_shape=jax.ShapeDtypeStruct((n_rows, w), jnp.int32),
        compiler_params=_cparams("arbitrary"),
        name="dispatch",
    )(dest, pad_end, h2p)


def _roll_lanes(v, shift):
    return jnp.concatenate([pltpu.roll(v[:, g * LANES:(g + 1) * LANES], shift, axis=1)
                            for g in range(v.shape[1] // LANES)], axis=1)


def _expert_kernel(e_ref, st_ref, ns_ref, tot_ref, x_hbm, wgu_ref, bgu_ref, wd_ref, bd_ref,
                   y_hbm, x_ref, acc, xw, wd_i, wd_b, act, sem, x_sems, *, n_ff, rmax, sub, n_rows):
    i = pl.program_id(0)
    j = pl.program_id(1)
    nsub = ns_ref[i]
    n_sub_max = rmax // sub
    tfc = wd_ref.shape[1]
    half = tfc // 2

    def out_copy(row0, s):
        return pltpu.make_async_copy(acc.at[pl.ds(s * sub, sub), :],
                                     y_hbm.at[pl.ds(pl.multiple_of(row0, sub), sub), :], sem)

    def for_valid_subs(item, fn):
        for s in range(n_sub_max):
            @pl.when(s < ns_ref[item])
            def _():
                fn(out_copy(st_ref[item] + s * sub, s))

    half_d = x_ref.shape[1] // 2

    def fetch(s):
        row0 = pl.multiple_of(st_ref[i] + s * sub, sub)
        return pltpu.make_async_copy(x_hbm.at[pl.ds(row0, sub), :], xw.at[s % 2], x_sems.at[s % 2])

    @pl.when(j == 0)
    def _():
        for s in range(2):
            @pl.when(s < nsub)
            def _():
                fetch(s).start()

    @pl.when(nsub > 0)
    def _():
        for g in range(wd_b.shape[1] // LANES):
            cols = slice(g * LANES, (g + 1) * LANES)
            wd_i[g % 2, pl.ds(0, half, stride=2), :] = wd_ref[0, 0:half, cols]
            wd_i[g % 2, pl.ds(1, half, stride=2), :] = wd_ref[0, half:tfc, cols]
            wd_b[:, cols] = wd_i[g % 2].astype(BF16)

    @pl.when(j == 0)
    def _():
        for s in range(n_sub_max):
            @pl.when(s < nsub)
            def _():
                fetch(s).wait()
                words = xw[s % 2]
                rows = slice(s * sub, (s + 1) * sub)
                x_ref[rows, 0:half_d] = lax.bitcast_convert_type(words << 16, F32).astype(BF16)
                x_ref[rows, half_d:] = lax.bitcast_convert_type(words & jnp.int32(-65536), F32).astype(BF16)

                @pl.when(s + 2 < nsub)
                def _():
                    fetch(s + 2).start()

    cw = 2 * LANES

    def paired(gu):
        gate = jnp.minimum(gu, SWIGLU_LIMIT)
        glu = gate * _sigmoid(gate * SWIGLU_ALPHA)
        up1 = jnp.clip(gu, -SWIGLU_LIMIT, SWIGLU_LIMIT) + 1.0
        return glu * _roll_lanes(up1, LANES - 1)

    def activate(row0, m_rows):
        rows = pl.ds(pl.multiple_of(row0, sub), m_rows)
        xm = x_ref[rows, :]
        even = (lax.broadcasted_iota(jnp.int32, (m_rows, cw), 1) % 2) == 0
        for c in range(tfc // cw):
            lo = slice(c * cw, (c + 1) * cw)
            hi = slice(tfc + c * cw, tfc + (c + 1) * cw)
            pa = paired(_dot(xm, wgu_ref[0, :, lo].astype(BF16)) + bgu_ref[0, :, lo])
            pb = paired(_dot(xm, wgu_ref[0, :, hi].astype(BF16)) + bgu_ref[0, :, hi])
            act[rows, lo] = jnp.where(even, pa, _roll_lanes(pb, 1)).astype(BF16)

    def project(row0, m_rows):
        rows = pl.ds(pl.multiple_of(row0, sub), m_rows)
        acc[rows, :] += _dot(act[rows, :], wd_b[...])

    def per_row_count(fn):
        done = 0
        for width in (w for w in (8, 4, 2, 1) if w * sub <= 1024):
            count = (nsub - done) // width

            def body(b, carry, done=done, width=width):
                fn((done + b * width) * sub, width * sub)
                return carry

            lax.fori_loop(0, count, body, 0)
            done = done + count * width

    per_row_count(activate)

    @pl.when(j == 0)
    def _():
        @pl.when(i > 0)
        def _():
            for_valid_subs(i - 1, lambda cp: cp.wait())

        acc[...] = jnp.broadcast_to(bd_ref[0], acc.shape)

    per_row_count(project)

    @pl.when(j == n_ff - 1)
    def _():
        for_valid_subs(i, lambda cp: cp.start())

        @pl.when(i == pl.num_programs(0) - 1)
        def _():
            for_valid_subs(i, lambda cp: cp.wait())
            acc[0:sub, :] = jnp.zeros((sub, acc.shape[1]), F32)
            first = tot_ref[0] // sub

            def fill_start(blk, carry):
                out_copy(blk * sub, 0).start()
                return carry

            def fill_wait(blk, carry):
                out_copy(blk * sub, 0).wait()
                return carry

            lax.fori_loop(first, n_rows // sub, fill_start, 0)
            lax.fori_loop(first, n_rows // sub, fill_wait, 0)


def _experts(item_e, item_start, item_nsub, total_rows, x_buf, w_gu, b_gu, w_down, b_down, n_rows):
    n_items = item_e.shape[0]
    n_exp, dff, d = w_down.shape
    tf = MOE_TF
    n_ff = dff // tf
    rmax = MOE_RMAX

    def jj(i, j, ns):
        return jnp.where(ns[i] > 0, j, n_ff - 1)

    in_specs = [
        pl.BlockSpec(memory_space=pl.ANY),
        pl.BlockSpec((1, d, 2 * tf), lambda i, j, e, st, ns, tot: (e[i], 0, jj(i, j, ns))),
        pl.BlockSpec((1, 1, 2 * tf), lambda i, j, e, st, ns, tot: (e[i], 0, jj(i, j, ns))),
        pl.BlockSpec((1, tf, d), lambda i, j, e, st, ns, tot: (e[i], jj(i, j, ns), 0)),
        pl.BlockSpec((1, 1, d), lambda i, j, e, st, ns, tot: (e[i], 0, 0)),
    ]
    return pl.pallas_call(
        functools.partial(_expert_kernel, n_ff=n_ff, rmax=rmax, sub=MOE_SUB, n_rows=n_rows),
        grid_spec=pltpu.PrefetchScalarGridSpec(
            num_scalar_prefetch=4,
            grid=(n_items, n_ff),
            in_specs=in_specs,
            out_specs=pl.BlockSpec(memory_space=pl.ANY),
            scratch_shapes=[pltpu.VMEM((rmax, d), BF16), pltpu.VMEM((rmax, d), F32),
                            pltpu.VMEM((2, MOE_SUB, d // 2), jnp.int32), pltpu.VMEM((2, tf, LANES), F32),
                            pltpu.VMEM((tf, d), BF16), pltpu.VMEM((rmax, tf), BF16),
                            pltpu.SemaphoreType.DMA(()), pltpu.SemaphoreType.DMA((2,))]),
        out_shape=jax.ShapeDtypeStruct((n_rows, d), F32),
        compiler_params=_cparams("arbitrary", "arbitrary"),
        name="experts",
    )(item_e, item_start, item_nsub, total_rows, x_buf, w_gu, b_gu.reshape(n_exp, 1, 2 * dff),
      w_down, b_down.reshape(n_exp, 1, d))


def _combine_kernel(dest_ref, y_hbm, prob_ref, x1_ref, g2_ref, lg_ref, lb_ref, o_ref, buf, sems,
                    *, tm, alpha):
    i = pl.program_id(0)
    slot = i % 2

    def issue_block(blk, slot_):
        def body(t, carry):
            for kk in range(TOP_K):
                _row_copy(y_hbm, buf.at[slot_, kk], dest_ref[(blk * tm + t) * TOP_K + kk], t,
                          sems.at[slot_]).start(priority=kk % 2)
            return carry

        lax.fori_loop(0, tm, body, 0, unroll=2)

    @pl.when(i == 0)
    def _():
        issue_block(0, 0)

    @pl.when(i + 1 < pl.num_programs(0))
    def _():
        issue_block(i + 1, 1 - slot)

    def drain(t, carry):
        for kk in range(TOP_K):
            _row_copy(y_hbm, buf.at[slot, kk], 0, t, sems.at[slot]).wait()
        return carry

    lax.fori_loop(0, tm, drain, 0, unroll=2)
    prob = prob_ref[...]
    f = prob[:, 0:1] * buf[slot, 0]
    for kk in range(1, TOP_K):
        f = f + prob[:, kk:kk + 1] * buf[slot, kk]
    xr = alpha * x1_ref[...] + g2_ref[0] * f
    m2 = jnp.mean(xr, axis=-1, keepdims=True)
    xc = xr - m2
    v2 = jnp.mean(xc * xc, axis=-1, keepdims=True)
    o_ref[...] = xc * lax.rsqrt(v2 + LN_EPS) * lg_ref[...] + lb_ref[...]


def _combine(dest, y_buf, prob, x1, g2, ln_g, ln_b, seq, alpha):
    m, d = x1.shape
    tm = 128
    tpb = seq // tm
    return pl.pallas_call(
        functools.partial(_combine_kernel, tm=tm, alpha=alpha),
        grid_spec=pltpu.PrefetchScalarGridSpec(
            num_scalar_prefetch=1,
            grid=(m // tm,),
            in_specs=[pl.BlockSpec(memory_space=pl.ANY),
                      pl.BlockSpec((tm, LANES), lambda i, dst: (i, 0)),
                      pl.BlockSpec((tm, d), lambda i, dst: (i, 0)),
                      pl.BlockSpec((1, 1, d), lambda i, dst: (i // tpb, 0, 0)),
                      pl.BlockSpec((1, d), lambda i, dst: (0, 0)),
                      pl.BlockSpec((1, d), lambda i, dst: (0, 0))],
            out_specs=pl.BlockSpec((tm, d), lambda i, dst: (i, 0)),
            scratch_shapes=[pltpu.VMEM((2, TOP_K, tm, d), F32), pltpu.SemaphoreType.DMA((2,))]),
        out_shape=jax.ShapeDtypeStruct((m, d), F32),
        compiler_params=_cparams("arbitrary"),
        name="combine_ln2",
    )(dest, y_buf, prob, x1, g2, ln_g.reshape(1, d), ln_b.reshape(1, d))


def _pad_rows(w, rows):
    return jnp.pad(w, ((0, rows - w.shape[0]), (0, 0)))


def _pad_cols(w, cols):
    return jnp.pad(w, ((0, 0), (0, cols - w.shape[1])))


def _layer(x2, c8, bsz, seq, alpha, w_ada, b_ada, w_in, mu_shift, w0, w_decay_up, a0, w_iclr_up,
           w_glora_up, k_k, k_a, r_k, lnx_g, lnx_b, w_rwkv_out, w_pool, pool_scale, w_out, ln1_g,
           ln1_b, w_router, b_router, w_gu, b_gu, w_down, b_down, ln2_g, ln2_b):
    m, d = x2.shape
    c = w0.shape[0]
    n_dec, n_icl, n_gate = w_decay_up.shape[0], w_iclr_up.shape[0], w_glora_up.shape[0]
    pw = w_pool.shape[0] * w_pool.shape[1]
    n_exp = w_router.shape[1]
    assert n_dec <= LANES and n_icl <= LANES and n_gate <= 2 * LANES and d == 2 * c and pw == c

    mod = _adaln(c8, w_ada, b_ada)[:bsz]
    sh1, sc1, g1, sh2, sc2, g2 = [t.reshape(bsz, 1, d) for t in jnp.split(mod, 6, axis=-1)]

    o = 3 * c
    segs = [(o, n_dec, LANES), (o + n_dec, n_icl, LANES), (o + n_dec + n_icl, n_gate, 2 * LANES)]
    n_shift = o + n_dec + n_icl + n_gate
    zpad = 2 * d - 3 * c - 4 * LANES
    w_shift = jnp.concatenate([w_in[:, :o]] + [_pad_cols(w_in[:, s:s + n], wd) for s, n, wd in segs]
                              + [jnp.zeros((d, zpad), F32)], axis=1)
    mu = jnp.concatenate([mu_shift[:o]] + [jnp.pad(mu_shift[s:s + n], (0, wd - n)) for s, n, wd in segs]
                         + [jnp.zeros((zpad,), F32)]).reshape(1, -1)
    w_cat = jnp.concatenate([w_in[:, n_shift + pw:], w_shift, w_in[:, n_shift:n_shift + pw]],
                            axis=1).astype(BF16)
    p = _inproj(x2, sc1, sh1, w_cat, seq)

    ones_bd = jnp.kron(jnp.eye(LANES // HEAD_SIZE, dtype=F32), jnp.ones((HEAD_SIZE, HEAD_SIZE), F32)).astype(BF16)
    r, lw, kmod, v, kkn, bvec, g = _prep(
        p, mu, w0, a0, k_k, k_a, _pad_rows(w_decay_up, LANES).astype(BF16),
        _pad_rows(w_iclr_up, LANES).astype(BF16), _pad_rows(w_glora_up, 2 * LANES).astype(BF16),
        ones_bd, seq, c, shift_blk=1)
    y = _wkv(r, lw, kmod, v, kkn, bvec, bsz, seq, c)

    w_r = _pad_cols(w_router, LANES).astype(BF16)
    b_r = jnp.pad(b_router, (0, LANES - n_exp))
    x1, h2, logits = _post(y, r, kmod, v, g, p, x2, g1, sc2, sh2, r_k.reshape(-1), lnx_g, lnx_b, ones_bd,
                           w_rwkv_out.astype(BF16), w_pool.astype(BF16), pool_scale, w_out.astype(BF16),
                           ln1_g, ln1_b, w_r, b_r, seq, alpha, gate_blk=0, pool_blk=(4 * d) // pw)

    idx, prob, rank, cnt = _route(logits, n_exp)
    counts = cnt[0, :n_exp].astype(jnp.int32)
    padded = (counts + MOE_SUB - 1) // MOE_SUB * MOE_SUB
    pad_end = jnp.cumsum(padded)
    pad_start = pad_end - padded
    mk = m * TOP_K
    n_rows = (mk // MOE_SUB + n_exp) * MOE_SUB
    dest = (pad_start[idx[:, :TOP_K]] + rank[:, :TOP_K]).reshape(mk).astype(jnp.int32)
    n_items = mk // MOE_RMAX + n_exp
    per_e = (padded + MOE_RMAX - 1) // MOE_RMAX
    it_end = jnp.cumsum(per_e)
    it_start = it_end - per_e
    ids = jnp.arange(n_items, dtype=jnp.int32)
    item_e = jnp.minimum(jnp.searchsorted(it_end, ids, side='right'), n_exp - 1).astype(jnp.int32)
    local = ids - it_start[item_e]
    used = ids < it_end[-1]
    item_start = jnp.where(used, pad_start[item_e] + local * MOE_RMAX, 0).astype(jnp.int32)
    item_nsub = jnp.where(used, jnp.clip((padded[item_e] - local * MOE_RMAX) // MOE_SUB, 0,
                                         MOE_RMAX // MOE_SUB), 0).astype(jnp.int32)

    x_buf = _dispatch(dest, pad_end.astype(jnp.int32), h2, n_rows)
    y_buf = _experts(item_e, item_start, item_nsub, pad_end[-1:].astype(jnp.int32), x_buf,
                     w_gu, b_gu, w_down, b_down, n_rows)
    return _combine(dest, y_buf, prob, x1, g2, ln2_g, ln2_b, seq, alpha)


def kernel(x, c, w_ada, b_ada, w_in, mu_shift, w0, w_decay_up, a0, w_iclr_up, w_glora_up, k_k, k_a, r_k, lnx_g, lnx_b, w_rwkv_out, w_pool, pool_scale, w_out, ln1_g, ln1_b, w_router, b_router, w_gu, b_gu, w_down, b_down, ln2_g, ln2_b):
    bsz, seq, d = x.shape
    depth = w_ada.shape[0]
    alpha = float((2 * depth) ** 0.25)
    c8 = jnp.pad(c, ((0, 8 - bsz), (0, 0)))
    x2 = x.reshape(bsz * seq, d)
    weights = (w_ada, b_ada, w_in, mu_shift, w0, w_decay_up, a0, w_iclr_up, w_glora_up, k_k, k_a, r_k,
               lnx_g, lnx_b, w_rwkv_out, w_pool, pool_scale, w_out, ln1_g, ln1_b, w_router, b_router,
               w_gu, b_gu, w_down, b_down, ln2_g, ln2_b)
    for l in range(depth):
        x2 = _layer(x2, c8, bsz, seq, alpha, *[w[l] for w in weights])
    return x2.reshape(bsz, seq, d)
```

```python
import functools

import jax
import jax.numpy as jnp
from jax import lax
from jax.experimental import pallas as pl
from jax.experimental.pallas import tpu as pltpu

F32 = jnp.float32
BF16 = jnp.bfloat16

HEAD_SIZE = 64
LANES = 128
TOP_K = 4
LN_EPS = 1e-5
GN_EPS = 64e-5
POOL_WINDOWS = (2, 4, 8, 16)
SWIGLU_LIMIT = 7.0
SWIGLU_ALPHA = 1.702
WKV_CHUNK = 64
MOE_SUB = 256
MOE_RMAX = 1280
MOE_TF = 512
VMEM_LIMIT = 54 * 1024 * 1024

_NT = (((1,), (1,)), ((), ()))
_TN = (((0,), (0,)), ((), ()))


def _cparams(*sem):
    return pltpu.CompilerParams(dimension_semantics=sem, vmem_limit_bytes=VMEM_LIMIT)


def _sigmoid(z):
    return 0.5 * jnp.tanh(0.5 * z) + 0.5


def _dot(a, b):
    return jnp.dot(a, b, preferred_element_type=F32)


def _split_dot(xv, w):
    hi = xv.astype(BF16)
    lo = (xv - hi.astype(F32)).astype(BF16)
    return _dot(hi, w) + _dot(lo, w)


def _head_sum(xv, ones_bd):
    outs = []
    for p in range(xv.shape[1] // LANES):
        outs.append(_split_dot(xv[:, p * LANES:(p + 1) * LANES], ones_bd))
    return jnp.concatenate(outs, axis=1)


def _adaln_kernel(c_ref, w_ref, b_ref, o_ref):
    c = c_ref[...]
    cond = c * _sigmoid(c)
    o_ref[...] = _dot(cond.astype(BF16), w_ref[...].astype(BF16)) + b_ref[...]


def _adaln(c8, w, b):
    d, n = w.shape
    tn = 1024
    return pl.pallas_call(
        _adaln_kernel,
        grid=(n // tn,),
        in_specs=[pl.BlockSpec((8, d), lambda j: (0, 0)),
                  pl.BlockSpec((d, tn), lambda j: (0, j)),
                  pl.BlockSpec((1, tn), lambda j: (0, j))],
        out_specs=pl.BlockSpec((8, tn), lambda j: (0, j)),
        out_shape=jax.ShapeDtypeStruct((8, n), F32),
        compiler_params=_cparams("arbitrary"),
        name="adaln",
    )(c8, w, b.reshape(1, n))


def _inproj_kernel(x_ref, sc_ref, sh_ref, w_ref, o_ref, h_scr):
    @pl.when(pl.program_id(1) == 0)
    def _():
        h_scr[...] = (x_ref[...] * (1.0 + sc_ref[0]) + sh_ref[0]).astype(BF16)

    o_ref[...] = _dot(h_scr[...], w_ref[...])


def _inproj(x2, sc, sh, w, seq):
    m, d = x2.shape
    n = w.shape[1]
    tm, tn = 1024, 1536
    tpb = seq // tm
    return pl.pallas_call(
        _inproj_kernel,
        grid=(m // tm, n // tn),
        in_specs=[pl.BlockSpec((tm, d), lambda i, j: (i, 0)),
                  pl.BlockSpec((1, 1, d), lambda i, j: (i // tpb, 0, 0)),
                  pl.BlockSpec((1, 1, d), lambda i, j: (i // tpb, 0, 0)),
                  pl.BlockSpec((d, tn), lambda i, j: (0, j))],
        out_specs=pl.BlockSpec((tm, tn), lambda i, j: (i, j)),
        out_shape=jax.ShapeDtypeStruct((m, n), F32),
        scratch_shapes=[pltpu.VMEM((tm, d), BF16)],
        compiler_params=_cparams("arbitrary", "arbitrary"),
        name="inproj",
    )(x2, sc, sh, w)


def _prep_kernel(p_ref, prev_ref, mu_ref, w0_ref, a0_ref, kk_ref, ka_ref, wd_ref, wa_ref, wg_ref,
                 ones_ref, r_out, lw_out, k_out, v_out, kkn_out, b_out, g_out, *, tpb, c):
    cur = p_ref[...]
    first = (pl.program_id(0) % tpb) == 0
    prev_row = jnp.where(first, 0.0, prev_ref[7:8, :])
    shifted = pltpu.roll(cur, 1, axis=0)
    row0 = lax.broadcasted_iota(jnp.int32, cur.shape, 0) == 0
    shifted = jnp.where(row0, prev_row, shifted)
    ps = cur + (shifted - cur) * mu_ref[...]
    r = ps[:, 0:c]
    k = ps[:, c:2 * c]
    v = ps[:, 2 * c:3 * c]
    o = 3 * c
    wdp = ps[:, o:o + LANES]
    adp = ps[:, o + LANES:o + 2 * LANES]
    gdp = ps[:, o + 2 * LANES:o + 4 * LANES]
    z = w0_ref[...] + _dot(jnp.tanh(wdp).astype(BF16), wd_ref[...])
    lw = -jnp.exp(-0.5) * _sigmoid(z)
    iclr = _sigmoid(a0_ref[...] + _dot(adp.astype(BF16), wa_ref[...]))
    g = _dot(_sigmoid(gdp).astype(BF16), wg_ref[...])
    kk0 = k * kk_ref[...]
    ss = _head_sum(kk0 * kk0, ones_ref[...])
    kkn = kk0 / jnp.maximum(jnp.sqrt(ss), 1e-12)
    kmod = k * (1.0 + (iclr - 1.0) * ka_ref[...])
    r_out[...] = r
    lw_out[...] = lw
    k_out[...] = kmod
    v_out[...] = v
    kkn_out[...] = kkn
    b_out[...] = kkn * iclr
    g_out[...] = g


def _prep(p, mu, w0, a0, k_k, k_a, wd, wa, wg, ones_bd, seq, c, shift_blk):
    m = p.shape[0]
    ns = mu.shape[1]
    tm = 256
    tpb = seq // tm
    row = lambda a: a.reshape(1, -1)
    full = lambda a: pl.BlockSpec(a.shape, lambda i: (0,) * a.ndim)
    out = jax.ShapeDtypeStruct((m, c), F32)
    ospec = pl.BlockSpec((tm, c), lambda i: (i, 0))
    args = (p, p, mu, row(w0), row(a0), row(k_k), row(k_a), wd, wa, wg, ones_bd)
    in_specs = [pl.BlockSpec((tm, ns), lambda i: (i, shift_blk)),
                pl.BlockSpec((8, ns), lambda i: (jnp.maximum(i * (tm // 8) - 1, 0), shift_blk))]
    in_specs += [full(a) for a in args[2:]]
    return pl.pallas_call(
        functools.partial(_prep_kernel, tpb=tpb, c=c),
        grid=(m // tm,),
        in_specs=in_specs,
        out_specs=[ospec] * 7,
        out_shape=[out] * 7,
        compiler_params=_cparams("arbitrary"),
        name="rwkv_prep",
    )(*args)


def _wkv_kernel(r_ref, lw_ref, k_ref, v_ref, kk_ref, b_ref, y_ref, s_scr, *, chunk, c):
    L = chunk
    bsz = r_ref.shape[0]

    @pl.when(pl.program_id(0) == 0)
    def _():
        s_scr[...] = jnp.zeros_like(s_scr)

    def lanes(ref):
        return jnp.concatenate([ref[bi] for bi in range(bsz)], axis=1)

    ri = lax.broadcasted_iota(jnp.int32, (L, L), 0)
    ci = lax.broadcasted_iota(jnp.int32, (L, L), 1)
    tri = (ri >= ci).astype(BF16)
    lw = lanes(lw_ref)
    cs = _split_dot_lhs(tri, lw)
    cs_last = cs[L - 1:L, :]
    g_in = jnp.exp(cs)
    g_ex = jnp.exp(cs - lw)
    g_inv = jnp.exp(-cs)
    g_tail = jnp.exp(cs_last - cs)
    g_last = jnp.exp(cs_last)
    kk = lanes(kk_ref)
    bb = lanes(b_ref)
    kx = lanes(k_ref)
    vv = lanes(v_ref)
    a_t = -kk * g_ex
    r_t = lanes(r_ref) * g_in
    b_t = bb * g_inv
    k_t = kx * g_inv
    b_p = bb * g_tail
    k_p = kx * g_tail

    r2 = lax.broadcasted_iota(jnp.int32, (2 * L, 2 * L), 0)
    c2 = lax.broadcasted_iota(jnp.int32, (2 * L, 2 * L), 1)
    same = (r2 >= L) == (c2 >= L)
    m_strict = same & (r2 > c2)
    m_incl = same & (r2 >= c2)
    lane = lax.broadcasted_iota(jnp.int32, (L, LANES), 1)
    low = lane < HEAD_SIZE

    def stack(zv):
        return jnp.concatenate([jnp.where(low, zv, 0.0), jnp.where(low, 0.0, zv)], axis=0)

    per_seq = c // LANES
    pairs = range(bsz * per_seq)
    sls = [slice(p * LANES, (p + 1) * LANES) for p in pairs]
    a_s = [stack(a_t[:, sl]) for sl in sls]
    r_s = [stack(r_t[:, sl]) for sl in sls]
    v_s = [stack(vv[:, sl]) for sl in sls]
    ar = [jnp.concatenate([a_s[p], r_s[p]], axis=0).astype(BF16) for p in pairs]
    qb = [lax.dot_general(ar[p], stack(b_t[:, sls[p]]).astype(BF16), _NT, preferred_element_type=F32)
          for p in pairs]
    qk = [lax.dot_general(ar[p], stack(k_t[:, sls[p]]).astype(BF16), _NT, preferred_element_type=F32)
          for p in pairs]
    a_ab = [jnp.where(m_strict, q[:2 * L], 0.0) for q in qb]
    a_rb = [jnp.where(m_incl, q[2 * L:], 0.0) for q in qb]
    a_ak = [jnp.where(m_strict, q[:2 * L], 0.0) for q in qk]
    a_rk = [jnp.where(m_incl, q[2 * L:], 0.0) for q in qk]
    v_b = [x.astype(BF16) for x in v_s]
    z = [jnp.concatenate([_dot(a_ak[p].astype(BF16), v_b[p]), a_s[p]], axis=1) for p in pairs]
    amat = a_ab
    n_sq = max(L.bit_length() - 1, 1)
    for it in range(n_sq):
        ab = [x.astype(BF16) for x in amat]
        z = [z[p] + _dot(ab[p], z[p].astype(BF16)) for p in pairs]
        if it + 1 < n_sq:
            amat = [_dot(x, x) for x in ab]
    s_old = [s_scr[p] for p in pairs]
    s_b = [x.astype(BF16) for x in s_old]
    u_s = [_dot(z[p][:, LANES:].astype(BF16), s_b[p]) + z[p][:, :LANES] for p in pairs]
    uv = [jnp.concatenate([u_s[p], v_s[p]], axis=0).astype(BF16) for p in pairs]
    y_s = [_dot(r_s[p].astype(BF16), s_b[p])
           + _dot(jnp.concatenate([a_rb[p], a_rk[p]], axis=1).astype(BF16), uv[p]) for p in pairs]
    for p in pairs:
        y_ref[p // per_seq, :, sls[p % per_seq]] = y_s[p][:L] + y_s[p][L:]
    upd = [lax.dot_general(jnp.concatenate([stack(b_p[:, sls[p]]), stack(k_p[:, sls[p]])], axis=0).astype(BF16),
                           uv[p], _TN, preferred_element_type=F32) for p in pairs]
    for p in pairs:
        g_col = jnp.transpose(jnp.broadcast_to(g_last[:, sls[p]], (LANES, LANES)))
        s_scr[p] = s_old[p] * g_col + upd[p]


def _split_dot_lhs(w, xv):
    hi = xv.astype(BF16)
    lo = (xv - hi.astype(F32)).astype(BF16)
    return _dot(w, hi) + _dot(w, lo)


def _wkv(r, lw, k, v, kk, b, bsz, seq, c):
    L = WKV_CHUNK
    spec = pl.BlockSpec((bsz, L, c), lambda ci: (0, ci, 0))
    seqs = lambda a: a.reshape(bsz, seq, c)
    return pl.pallas_call(
        functools.partial(_wkv_kernel, chunk=L, c=c),
        grid=(seq // L,),
        in_specs=[spec] * 6,
        out_specs=spec,
        out_shape=jax.ShapeDtypeStruct((bsz, seq, c), F32),
        scratch_shapes=[pltpu.VMEM((bsz * c // LANES, LANES, LANES), F32)],
        compiler_params=_cparams("arbitrary"),
        name="wkv7",
    )(seqs(r), seqs(lw), seqs(k), seqs(v), seqs(kk), seqs(b)).reshape(bsz * seq, c)


def _post_kernel(y_ref, r_ref, k_ref, v_ref, g_ref, gate_ref, pool_ref, halo_ref, x_ref,
                 g1_ref, sc2_ref, sh2_ref, rk_ref, lng_ref, lnb_ref, ones_ref, wro_ref, wp_ref,
                 psc_ref, wo_ref, l1g_ref, l1b_ref, wr_ref, br_ref,
                 x1_out, h2_out, lg_out, pool_scr, *, tpb, alpha, d, tm):
    ones_bd = ones_ref[...]
    inv_n = 1.0 / HEAD_SIZE
    y = y_ref[...]
    mu = _head_sum(y, ones_bd) * inv_n
    dy = y - mu
    var = _head_sum(dy * dy, ones_bd) * inv_n
    yn = dy * lax.rsqrt(var + GN_EPS) * lng_ref[...] + lnb_ref[...]
    vv = v_ref[...]
    bonus = _head_sum(r_ref[...] * k_ref[...] * rk_ref[...], ones_bd) * vv
    y_rwkv = ((yn + bonus) * g_ref[...]).astype(BF16)
    branch_a = _dot(y_rwkv, wro_ref[...])

    first = (pl.program_id(0) % tpb) == 0
    halo = 16
    pool_scr[0:halo, :] = jnp.where(first, 0.0, halo_ref[...])
    pool_scr[halo:, :] = pool_ref[...]
    pos = (pl.program_id(0) % tpb) * tm + lax.broadcasted_iota(jnp.int32, (tm, 1), 0)
    gw = pool_ref.shape[1] // len(POOL_WINDOWS)
    dg = d // len(POOL_WINDOWS)
    outs = []
    for gi, win in enumerate(POOL_WINDOWS):
        cols = slice(gi * gw, (gi + 1) * gw)
        u = pool_scr[halo:, cols]
        wsum = u
        for s in range(1, win):
            wsum = wsum + pool_scr[halo - s:halo - s + tm, cols]
        cnt = jnp.minimum(pos + 1, win).astype(F32)
        dd = (wsum / cnt - u).astype(BF16)
        outs.append(_dot(dd, wp_ref[gi]))
    branch_b = jnp.concatenate(outs, axis=1) * psc_ref[...]

    gates = gate_ref[...]
    mixin = _sigmoid(gates[:, :d]) * branch_a + _sigmoid(gates[:, d:]) * branch_b
    mix = _dot(mixin.astype(BF16), wo_ref[...])
    xr = alpha * x_ref[...] + g1_ref[0] * mix
    m1 = jnp.mean(xr, axis=-1, keepdims=True)
    xc = xr - m1
    v1 = jnp.mean(xc * xc, axis=-1, keepdims=True)
    x1 = xc * lax.rsqrt(v1 + LN_EPS) * l1g_ref[...] + l1b_ref[...]
    x1_out[...] = x1
    h2 = x1 * (1.0 + sc2_ref[0]) + sh2_ref[0]
    h2_b = h2.astype(BF16)
    h2_r = h2_b.astype(F32)
    lo = lax.shift_right_logical(lax.bitcast_convert_type(h2_r[:, :d // 2], jnp.int32), 16)
    hi = lax.bitcast_convert_type(h2_r[:, d // 2:], jnp.int32) & jnp.int32(-65536)
    h2_out[...] = lo | hi
    lg_out[...] = _dot(h2_b, wr_ref[...]) + br_ref[...]


def _post(y, r, k, v, g, p, x2, g1, sc2, sh2, r_k, lnx_g, lnx_b, ones_bd, w_ro, w_pool, pool_scale,
          w_out, ln1_g, ln1_b, w_r, b_r, seq, alpha, gate_blk, pool_blk):
    m, d = x2.shape
    c = y.shape[1]
    pw = w_pool.shape[0] * w_pool.shape[1]
    tm = 256
    tpb = seq // tm
    row = lambda a: a.reshape(1, -1)
    const = lambda a: pl.BlockSpec(a.shape, lambda i: (0,) * a.ndim, pipeline_mode=pl.Buffered(1))
    tile = lambda w: pl.BlockSpec((tm, w), lambda i: (i, 0))
    bvec = pl.BlockSpec((1, 1, d), lambda i: (i // tpb, 0, 0))
    consts = (row(r_k), row(lnx_g), row(lnx_b), ones_bd, w_ro, w_pool, row(pool_scale), w_out,
              row(ln1_g), row(ln1_b), w_r, row(b_r))
    in_specs = [tile(c)] * 5 + [
        pl.BlockSpec((tm, 2 * d), lambda i: (i, gate_blk)),
        pl.BlockSpec((tm, pw), lambda i: (i, pool_blk)),
        pl.BlockSpec((16, pw), lambda i: (jnp.maximum(i * (tm // 16) - 1, 0), pool_blk)),
        tile(d), bvec, bvec, bvec] + [const(a) for a in consts]
    return pl.pallas_call(
        functools.partial(_post_kernel, tpb=tpb, alpha=alpha, d=d, tm=tm),
        grid=(m // tm,),
        in_specs=in_specs,
        out_specs=[tile(d), tile(d // 2), tile(LANES)],
        out_shape=[jax.ShapeDtypeStruct((m, d), F32), jax.ShapeDtypeStruct((m, d // 2), jnp.int32),
                   jax.ShapeDtypeStruct((m, LANES), F32)],
        scratch_shapes=[pltpu.VMEM((tm + 16, pw), F32)],
        compiler_params=_cparams("arbitrary"),
        name="merge_ln1",
    )(y, r, k, v, g, p, p, p, x2, g1, sc2, sh2, *consts)


def _route_kernel(lg_ref, idx_out, prob_out, rank_out, cnt_out, carry, *, n_exp, tm):
    @pl.when(pl.program_id(0) == 0)
    def _():
        carry[...] = jnp.zeros_like(carry)

    lane = lax.broadcasted_iota(jnp.int32, (tm, LANES), 1)
    lane_f = lane.astype(F32)
    neg = jnp.float32(-jnp.inf)
    work = jnp.where(lane < n_exp, lg_ref[...], neg)
    vals, sels = [], []
    idx_acc = jnp.zeros((tm, LANES), jnp.int32)
    hot = jnp.zeros((tm, LANES), F32)
    for kk in range(TOP_K):
        mx = jnp.max(work, axis=-1, keepdims=True)
        idx = jnp.min(jnp.where(work == mx, lane_f, float(LANES)), axis=-1, keepdims=True).astype(jnp.int32)
        sel = lane == idx
        work = jnp.where(sel, neg, work)
        vals.append(mx)
        sels.append(sel)
        idx_acc = jnp.where(lane == kk, idx, idx_acc)
        hot = hot + sel.astype(F32)
    es = [jnp.exp(vv - vals[0]) for vv in vals]
    den = es[0] + es[1] + es[2] + es[3]
    prob = jnp.zeros((tm, LANES), F32)
    for kk in range(TOP_K):
        prob = jnp.where(lane == kk, es[kk] / den, prob)
    ri = lax.broadcasted_iota(jnp.int32, (tm, tm), 0)
    ci = lax.broadcasted_iota(jnp.int32, (tm, tm), 1)
    before = _dot((ri > ci).astype(BF16), hot.astype(BF16)) + carry[0:1, :]
    rank = jnp.zeros((tm, LANES), jnp.int32)
    for kk in range(TOP_K):
        rk = jnp.sum(jnp.where(sels[kk], before, 0.0), axis=-1, keepdims=True).astype(jnp.int32)
        rank = jnp.where(lane == kk, rk, rank)
    carry[...] = carry[...] + jnp.sum(hot, axis=0, keepdims=True)
    idx_out[...] = idx_acc
    prob_out[...] = prob
    rank_out[...] = rank
    cnt_out[...] = carry[...]


def _route(logits, n_exp):
    m = logits.shape[0]
    tm = 512
    tile = pl.BlockSpec((tm, LANES), lambda i: (i, 0))
    return pl.pallas_call(
        functools.partial(_route_kernel, n_exp=n_exp, tm=tm),
        grid=(m // tm,),
        in_specs=[tile],
        out_specs=[tile, tile, tile, pl.BlockSpec((8, LANES), lambda i: (0, 0))],
        out_shape=[jax.ShapeDtypeStruct((m, LANES), jnp.int32), jax.ShapeDtypeStruct((m, LANES), F32),
                   jax.ShapeDtypeStruct((m, LANES), jnp.int32), jax.ShapeDtypeStruct((8, LANES), F32)],
        scratch_shapes=[pltpu.VMEM((8, LANES), F32)],
        compiler_params=_cparams("arbitrary"),
        name="route",
    )(logits)


def _row_copy(src_hbm, dst, src_row, dst_row, sem):
    return pltpu.make_async_copy(src_hbm.at[pl.ds(src_row, 1), :], dst.at[pl.ds(dst_row, 1), :], sem)


def _dispatch_kernel(dest_ref, pend_ref, h_ref, x_hbm, stage, zero, zsem, sems, *, tb, n_exp, sub, n_rows):
    i = pl.program_id(0)
    slot = i % 2
    stage[slot] = h_ref[...]

    def zero_copy(row0):
        return pltpu.make_async_copy(zero, x_hbm.at[pl.ds(pl.multiple_of(row0, sub), sub), :], zsem)

    def zero_fill(fn):
        for e in range(n_exp):
            lo_e = pend_ref[e - 1] if e > 0 else 0

            @pl.when(pend_ref[e] > lo_e)
            def _():
                fn(zero_copy(pend_ref[e] - sub))

        def body(blk, carry):
            fn(zero_copy(blk * sub))
            return carry

        lax.fori_loop(pend_ref[n_exp - 1] // sub, n_rows // sub, body, 0)

    @pl.when(i == 0)
    def _():
        zero[...] = jnp.zeros_like(zero)
        zero_fill(lambda cp: cp.start())
        zero_fill(lambda cp: cp.wait())

    def issue(u, carry):
        t = i * tb + u
        for kk in range(TOP_K):
            _row_copy(stage.at[slot], x_hbm, u, dest_ref[t * TOP_K + kk], sems.at[slot]).start(priority=kk % 2)
        return carry

    lax.fori_loop(0, tb, issue, 0, unroll=2)

    def drain_on(slot_):
        def body(u, carry):
            for kk in range(TOP_K):
                _row_copy(stage.at[slot_], x_hbm, 0, 0, sems.at[slot_]).wait()
            return carry

        lax.fori_loop(0, tb, body, 0, unroll=2)

    @pl.when(i > 0)
    def _():
        drain_on(1 - slot)

    @pl.when(i == pl.num_programs(0) - 1)
    def _():
        drain_on(slot)


def _dispatch(dest, pad_end, h2p, n_rows):
    m, w = h2p.shape
    tb = 256
    n_exp = pad_end.shape[0]
    return pl.pallas_call(
        functools.partial(_dispatch_kernel, tb=tb, n_exp=n_exp, sub=MOE_SUB, n_rows=n_rows),
        grid_spec=pltpu.PrefetchScalarGridSpec(
            num_scalar_prefetch=2,
            grid=(m // tb,),
            in_specs=[pl.BlockSpec((tb, w), lambda i, dst, pend: (i, 0))],
            out_specs=pl.BlockSpec(memory_space=pl.ANY),
            scratch_shapes=[pltpu.VMEM((2, tb, w), jnp.int32), pltpu.VMEM((MOE_SUB, w), jnp.int32),
                            pltpu.SemaphoreType.DMA(()), pltpu.SemaphoreType.DMA((2,))]),
        out_shape=jax.ShapeDtypeStruct((n_rows, w), jnp.int32),
        compiler_params=_cparams("arbitrary"),
        name="dispatch",
    )(dest, pad_end, h2p)


def _roll_lanes(v, shift):
    return jnp.concatenate([pltpu.roll(v[:, g * LANES:(g + 1) * LANES], shift, axis=1)
                            for g in range(v.shape[1] // LANES)], axis=1)


def _expert_kernel(e_ref, st_ref, ns_ref, tot_ref, x_hbm, wgu_ref, bgu_ref, wd_ref, bd_ref,
                   y_hbm, x_ref, acc, xw, wd_i, wd_b, act, sem, x_sems, *, n_ff, rmax, sub, n_rows):
    i = pl.program_id(0)
    j = pl.program_id(1)
    nsub = ns_ref[i]
    n_sub_max = rmax // sub
    tfc = wd_ref.shape[1]
    half = tfc // 2

    def out_copy(row0, s):
        return pltpu.make_async_copy(acc.at[pl.ds(s * sub, sub), :],
                                     y_hbm.at[pl.ds(pl.multiple_of(row0, sub), sub), :], sem)

    def for_valid_subs(item, fn):
        for s in range(n_sub_max):
            @pl.when(s < ns_ref[item])
            def _():
                fn(out_copy(st_ref[item] + s * sub, s))

    half_d = x_ref.shape[1] // 2

    def fetch(s):
        row0 = pl.multiple_of(st_ref[i] + s * sub, sub)
        return pltpu.make_async_copy(x_hbm.at[pl.ds(row0, sub), :], xw.at[s % 2], x_sems.at[s % 2])

    @pl.when(j == 0)
    def _():
        for s in range(2):
            @pl.when(s < nsub)
            def _():
                fetch(s).start()

    @pl.when(nsub > 0)
    def _():
        for g in range(wd_b.shape[1] // LANES):
            cols = slice(g * LANES, (g + 1) * LANES)
            wd_i[g % 2, pl.ds(0, half, stride=2), :] = wd_ref[0, 0:half, cols]
            wd_i[g % 2, pl.ds(1, half, stride=2), :] = wd_ref[0, half:tfc, cols]
            wd_b[:, cols] = wd_i[g % 2].astype(BF16)

    @pl.when(j == 0)
    def _():
        for s in range(n_sub_max):
            @pl.when(s < nsub)
            def _():
                fetch(s).wait()
                words = xw[s % 2]
                rows = slice(s * sub, (s + 1) * sub)
                x_ref[rows, 0:half_d] = lax.bitcast_convert_type(words << 16, F32).astype(BF16)
                x_ref[rows, half_d:] = lax.bitcast_convert_type(words & jnp.int32(-65536), F32).astype(BF16)

                @pl.when(s + 2 < nsub)
                def _():
                    fetch(s + 2).start()

    cw = 2 * LANES

    def paired(gu, to_odd):
        gate = jnp.minimum(gu, SWIGLU_LIMIT)
        glu = gate * _sigmoid(gate * SWIGLU_ALPHA)
        up1 = jnp.clip(gu, -SWIGLU_LIMIT, SWIGLU_LIMIT) + 1.0
        if to_odd:
            return _roll_lanes(glu, 1) * up1
        return glu * _roll_lanes(up1, LANES - 1)

    def activate(row0, m_rows):
        rows = pl.ds(pl.multiple_of(row0, sub), m_rows)
        xm = x_ref[rows, :]
        even = (lax.broadcasted_iota(jnp.int32, (m_rows, cw), 1) % 2) == 0
        for c in range(tfc // cw):
            lo = slice(c * cw, (c + 1) * cw)
            hi = slice(tfc + c * cw, tfc + (c + 1) * cw)
            pa = paired(_dot(xm, wgu_ref[0, :, lo].astype(BF16)) + bgu_ref[0, :, lo], False)
            pb = paired(_dot(xm, wgu_ref[0, :, hi].astype(BF16)) + bgu_ref[0, :, hi], True)
            act[rows, lo] = jnp.where(even, pa, pb).astype(BF16)

    def project(row0, m_rows, first):
        rows = pl.ds(pl.multiple_of(row0, sub), m_rows)
        down = _dot(act[rows, :], wd_b[...])
        if first:
            acc[rows, :] = down + bd_ref[0]
        else:
            acc[rows, :] += down

    def per_row_count(fn):
        done = 0
        for width in (w for w in (8, 4, 2, 1) if w * sub <= 1024):
            count = (nsub - done) // width

            def body(b, carry, done=done, width=width):
                fn((done + b * width) * sub, width * sub)
                return carry

            lax.fori_loop(0, count, body, 0)
            done = done + count * width

    per_row_count(activate)

    @pl.when(j == 0)
    def _():
        @pl.when(i > 0)
        def _():
            for_valid_subs(i - 1, lambda cp: cp.wait())

        per_row_count(functools.partial(project, first=True))

    @pl.when(j > 0)
    def _():
        per_row_count(functools.partial(project, first=False))

    @pl.when(j == n_ff - 1)
    def _():
        for_valid_subs(i, lambda cp: cp.start())

        @pl.when(i == pl.num_programs(0) - 1)
        def _():
            for_valid_subs(i, lambda cp: cp.wait())
            acc[0:sub, :] = jnp.zeros((sub, acc.shape[1]), F32)
            first = tot_ref[0] // sub

            def fill_start(blk, carry):
                out_copy(blk * sub, 0).start()
                return carry

            def fill_wait(blk, carry):
                out_copy(blk * sub, 0).wait()
                return carry

            lax.fori_loop(first, n_rows // sub, fill_start, 0)
            lax.fori_loop(first, n_rows // sub, fill_wait, 0)


def _experts(item_e, item_start, item_nsub, total_rows, x_buf, w_gu, b_gu, w_down, b_down, n_rows):
    n_items = item_e.shape[0]
    n_exp, dff, d = w_down.shape
    tf = MOE_TF
    n_ff = dff // tf
    rmax = MOE_RMAX

    def jj(i, j, ns):
        return jnp.where(ns[i] > 0, j, n_ff - 1)

    in_specs = [
        pl.BlockSpec(memory_space=pl.ANY),
        pl.BlockSpec((1, d, 2 * tf), lambda i, j, e, st, ns, tot: (e[i], 0, jj(i, j, ns))),
        pl.BlockSpec((1, 1, 2 * tf), lambda i, j, e, st, ns, tot: (e[i], 0, jj(i, j, ns))),
        pl.BlockSpec((1, tf, d), lambda i, j, e, st, ns, tot: (e[i], jj(i, j, ns), 0)),
        pl.BlockSpec((1, 1, d), lambda i, j, e, st, ns, tot: (e[i], 0, 0)),
    ]
    return pl.pallas_call(
        functools.partial(_expert_kernel, n_ff=n_ff, rmax=rmax, sub=MOE_SUB, n_rows=n_rows),
        grid_spec=pltpu.PrefetchScalarGridSpec(
            num_scalar_prefetch=4,
            grid=(n_items, n_ff),
            in_specs=in_specs,
            out_specs=pl.BlockSpec(memory_space=pl.ANY),
            scratch_shapes=[pltpu.VMEM((rmax, d), BF16), pltpu.VMEM((rmax, d), F32),
                            pltpu.VMEM((2, MOE_SUB, d // 2), jnp.int32), pltpu.VMEM((2, tf, LANES), F32),
                            pltpu.VMEM((tf, d), BF16), pltpu.VMEM((rmax, tf), BF16),
                            pltpu.SemaphoreType.DMA(()), pltpu.SemaphoreType.DMA((2,))]),
        out_shape=jax.ShapeDtypeStruct((n_rows, d), F32),
        compiler_params=_cparams("arbitrary", "arbitrary"),
        name="experts",
    )(item_e, item_start, item_nsub, total_rows, x_buf, w_gu, b_gu.reshape(n_exp, 1, 2 * dff),
      w_down, b_down.reshape(n_exp, 1, d))


def _combine_kernel(dest_ref, y_hbm, prob_ref, x1_ref, g2_ref, lg_ref, lb_ref, o_ref, buf, sems,
                    *, tm, alpha):
    i = pl.program_id(0)
    slot = i % 2

    def issue_block(blk, slot_):
        def body(t, carry):
            for kk in range(TOP_K):
                _row_copy(y_hbm, buf.at[slot_, kk], dest_ref[(blk * tm + t) * TOP_K + kk], t,
                          sems.at[slot_]).start(priority=kk % 2)
            return carry

        lax.fori_loop(0, tm, body, 0, unroll=2)

    @pl.when(i == 0)
    def _():
        issue_block(0, 0)

    @pl.when(i + 1 < pl.num_programs(0))
    def _():
        issue_block(i + 1, 1 - slot)

    def drain(t, carry):
        for kk in range(TOP_K):
            _row_copy(y_hbm, buf.at[slot, kk], 0, t, sems.at[slot]).wait()
        return carry

    lax.fori_loop(0, tm, drain, 0, unroll=2)
    prob = prob_ref[...]
    f = prob[:, 0:1] * buf[slot, 0]
    for kk in range(1, TOP_K):
        f = f + prob[:, kk:kk + 1] * buf[slot, kk]
    xr = alpha * x1_ref[...] + g2_ref[0] * f
    m2 = jnp.mean(xr, axis=-1, keepdims=True)
    xc = xr - m2
    v2 = jnp.mean(xc * xc, axis=-1, keepdims=True)
    o_ref[...] = xc * lax.rsqrt(v2 + LN_EPS) * lg_ref[...] + lb_ref[...]


def _combine(dest, y_buf, prob, x1, g2, ln_g, ln_b, seq, alpha):
    m, d = x1.shape
    tm = 128
    tpb = seq // tm
    return pl.pallas_call(
        functools.partial(_combine_kernel, tm=tm, alpha=alpha),
        grid_spec=pltpu.PrefetchScalarGridSpec(
            num_scalar_prefetch=1,
            grid=(m // tm,),
            in_specs=[pl.BlockSpec(memory_space=pl.ANY),
                      pl.BlockSpec((tm, LANES), lambda i, dst: (i, 0)),
                      pl.BlockSpec((tm, d), lambda i, dst: (i, 0)),
                      pl.BlockSpec((1, 1, d), lambda i, dst: (i // tpb, 0, 0)),
                      pl.BlockSpec((1, d), lambda i, dst: (0, 0)),
                      pl.BlockSpec((1, d), lambda i, dst: (0, 0))],
            out_specs=pl.BlockSpec((tm, d), lambda i, dst: (i, 0)),
            scratch_shapes=[pltpu.VMEM((2, TOP_K, tm, d), F32), pltpu.SemaphoreType.DMA((2,))]),
        out_shape=jax.ShapeDtypeStruct((m, d), F32),
        compiler_params=_cparams("arbitrary"),
        name="combine_ln2",
    )(dest, y_buf, prob, x1, g2, ln_g.reshape(1, d), ln_b.reshape(1, d))


def _pad_rows(w, rows):
    return jnp.pad(w, ((0, rows - w.shape[0]), (0, 0)))


def _pad_cols(w, cols):
    return jnp.pad(w, ((0, 0), (0, cols - w.shape[1])))


def _layer(x2, c8, bsz, seq, alpha, w_ada, b_ada, w_in, mu_shift, w0, w_decay_up, a0, w_iclr_up,
           w_glora_up, k_k, k_a, r_k, lnx_g, lnx_b, w_rwkv_out, w_pool, pool_scale, w_out, ln1_g,
           ln1_b, w_router, b_router, w_gu, b_gu, w_down, b_down, ln2_g, ln2_b):
    m, d = x2.shape
    c = w0.shape[0]
    n_dec, n_icl, n_gate = w_decay_up.shape[0], w_iclr_up.shape[0], w_glora_up.shape[0]
    pw = w_pool.shape[0] * w_pool.shape[1]
    n_exp = w_router.shape[1]
    assert n_dec <= LANES and n_icl <= LANES and n_gate <= 2 * LANES and d == 2 * c and pw == c

    mod = _adaln(c8, w_ada, b_ada)[:bsz]
    sh1, sc1, g1, sh2, sc2, g2 = [t.reshape(bsz, 1, d) for t in jnp.split(mod, 6, axis=-1)]

    o = 3 * c
    segs = [(o, n_dec, LANES), (o + n_dec, n_icl, LANES), (o + n_dec + n_icl, n_gate, 2 * LANES)]
    n_shift = o + n_dec + n_icl + n_gate
    zpad = 2 * d - 3 * c - 4 * LANES
    w_shift = jnp.concatenate([w_in[:, :o]] + [_pad_cols(w_in[:, s:s + n], wd) for s, n, wd in segs]
                              + [jnp.zeros((d, zpad), F32)], axis=1)
    mu = jnp.concatenate([mu_shift[:o]] + [jnp.pad(mu_shift[s:s + n], (0, wd - n)) for s, n, wd in segs]
                         + [jnp.zeros((zpad,), F32)]).reshape(1, -1)
    w_cat = jnp.concatenate([w_in[:, n_shift + pw:], w_shift, w_in[:, n_shift:n_shift + pw]],
                            axis=1).astype(BF16)
    p = _inproj(x2, sc1, sh1, w_cat, seq)

    ones_bd = jnp.kron(jnp.eye(LANES // HEAD_SIZE, dtype=F32), jnp.ones((HEAD_SIZE, HEAD_SIZE), F32)).astype(BF16)
    r, lw, kmod, v, kkn, bvec, g = _prep(
        p, mu, w0, a0, k_k, k_a, _pad_rows(w_decay_up, LANES).astype(BF16),
        _pad_rows(w_iclr_up, LANES).astype(BF16), _pad_rows(w_glora_up, 2 * LANES).astype(BF16),
        ones_bd, seq, c, shift_blk=1)
    y = _wkv(r, lw, kmod, v, kkn, bvec, bsz, seq, c)

    w_r = _pad_cols(w_router, LANES).astype(BF16)
    b_r = jnp.pad(b_router, (0, LANES - n_exp))
    x1, h2, logits = _post(y, r, kmod, v, g, p, x2, g1, sc2, sh2, r_k.reshape(-1), lnx_g, lnx_b, ones_bd,
                           w_rwkv_out.astype(BF16), w_pool.astype(BF16), pool_scale, w_out.astype(BF16),
                           ln1_g, ln1_b, w_r, b_r, seq, alpha, gate_blk=0, pool_blk=(4 * d) // pw)

    idx, prob, rank, cnt = _route(logits, n_exp)
    counts = cnt[0, :n_exp].astype(jnp.int32)
    padded = (counts + MOE_SUB - 1) // MOE_SUB * MOE_SUB
    pad_end = jnp.cumsum(padded)
    pad_start = pad_end - padded
    mk = m * TOP_K
    n_rows = (mk // MOE_SUB + n_exp) * MOE_SUB
    dest = (pad_start[idx[:, :TOP_K]] + rank[:, :TOP_K]).reshape(mk).astype(jnp.int32)
    n_items = mk // MOE_RMAX + n_exp
    per_e = (padded + MOE_RMAX - 1) // MOE_RMAX
    it_end = jnp.cumsum(per_e)
    it_start = it_end - per_e
    ids = jnp.arange(n_items, dtype=jnp.int32)
    item_e = jnp.minimum(jnp.searchsorted(it_end, ids, side='right'), n_exp - 1).astype(jnp.int32)
    local = ids - it_start[item_e]
    used = ids < it_end[-1]
    item_start = jnp.where(used, pad_start[item_e] + local * MOE_RMAX, 0).astype(jnp.int32)
    item_nsub = jnp.where(used, jnp.clip((padded[item_e] - local * MOE_RMAX) // MOE_SUB, 0,
                                         MOE_RMAX // MOE_SUB), 0).astype(jnp.int32)

    x_buf = _dispatch(dest, pad_end.astype(jnp.int32), h2, n_rows)
    y_buf = _experts(item_e, item_start, item_nsub, pad_end[-1:].astype(jnp.int32), x_buf,
                     w_gu, b_gu, w_down, b_down, n_rows)
    return _combine(dest, y_buf, prob, x1, g2, ln2_g, ln2_b, seq, alpha)


def kernel(x, c, w_ada, b_ada, w_in, mu_shift, w0, w_decay_up, a0, w_iclr_up, w_glora_up, k_k, k_a, r_k, lnx_g, lnx_b, w_rwkv_out, w_pool, pool_scale, w_out, ln1_g, ln1_b, w_router, b_router, w_gu, b_gu, w_down, b_down, ln2_g, ln2_b):
    bsz, seq, d = x.shape
    depth = w_ada.shape[0]
    alpha = float((2 * depth) ** 0.25)
    c8 = jnp.pad(c, ((0, 8 - bsz), (0, 0)))
    x2 = x.reshape(bsz * seq, d)
    weights = (w_ada, b_ada, w_in, mu_shift, w0, w_decay_up, a0, w_iclr_up, w_glora_up, k_k, k_a, r_k,
               lnx_g, lnx_b, w_rwkv_out, w_pool, pool_scale, w_out, ln1_g, ln1_b, w_router, b_router,
               w_gu, b_gu, w_down, b_down, ln2_g, ln2_b)
    for l in range(depth):
        x2 = _layer(x2, c8, bsz, seq, alpha, *[w[l] for w in weights])
    return x2.reshape(bsz, seq, d)
```

```python
import functools

import jax
import jax.numpy as jnp
from jax import lax
from jax.experimental import pallas as pl
from jax.experimental.pallas import tpu as pltpu

F32 = jnp.float32
BF16 = jnp.bfloat16

HEAD_SIZE = 64
LANES = 128
TOP_K = 4
LN_EPS = 1e-5
GN_EPS = 64e-5
POOL_WINDOWS = (2, 4, 8, 16)
SWIGLU_LIMIT = 7.0
SWIGLU_ALPHA = 1.702
WKV_CHUNK = 64
MOE_SUB = 256
MOE_RMAX = 1280
MOE_TF = 512
VMEM_LIMIT = 54 * 1024 * 1024

_NT = (((1,), (1,)), ((), ()))
_TN = (((0,), (0,)), ((), ()))


def _cparams(*sem):
    return pltpu.CompilerParams(dimension_semantics=sem, vmem_limit_bytes=VMEM_LIMIT)


def _sigmoid(z):
    return 0.5 * jnp.tanh(0.5 * z) + 0.5


def _dot(a, b):
    return jnp.dot(a, b, preferred_element_type=F32)


def _split_dot(xv, w):
    hi = xv.astype(BF16)
    lo = (xv - hi.astype(F32)).astype(BF16)
    return _dot(hi, w) + _dot(lo, w)


def _head_sum(xv, ones_bd):
    outs = []
    for p in range(xv.shape[1] // LANES):
        outs.append(_split_dot(xv[:, p * LANES:(p + 1) * LANES], ones_bd))
    return jnp.concatenate(outs, axis=1)


def _adaln_kernel(c_ref, w_ref, b_ref, o_ref):
    c = c_ref[...]
    cond = c * _sigmoid(c)
    o_ref[...] = _dot(cond.astype(BF16), w_ref[...].astype(BF16)) + b_ref[...]


def _adaln(c8, w, b):
    d, n = w.shape
    tn = 1024
    return pl.pallas_call(
        _adaln_kernel,
        grid=(n // tn,),
        in_specs=[pl.BlockSpec((8, d), lambda j: (0, 0)),
                  pl.BlockSpec((d, tn), lambda j: (0, j)),
                  pl.BlockSpec((1, tn), lambda j: (0, j))],
        out_specs=pl.BlockSpec((8, tn), lambda j: (0, j)),
        out_shape=jax.ShapeDtypeStruct((8, n), F32),
        compiler_params=_cparams("arbitrary"),
        name="adaln",
    )(c8, w, b.reshape(1, n))


def _inproj_kernel(x_ref, sc_ref, sh_ref, w_ref, o_ref, h_scr):
    @pl.when(pl.program_id(1) == 0)
    def _():
        h_scr[...] = (x_ref[...] * (1.0 + sc_ref[0]) + sh_ref[0]).astype(BF16)

    o_ref[...] = _dot(h_scr[...], w_ref[...])


def _inproj(x2, sc, sh, w, seq):
    m, d = x2.shape
    n = w.shape[1]
    tm, tn = 1024, 1536
    tpb = seq // tm
    return pl.pallas_call(
        _inproj_kernel,
        grid=(m // tm, n // tn),
        in_specs=[pl.BlockSpec((tm, d), lambda i, j: (i, 0)),
                  pl.BlockSpec((1, 1, d), lambda i, j: (i // tpb, 0, 0)),
                  pl.BlockSpec((1, 1, d), lambda i, j: (i // tpb, 0, 0)),
                  pl.BlockSpec((d, tn), lambda i, j: (0, j))],
        out_specs=pl.BlockSpec((tm, tn), lambda i, j: (i, j)),
        out_shape=jax.ShapeDtypeStruct((m, n), F32),
        scratch_shapes=[pltpu.VMEM((tm, d), BF16)],
        compiler_params=_cparams("arbitrary", "arbitrary"),
        name="inproj",
    )(x2, sc, sh, w)


def _prep_kernel(p_ref, prev_ref, mu_ref, w0_ref, a0_ref, kk_ref, ka_ref, wd_ref, wa_ref, wg_ref,
                 ones_ref, r_out, lw_out, k_out, v_out, kkn_out, b_out, g_out, *, tpb, c):
    cur = p_ref[...]
    first = (pl.program_id(0) % tpb) == 0
    prev_row = jnp.where(first, 0.0, prev_ref[7:8, :])
    shifted = pltpu.roll(cur, 1, axis=0)
    row0 = lax.broadcasted_iota(jnp.int32, cur.shape, 0) == 0
    shifted = jnp.where(row0, prev_row, shifted)
    ps = cur + (shifted - cur) * mu_ref[...]
    r = ps[:, 0:c]
    k = ps[:, c:2 * c]
    v = ps[:, 2 * c:3 * c]
    o = 3 * c
    wdp = ps[:, o:o + LANES]
    adp = ps[:, o + LANES:o + 2 * LANES]
    gdp = ps[:, o + 2 * LANES:o + 4 * LANES]
    z = w0_ref[...] + _dot(jnp.tanh(wdp).astype(BF16), wd_ref[...])
    lw = -jnp.exp(-0.5) * _sigmoid(z)
    iclr = _sigmoid(a0_ref[...] + _dot(adp.astype(BF16), wa_ref[...]))
    g = _dot(_sigmoid(gdp).astype(BF16), wg_ref[...])
    kk0 = k * kk_ref[...]
    ss = _head_sum(kk0 * kk0, ones_ref[...])
    kkn = kk0 / jnp.maximum(jnp.sqrt(ss), 1e-12)
    kmod = k * (1.0 + (iclr - 1.0) * ka_ref[...])
    r_out[...] = r
    lw_out[...] = lw
    k_out[...] = kmod
    v_out[...] = v
    kkn_out[...] = kkn
    b_out[...] = kkn * iclr
    g_out[...] = g


def _prep(p, mu, w0, a0, k_k, k_a, wd, wa, wg, ones_bd, seq, c, shift_blk):
    m = p.shape[0]
    ns = mu.shape[1]
    tm = 256
    tpb = seq // tm
    row = lambda a: a.reshape(1, -1)
    full = lambda a: pl.BlockSpec(a.shape, lambda i: (0,) * a.ndim)
    out = jax.ShapeDtypeStruct((m, c), F32)
    ospec = pl.BlockSpec((tm, c), lambda i: (i, 0))
    args = (p, p, mu, row(w0), row(a0), row(k_k), row(k_a), wd, wa, wg, ones_bd)
    in_specs = [pl.BlockSpec((tm, ns), lambda i: (i, shift_blk)),
                pl.BlockSpec((8, ns), lambda i: (jnp.maximum(i * (tm // 8) - 1, 0), shift_blk))]
    in_specs += [full(a) for a in args[2:]]
    return pl.pallas_call(
        functools.partial(_prep_kernel, tpb=tpb, c=c),
        grid=(m // tm,),
        in_specs=in_specs,
        out_specs=[ospec] * 7,
        out_shape=[out] * 7,
        compiler_params=_cparams("arbitrary"),
        name="rwkv_prep",
    )(*args)


def _wkv_kernel(r_ref, lw_ref, k_ref, v_ref, kk_ref, b_ref, y_ref, s_scr, *, chunk, c):
    L = chunk
    bsz = r_ref.shape[0]

    @pl.when(pl.program_id(0) == 0)
    def _():
        s_scr[...] = jnp.zeros_like(s_scr)

    def lanes(ref):
        return jnp.concatenate([ref[bi] for bi in range(bsz)], axis=1)

    ri = lax.broadcasted_iota(jnp.int32, (L, L), 0)
    ci = lax.broadcasted_iota(jnp.int32, (L, L), 1)
    tri = (ri >= ci).astype(BF16)
    lw = lanes(lw_ref)
    cs = _split_dot_lhs(tri, lw)
    cs_last = cs[L - 1:L, :]
    g_in = jnp.exp(cs)
    g_ex = jnp.exp(cs - lw)
    g_inv = jnp.exp(-cs)
    g_tail = jnp.exp(cs_last - cs)
    g_last = jnp.exp(cs_last)
    kk = lanes(kk_ref)
    bb = lanes(b_ref)
    kx = lanes(k_ref)
    vv = lanes(v_ref)
    a_t = -kk * g_ex
    r_t = lanes(r_ref) * g_in
    b_t = bb * g_inv
    k_t = kx * g_inv
    b_p = bb * g_tail
    k_p = kx * g_tail

    r2 = lax.broadcasted_iota(jnp.int32, (2 * L, 2 * L), 0)
    c2 = lax.broadcasted_iota(jnp.int32, (2 * L, 2 * L), 1)
    same = (r2 >= L) == (c2 >= L)
    m_strict = same & (r2 > c2)
    m_incl = same & (r2 >= c2)
    lane = lax.broadcasted_iota(jnp.int32, (L, LANES), 1)
    low = lane < HEAD_SIZE

    def stack(zv):
        return jnp.concatenate([jnp.where(low, zv, 0.0), jnp.where(low, 0.0, zv)], axis=0)

    per_seq = c // LANES
    pairs = range(bsz * per_seq)
    sls = [slice(p * LANES, (p + 1) * LANES) for p in pairs]
    a_s = [stack(a_t[:, sl]) for sl in sls]
    r_s = [stack(r_t[:, sl]) for sl in sls]
    v_s = [stack(vv[:, sl]) for sl in sls]
    ar = [jnp.concatenate([a_s[p], r_s[p]], axis=0).astype(BF16) for p in pairs]
    qb = [lax.dot_general(ar[p], stack(b_t[:, sls[p]]).astype(BF16), _NT, preferred_element_type=F32)
          for p in pairs]
    qk = [lax.dot_general(ar[p], stack(k_t[:, sls[p]]).astype(BF16), _NT, preferred_element_type=F32)
          for p in pairs]
    a_ab = [jnp.where(m_strict, q[:2 * L], 0.0) for q in qb]
    a_rb = [jnp.where(m_incl, q[2 * L:], 0.0) for q in qb]
    a_ak = [jnp.where(m_strict, q[:2 * L], 0.0) for q in qk]
    a_rk = [jnp.where(m_incl, q[2 * L:], 0.0) for q in qk]
    v_b = [x.astype(BF16) for x in v_s]
    z = [jnp.concatenate([_dot(a_ak[p].astype(BF16), v_b[p]), a_s[p]], axis=1) for p in pairs]
    amat = a_ab
    n_sq = max(L.bit_length() - 1, 1)
    for it in range(n_sq):
        ab = [x.astype(BF16) for x in amat]
        z = [z[p] + _dot(ab[p], z[p].astype(BF16)) for p in pairs]
        if it + 1 < n_sq:
            amat = [_dot(x, x) for x in ab]
    s_old = [s_scr[p] for p in pairs]
    s_b = [x.astype(BF16) for x in s_old]
    u_s = [_dot(z[p][:, LANES:].astype(BF16), s_b[p]) + z[p][:, :LANES] for p in pairs]
    uv = [jnp.concatenate([u_s[p], v_s[p]], axis=0).astype(BF16) for p in pairs]
    y_s = [_dot(r_s[p].astype(BF16), s_b[p])
           + _dot(jnp.concatenate([a_rb[p], a_rk[p]], axis=1).astype(BF16), uv[p]) for p in pairs]
    for p in pairs:
        y_ref[p // per_seq, :, sls[p % per_seq]] = y_s[p][:L] + y_s[p][L:]
    upd = [lax.dot_general(jnp.concatenate([stack(b_p[:, sls[p]]), stack(k_p[:, sls[p]])], axis=0).astype(BF16),
                           uv[p], _TN, preferred_element_type=F32) for p in pairs]
    for p in pairs:
        g_col = jnp.transpose(jnp.broadcast_to(g_last[:, sls[p]], (LANES, LANES)))
        s_scr[p] = s_old[p] * g_col + upd[p]


def _split_dot_lhs(w, xv):
    hi = xv.astype(BF16)
    lo = (xv - hi.astype(F32)).astype(BF16)
    return _dot(w, hi) + _dot(w, lo)


def _wkv(r, lw, k, v, kk, b, bsz, seq, c):
    L = WKV_CHUNK
    spec = pl.BlockSpec((bsz, L, c), lambda ci: (0, ci, 0))
    seqs = lambda a: a.reshape(bsz, seq, c)
    return pl.pallas_call(
        functools.partial(_wkv_kernel, chunk=L, c=c),
        grid=(seq // L,),
        in_specs=[spec] * 6,
        out_specs=spec,
        out_shape=jax.ShapeDtypeStruct((bsz, seq, c), F32),
        scratch_shapes=[pltpu.VMEM((bsz * c // LANES, LANES, LANES), F32)],
        compiler_params=_cparams("arbitrary"),
        name="wkv7",
    )(seqs(r), seqs(lw), seqs(k), seqs(v), seqs(kk), seqs(b)).reshape(bsz * seq, c)


def _post_kernel(y_ref, r_ref, k_ref, v_ref, g_ref, gate_ref, pool_ref, halo_ref, x_ref,
                 g1_ref, sc2_ref, sh2_ref, rk_ref, lng_ref, lnb_ref, ones_ref, wro_ref, wp_ref,
                 psc_ref, wo_ref, l1g_ref, l1b_ref, wr_ref, br_ref,
                 x1_out, h2_out, lg_out, pool_scr, *, tpb, alpha, d, tm):
    ones_bd = ones_ref[...]
    inv_n = 1.0 / HEAD_SIZE
    y = y_ref[...]
    mu = _head_sum(y, ones_bd) * inv_n
    dy = y - mu
    var = _head_sum(dy * dy, ones_bd) * inv_n
    yn = dy * lax.rsqrt(var + GN_EPS) * lng_ref[...] + lnb_ref[...]
    vv = v_ref[...]
    bonus = _head_sum(r_ref[...] * k_ref[...] * rk_ref[...], ones_bd) * vv
    y_rwkv = ((yn + bonus) * g_ref[...]).astype(BF16)
    branch_a = _dot(y_rwkv, wro_ref[...])

    first = (pl.program_id(0) % tpb) == 0
    halo = 16
    pool_scr[0:halo, :] = jnp.where(first, 0.0, halo_ref[...])
    pool_scr[halo:, :] = pool_ref[...]
    pos = (pl.program_id(0) % tpb) * tm + lax.broadcasted_iota(jnp.int32, (tm, 1), 0)
    gw = pool_ref.shape[1] // len(POOL_WINDOWS)
    dg = d // len(POOL_WINDOWS)
    outs = []
    for gi, win in enumerate(POOL_WINDOWS):
        cols = slice(gi * gw, (gi + 1) * gw)
        u = pool_scr[halo:, cols]
        wsum = u
        for s in range(1, win):
            wsum = wsum + pool_scr[halo - s:halo - s + tm, cols]
        cnt = jnp.minimum(pos + 1, win).astype(F32)
        dd = (wsum / cnt - u).astype(BF16)
        outs.append(_dot(dd, wp_ref[gi]))
    branch_b = jnp.concatenate(outs, axis=1) * psc_ref[...]

    gates = gate_ref[...]
    mixin = _sigmoid(gates[:, :d]) * branch_a + _sigmoid(gates[:, d:]) * branch_b
    mix = _dot(mixin.astype(BF16), wo_ref[...])
    xr = alpha * x_ref[...] + g1_ref[0] * mix
    m1 = jnp.mean(xr, axis=-1, keepdims=True)
    xc = xr - m1
    v1 = jnp.mean(xc * xc, axis=-1, keepdims=True)
    x1 = xc * lax.rsqrt(v1 + LN_EPS) * l1g_ref[...] + l1b_ref[...]
    x1_out[...] = x1
    h2 = x1 * (1.0 + sc2_ref[0]) + sh2_ref[0]
    h2_b = h2.astype(BF16)
    h2_r = h2_b.astype(F32)
    lo = lax.shift_right_logical(lax.bitcast_convert_type(h2_r[:, :d // 2], jnp.int32), 16)
    hi = lax.bitcast_convert_type(h2_r[:, d // 2:], jnp.int32) & jnp.int32(-65536)
    h2_out[...] = lo | hi
    lg_out[...] = _dot(h2_b, wr_ref[...]) + br_ref[...]


def _post(y, r, k, v, g, p, x2, g1, sc2, sh2, r_k, lnx_g, lnx_b, ones_bd, w_ro, w_pool, pool_scale,
          w_out, ln1_g, ln1_b, w_r, b_r, seq, alpha, gate_blk, pool_blk):
    m, d = x2.shape
    c = y.shape[1]
    pw = w_pool.shape[0] * w_pool.shape[1]
    tm = 256
    tpb = seq // tm
    row = lambda a: a.reshape(1, -1)
    const = lambda a: pl.BlockSpec(a.shape, lambda i: (0,) * a.ndim, pipeline_mode=pl.Buffered(1))
    tile = lambda w: pl.BlockSpec((tm, w), lambda i: (i, 0))
    bvec = pl.BlockSpec((1, 1, d), lambda i: (i // tpb, 0, 0))
    consts = (row(r_k), row(lnx_g), row(lnx_b), ones_bd, w_ro, w_pool, row(pool_scale), w_out,
              row(ln1_g), row(ln1_b), w_r, row(b_r))
    in_specs = [tile(c)] * 5 + [
        pl.BlockSpec((tm, 2 * d), lambda i: (i, gate_blk)),
        pl.BlockSpec((tm, pw), lambda i: (i, pool_blk)),
        pl.BlockSpec((16, pw), lambda i: (jnp.maximum(i * (tm // 16) - 1, 0), pool_blk)),
        tile(d), bvec, bvec, bvec] + [const(a) for a in consts]
    return pl.pallas_call(
        functools.partial(_post_kernel, tpb=tpb, alpha=alpha, d=d, tm=tm),
        grid=(m // tm,),
        in_specs=in_specs,
        out_specs=[tile(d), tile(d // 2), tile(LANES)],
        out_shape=[jax.ShapeDtypeStruct((m, d), F32), jax.ShapeDtypeStruct((m, d // 2), jnp.int32),
                   jax.ShapeDtypeStruct((m, LANES), F32)],
        scratch_shapes=[pltpu.VMEM((tm + 16, pw), F32)],
        compiler_params=_cparams("arbitrary"),
        name="merge_ln1",
    )(y, r, k, v, g, p, p, p, x2, g1, sc2, sh2, *consts)


def _route_kernel(lg_ref, idx_out, prob_out, rank_out, cnt_out, carry, *, n_exp, tm):
    @pl.when(pl.program_id(0) == 0)
    def _():
        carry[...] = jnp.zeros_like(carry)

    lane = lax.broadcasted_iota(jnp.int32, (tm, LANES), 1)
    lane_f = lane.astype(F32)
    neg = jnp.float32(-jnp.inf)
    work = jnp.where(lane < n_exp, lg_ref[...], neg)
    vals, sels = [], []
    idx_acc = jnp.zeros((tm, LANES), jnp.int32)
    hot = jnp.zeros((tm, LANES), F32)
    for kk in range(TOP_K):
        mx = jnp.max(work, axis=-1, keepdims=True)
        idx = jnp.min(jnp.where(work == mx, lane_f, float(LANES)), axis=-1, keepdims=True).astype(jnp.int32)
        sel = lane == idx
        work = jnp.where(sel, neg, work)
        vals.append(mx)
        sels.append(sel)
        idx_acc = jnp.where(lane == kk, idx, idx_acc)
        hot = hot + sel.astype(F32)
    es = [jnp.exp(vv - vals[0]) for vv in vals]
    den = es[0] + es[1] + es[2] + es[3]
    prob = jnp.zeros((tm, LANES), F32)
    for kk in range(TOP_K):
        prob = jnp.where(lane == kk, es[kk] / den, prob)
    ri = lax.broadcasted_iota(jnp.int32, (tm, tm), 0)
    ci = lax.broadcasted_iota(jnp.int32, (tm, tm), 1)
    before = _dot((ri > ci).astype(BF16), hot.astype(BF16)) + carry[0:1, :]
    rank = jnp.zeros((tm, LANES), jnp.int32)
    for kk in range(TOP_K):
        rk = jnp.sum(jnp.where(sels[kk], before, 0.0), axis=-1, keepdims=True).astype(jnp.int32)
        rank = jnp.where(lane == kk, rk, rank)
    carry[...] = carry[...] + jnp.sum(hot, axis=0, keepdims=True)
    idx_out[...] = idx_acc
    prob_out[...] = prob
    rank_out[...] = rank
    cnt_out[...] = carry[...]


def _route(logits, n_exp):
    m = logits.shape[0]
    tm = 512
    tile = pl.BlockSpec((tm, LANES), lambda i: (i, 0))
    return pl.pallas_call(
        functools.partial(_route_kernel, n_exp=n_exp, tm=tm),
        grid=(m // tm,),
        in_specs=[tile],
        out_specs=[tile, tile, tile, pl.BlockSpec((8, LANES), lambda i: (0, 0))],
        out_shape=[jax.ShapeDtypeStruct((m, LANES), jnp.int32), jax.ShapeDtypeStruct((m, LANES), F32),
                   jax.ShapeDtypeStruct((m, LANES), jnp.int32), jax.ShapeDtypeStruct((8, LANES), F32)],
        scratch_shapes=[pltpu.VMEM((8, LANES), F32)],
        compiler_params=_cparams("arbitrary"),
        name="route",
    )(logits)


def _row_copy(src_hbm, dst, src_row, dst_row, sem):
    return pltpu.make_async_copy(src_hbm.at[pl.ds(src_row, 1), :], dst.at[pl.ds(dst_row, 1), :], sem)


def _dispatch_kernel(dest_ref, pend_ref, h_ref, x_hbm, stage, zero, zsem, sems, *, tb, n_exp, sub, n_rows):
    i = pl.program_id(0)
    slot = i % 2
    stage[slot] = h_ref[...]

    def zero_copy(row0):
        return pltpu.make_async_copy(zero, x_hbm.at[pl.ds(pl.multiple_of(row0, sub), sub), :], zsem)

    def zero_fill(fn):
        for e in range(n_exp):
            lo_e = pend_ref[e - 1] if e > 0 else 0

            @pl.when(pend_ref[e] > lo_e)
            def _():
                fn(zero_copy(pend_ref[e] - sub))

        def body(blk, carry):
            fn(zero_copy(blk * sub))
            return carry

        lax.fori_loop(pend_ref[n_exp - 1] // sub, n_rows // sub, body, 0)

    @pl.when(i == 0)
    def _():
        zero[...] = jnp.zeros_like(zero)
        zero_fill(lambda cp: cp.start())
        zero_fill(lambda cp: cp.wait())

    def issue(u, carry):
        t = i * tb + u
        for kk in range(TOP_K):
            _row_copy(stage.at[slot], x_hbm, u, dest_ref[t * TOP_K + kk], sems.at[slot]).start(priority=kk % 2)
        return carry

    lax.fori_loop(0, tb, issue, 0, unroll=2)

    def drain_on(slot_):
        def body(u, carry):
            for kk in range(TOP_K):
                _row_copy(stage.at[slot_], x_hbm, 0, 0, sems.at[slot_]).wait()
            return carry

        lax.fori_loop(0, tb, body, 0, unroll=2)

    @pl.when(i > 0)
    def _():
        drain_on(1 - slot)

    @pl.when(i == pl.num_programs(0) - 1)
    def _():
        drain_on(slot)


def _dispatch(dest, pad_end, h2p, n_rows):
    m, w = h2p.shape
    tb = 256
    n_exp = pad_end.shape[0]
    return pl.pallas_call(
        functools.partial(_dispatch_kernel, tb=tb, n_exp=n_exp, sub=MOE_SUB, n_rows=n_rows),
        grid_spec=pltpu.PrefetchScalarGridSpec(
            num_scalar_prefetch=2,
            grid=(m // tb,),
            in_specs=[pl.BlockSpec((tb, w), lambda i, dst, pend: (i, 0))],
            out_specs=pl.BlockSpec(memory_space=pl.ANY),
            scratch_shapes=[pltpu.VMEM((2, tb, w), jnp.int32), pltpu.VMEM((MOE_SUB, w), jnp.int32),
                            pltpu.SemaphoreType.DMA(()), pltpu.SemaphoreType.DMA((2,))]),
        out_shape=jax.ShapeDtypeStruct((n_rows, w), jnp.int32),
        compiler_params=_cparams("arbitrary"),
        name="dispatch",
    )(dest, pad_end, h2p)


def _roll_lanes(v, shift):
    return jnp.concatenate([pltpu.roll(v[:, g * LANES:(g + 1) * LANES], shift, axis=1)
                            for g in range(v.shape[1] // LANES)], axis=1)


def _expert_kernel(e_ref, st_ref, ns_ref, tot_ref, x_hbm, wgu_ref, bgu_ref, wd_ref, bd_ref,
                   y_hbm, x_ref, acc, xw, wd_i, wd_b, act, sem, x_sems, *, n_ff, rmax, sub, n_rows):
    i = pl.program_id(0)
    j = pl.program_id(1)
    nsub = ns_ref[i]
    n_sub_max = rmax // sub
    tfc = wd_ref.shape[1]
    half = tfc // 2

    def out_copy(row0, s):
        return pltpu.make_async_copy(acc.at[pl.ds(s * sub, sub), :],
                                     y_hbm.at[pl.ds(pl.multiple_of(row0, sub), sub), :], sem)

    def for_valid_subs(item, fn):
        for s in range(n_sub_max):
            @pl.when(s < ns_ref[item])
            def _():
                fn(out_copy(st_ref[item] + s * sub, s))

    half_d = x_ref.shape[1] // 2

    def fetch(s, item=None):
        item = i if item is None else item
        row0 = pl.multiple_of(st_ref[item] + s * sub, sub)
        return pltpu.make_async_copy(x_hbm.at[pl.ds(row0, sub), :], xw.at[s % 2], x_sems.at[s % 2])

    def start_first_fetches(item):
        for s in range(2):
            @pl.when(s < ns_ref[item])
            def _():
                fetch(s, item).start()

    @pl.when((j == 0) & (i == 0))
    def _():
        start_first_fetches(i)

    @pl.when(nsub > 0)
    def _():
        for g in range(wd_b.shape[1] // LANES):
            cols = slice(g * LANES, (g + 1) * LANES)
            wd_i[g % 2, pl.ds(0, half, stride=2), :] = wd_ref[0, 0:half, cols]
            wd_i[g % 2, pl.ds(1, half, stride=2), :] = wd_ref[0, half:tfc, cols]
            wd_b[:, cols] = wd_i[g % 2].astype(BF16)

    @pl.when(j == 0)
    def _():
        for s in range(n_sub_max):
            @pl.when(s < nsub)
            def _():
                fetch(s).wait()
                words = xw[s % 2]
                rows = slice(s * sub, (s + 1) * sub)
                x_ref[rows, 0:half_d] = lax.bitcast_convert_type(words << 16, F32).astype(BF16)
                x_ref[rows, half_d:] = lax.bitcast_convert_type(words & jnp.int32(-65536), F32).astype(BF16)

                @pl.when(s + 2 < nsub)
                def _():
                    fetch(s + 2).start()

    cw = 2 * LANES

    def paired(gu, to_odd):
        gate = jnp.minimum(gu, SWIGLU_LIMIT)
        glu = gate * _sigmoid(gate * SWIGLU_ALPHA)
        up1 = jnp.clip(gu, -SWIGLU_LIMIT, SWIGLU_LIMIT) + 1.0
        if to_odd:
            return _roll_lanes(glu, 1) * up1
        return glu * _roll_lanes(up1, LANES - 1)

    def activate(row0, m_rows):
        rows = pl.ds(pl.multiple_of(row0, sub), m_rows)
        xm = x_ref[rows, :]
        even = (lax.broadcasted_iota(jnp.int32, (m_rows, cw), 1) % 2) == 0
        for c in range(tfc // cw):
            lo = slice(c * cw, (c + 1) * cw)
            hi = slice(tfc + c * cw, tfc + (c + 1) * cw)
            pa = paired(_dot(xm, wgu_ref[0, :, lo].astype(BF16)) + bgu_ref[0, :, lo], False)
            pb = paired(_dot(xm, wgu_ref[0, :, hi].astype(BF16)) + bgu_ref[0, :, hi], True)
            act[rows, lo] = jnp.where(even, pa, pb).astype(BF16)

    def project(row0, m_rows, first):
        rows = pl.ds(pl.multiple_of(row0, sub), m_rows)
        down = _dot(act[rows, :], wd_b[...])
        if first:
            acc[rows, :] = down + bd_ref[0]
        else:
            acc[rows, :] += down

    def per_row_count(fn):
        done = 0
        for width in (w for w in (8, 4, 2, 1) if w * sub <= 1024):
            count = (nsub - done) // width

            def body(b, carry, done=done, width=width):
                fn((done + b * width) * sub, width * sub)
                return carry

            lax.fori_loop(0, count, body, 0)
            done = done + count * width

    per_row_count(activate)

    @pl.when(j == 0)
    def _():
        @pl.when(i > 0)
        def _():
            for_valid_subs(i - 1, lambda cp: cp.wait())

        per_row_count(functools.partial(project, first=True))

    @pl.when(j > 0)
    def _():
        per_row_count(functools.partial(project, first=False))

    @pl.when(j == n_ff - 1)
    def _():
        for_valid_subs(i, lambda cp: cp.start())

        @pl.when(i + 1 < pl.num_programs(0))
        def _():
            start_first_fetches(i + 1)

        @pl.when(i == pl.num_programs(0) - 1)
        def _():
            for_valid_subs(i, lambda cp: cp.wait())
            acc[0:sub, :] = jnp.zeros((sub, acc.shape[1]), F32)
            first = tot_ref[0] // sub

            def fill_start(blk, carry):
                out_copy(blk * sub, 0).start()
                return carry

            def fill_wait(blk, carry):
                out_copy(blk * sub, 0).wait()
                return carry

            lax.fori_loop(first, n_rows // sub, fill_start, 0)
            lax.fori_loop(first, n_rows // sub, fill_wait, 0)


def _experts(item_e, item_start, item_nsub, total_rows, x_buf, w_gu, b_gu, w_down, b_down, n_rows):
    n_items = item_e.shape[0]
    n_exp, dff, d = w_down.shape
    tf = MOE_TF
    n_ff = dff // tf
    rmax = MOE_RMAX

    def jj(i, j, ns):
        return jnp.where(ns[i] > 0, j, n_ff - 1)

    in_specs = [
        pl.BlockSpec(memory_space=pl.ANY),
        pl.BlockSpec((1, d, 2 * tf), lambda i, j, e, st, ns, tot: (e[i], 0, jj(i, j, ns))),
        pl.BlockSpec((1, 1, 2 * tf), lambda i, j, e, st, ns, tot: (e[i], 0, jj(i, j, ns))),
        pl.BlockSpec((1, tf, d), lambda i, j, e, st, ns, tot: (e[i], jj(i, j, ns), 0)),
        pl.BlockSpec((1, 1, d), lambda i, j, e, st, ns, tot: (e[i], 0, 0)),
    ]
    return pl.pallas_call(
        functools.partial(_expert_kernel, n_ff=n_ff, rmax=rmax, sub=MOE_SUB, n_rows=n_rows),
        grid_spec=pltpu.PrefetchScalarGridSpec(
            num_scalar_prefetch=4,
            grid=(n_items, n_ff),
            in_specs=in_specs,
            out_specs=pl.BlockSpec(memory_space=pl.ANY),
            scratch_shapes=[pltpu.VMEM((rmax, d), BF16), pltpu.VMEM((rmax, d), F32),
                            pltpu.VMEM((2, MOE_SUB, d // 2), jnp.int32), pltpu.VMEM((2, tf, LANES), F32),
                            pltpu.VMEM((tf, d), BF16), pltpu.VMEM((rmax, tf), BF16),
                            pltpu.SemaphoreType.DMA(()), pltpu.SemaphoreType.DMA((2,))]),
        out_shape=jax.ShapeDtypeStruct((n_rows, d), F32),
        compiler_params=_cparams("arbitrary", "arbitrary"),
        name="experts",
    )(item_e, item_start, item_nsub, total_rows, x_buf, w_gu, b_gu.reshape(n_exp, 1, 2 * dff),
      w_down, b_down.reshape(n_exp, 1, d))


def _combine_kernel(dest_ref, y_hbm, prob_ref, x1_ref, g2_ref, lg_ref, lb_ref, o_ref, buf, sems,
                    *, tm, alpha):
    i = pl.program_id(0)
    slot = i % 2

    def issue_block(blk, slot_):
        def body(t, carry):
            for kk in range(TOP_K):
                _row_copy(y_hbm, buf.at[slot_, kk], dest_ref[(blk * tm + t) * TOP_K + kk], t,
                          sems.at[slot_]).start(priority=kk % 2)
            return carry

        lax.fori_loop(0, tm, body, 0, unroll=2)

    @pl.when(i == 0)
    def _():
        issue_block(0, 0)

    @pl.when(i + 1 < pl.num_programs(0))
    def _():
        issue_block(i + 1, 1 - slot)

    def drain(t, carry):
        for kk in range(TOP_K):
            _row_copy(y_hbm, buf.at[slot, kk], 0, t, sems.at[slot]).wait()
        return carry

    lax.fori_loop(0, tm, drain, 0, unroll=2)
    prob = prob_ref[...]
    f = prob[:, 0:1] * buf[slot, 0]
    for kk in range(1, TOP_K):
        f = f + prob[:, kk:kk + 1] * buf[slot, kk]
    xr = alpha * x1_ref[...] + g2_ref[0] * f
    m2 = jnp.mean(xr, axis=-1, keepdims=True)
    xc = xr - m2
    v2 = jnp.mean(xc * xc, axis=-1, keepdims=True)
    o_ref[...] = xc * lax.rsqrt(v2 + LN_EPS) * lg_ref[...] + lb_ref[...]


def _combine(dest, y_buf, prob, x1, g2, ln_g, ln_b, seq, alpha):
    m, d = x1.shape
    tm = 128
    tpb = seq // tm
    return pl.pallas_call(
        functools.partial(_combine_kernel, tm=tm, alpha=alpha),
        grid_spec=pltpu.PrefetchScalarGridSpec(
            num_scalar_prefetch=1,
            grid=(m // tm,),
            in_specs=[pl.BlockSpec(memory_space=pl.ANY),
                      pl.BlockSpec((tm, LANES), lambda i, dst: (i, 0)),
                      pl.BlockSpec((tm, d), lambda i, dst: (i, 0)),
                      pl.BlockSpec((1, 1, d), lambda i, dst: (i // tpb, 0, 0)),
                      pl.BlockSpec((1, d), lambda i, dst: (0, 0)),
                      pl.BlockSpec((1, d), lambda i, dst: (0, 0))],
            out_specs=pl.BlockSpec((tm, d), lambda i, dst: (i, 0)),
            scratch_shapes=[pltpu.VMEM((2, TOP_K, tm, d), F32), pltpu.SemaphoreType.DMA((2,))]),
        out_shape=jax.ShapeDtypeStruct((m, d), F32),
        compiler_params=_cparams("arbitrary"),
        name="combine_ln2",
    )(dest, y_buf, prob, x1, g2, ln_g.reshape(1, d), ln_b.reshape(1, d))


def _pad_rows(w, rows):
    return jnp.pad(w, ((0, rows - w.shape[0]), (0, 0)))


def _pad_cols(w, cols):
    return jnp.pad(w, ((0, 0), (0, cols - w.shape[1])))


def _layer(x2, c8, bsz, seq, alpha, w_ada, b_ada, w_in, mu_shift, w0, w_decay_up, a0, w_iclr_up,
           w_glora_up, k_k, k_a, r_k, lnx_g, lnx_b, w_rwkv_out, w_pool, pool_scale, w_out, ln1_g,
           ln1_b, w_router, b_router, w_gu, b_gu, w_down, b_down, ln2_g, ln2_b):
    m, d = x2.shape
    c = w0.shape[0]
    n_dec, n_icl, n_gate = w_decay_up.shape[0], w_iclr_up.shape[0], w_glora_up.shape[0]
    pw = w_pool.shape[0] * w_pool.shape[1]
    n_exp = w_router.shape[1]
    assert n_dec <= LANES and n_icl <= LANES and n_gate <= 2 * LANES and d == 2 * c and pw == c

    mod = _adaln(c8, w_ada, b_ada)[:bsz]
    sh1, sc1, g1, sh2, sc2, g2 = [t.reshape(bsz, 1, d) for t in jnp.split(mod, 6, axis=-1)]

    o = 3 * c
    segs = [(o, n_dec, LANES), (o + n_dec, n_icl, LANES), (o + n_dec + n_icl, n_gate, 2 * LANES)]
    n_shift = o + n_dec + n_icl + n_gate
    zpad = 2 * d - 3 * c - 4 * LANES
    w_shift = jnp.concatenate([w_in[:, :o]] + [_pad_cols(w_in[:, s:s + n], wd) for s, n, wd in segs]
                              + [jnp.zeros((d, zpad), F32)], axis=1)
    mu = jnp.concatenate([mu_shift[:o]] + [jnp.pad(mu_shift[s:s + n], (0, wd - n)) for s, n, wd in segs]
                         + [jnp.zeros((zpad,), F32)]).reshape(1, -1)
    w_cat = jnp.concatenate([w_in[:, n_shift + pw:], w_shift, w_in[:, n_shift:n_shift + pw]],
                            axis=1).astype(BF16)
    p = _inproj(x2, sc1, sh1, w_cat, seq)

    ones_bd = jnp.kron(jnp.eye(LANES // HEAD_SIZE, dtype=F32), jnp.ones((HEAD_SIZE, HEAD_SIZE), F32)).astype(BF16)
    r, lw, kmod, v, kkn, bvec, g = _prep(
        p, mu, w0, a0, k_k, k_a, _pad_rows(w_decay_up, LANES).astype(BF16),
        _pad_rows(w_iclr_up, LANES).astype(BF16), _pad_rows(w_glora_up, 2 * LANES).astype(BF16),
        ones_bd, seq, c, shift_blk=1)
    y = _wkv(r, lw, kmod, v, kkn, bvec, bsz, seq, c)

    w_r = _pad_cols(w_router, LANES).astype(BF16)
    b_r = jnp.pad(b_router, (0, LANES - n_exp))
    x1, h2, logits = _post(y, r, kmod, v, g, p, x2, g1, sc2, sh2, r_k.reshape(-1), lnx_g, lnx_b, ones_bd,
                           w_rwkv_out.astype(BF16), w_pool.astype(BF16), pool_scale, w_out.astype(BF16),
                           ln1_g, ln1_b, w_r, b_r, seq, alpha, gate_blk=0, pool_blk=(4 * d) // pw)

    idx, prob, rank, cnt = _route(logits, n_exp)
    counts = cnt[0, :n_exp].astype(jnp.int32)
    padded = (counts + MOE_SUB - 1) // MOE_SUB * MOE_SUB
    pad_end = jnp.cumsum(padded)
    pad_start = pad_end - padded
    mk = m * TOP_K
    n_rows = (mk // MOE_SUB + n_exp) * MOE_SUB
    dest = (pad_start[idx[:, :TOP_K]] + rank[:, :TOP_K]).reshape(mk).astype(jnp.int32)
    n_items = mk // MOE_RMAX + n_exp
    per_e = (padded + MOE_RMAX - 1) // MOE_RMAX
    it_end = jnp.cumsum(per_e)
    it_start = it_end - per_e
    ids = jnp.arange(n_items, dtype=jnp.int32)
    item_e = jnp.minimum(jnp.searchsorted(it_end, ids, side='right'), n_exp - 1).astype(jnp.int32)
    local = ids - it_start[item_e]
    used = ids < it_end[-1]
    item_start = jnp.where(used, pad_start[item_e] + local * MOE_RMAX, 0).astype(jnp.int32)
    item_nsub = jnp.where(used, jnp.clip((padded[item_e] - local * MOE_RMAX) // MOE_SUB, 0,
                                         MOE_RMAX // MOE_SUB), 0).astype(jnp.int32)

    x_buf = _dispatch(dest, pad_end.astype(jnp.int32), h2, n_rows)
    y_buf = _experts(item_e, item_start, item_nsub, pad_end[-1:].astype(jnp.int32), x_buf,
                     w_gu, b_gu, w_down, b_down, n_rows)
    return _combine(dest, y_buf, prob, x1, g2, ln2_g, ln2_b, seq, alpha)


def kernel(x, c, w_ada, b_ada, w_in, mu_shift, w0, w_decay_up, a0, w_iclr_up, w_glora_up, k_k, k_a, r_k, lnx_g, lnx_b, w_rwkv_out, w_pool, pool_scale, w_out, ln1_g, ln1_b, w_router, b_router, w_gu, b_gu, w_down, b_down, ln2_g, ln2_b):
    bsz, seq, d = x.shape
    depth = w_ada.shape[0]
    alpha = float((2 * depth) ** 0.25)
    c8 = jnp.pad(c, ((0, 8 - bsz), (0, 0)))
    x2 = x.reshape(bsz * seq, d)
    weights = (w_ada, b_ada, w_in, mu_shift, w0, w_decay_up, a0, w_iclr_up, w_glora_up, k_k, k_a, r_k,
               lnx_g, lnx_b, w_rwkv_out, w_pool, pool_scale, w_out, ln1_g, ln1_b, w_router, b_router,
               w_gu, b_gu, w_down, b_down, ln2_g, ln2_b)
    for l in range(depth):
        x2 = _layer(x2, c8, bsz, seq, alpha, *[w[l] for w in weights])
    return x2.reshape(bsz, seq, d)
```
